```python
import jax, jax.numpy as jnp
from jax import lax
import numpy as np

D_MODEL = 4096
BATCH = 4
SEQ = 2048
DEPTH = 2

N_EVEN = (DEPTH + 1) // 2
N_ODD = DEPTH // 2
NORM_EPS = 1e-6
BLOCK_Q = 128

MLA_HEADS = 16
MLA_Q_LORA = 1024
MLA_KV_LORA = 512
MLA_NOPE = 128
MLA_ROPE = 64
MLA_V = 128
ROPE_THETA = 10000.0
MLA_IN = MLA_Q_LORA + MLA_KV_LORA + MLA_ROPE

ML_HEADS = 4
ML_QK = 256
ML_V = 512
ML_CHUNK = 128
ML_IN = 2 * ML_HEADS * ML_QK + 2 * ML_HEADS * ML_V + 2 * ML_HEADS

SC_WIDTH = 2048
SC_KERNEL = 3
SC_IN = 3 * SC_WIDTH

SB_HEADS = 16
SB_HEAD_DIM = 128
SB_IN = 3 * SB_HEADS * SB_HEAD_DIM

EVEN_IN = MLA_IN + ML_IN
EVEN_MIX = MLA_HEADS * MLA_V + ML_HEADS * ML_V
ODD_IN = SC_IN + SB_IN
ODD_MIX = SC_WIDTH + SB_HEADS * SB_HEAD_DIM

D_FF = 11008
N_EXPERTS = 8
TOP_K = 2
D_FF_EXPERT = 4096

kernel_name = 'hybrid_mla_mlstm_shortconv_stickbreak_moe'


def rms_norm(x, g):
    xf = x.astype(jnp.float32)
    y = xf * lax.rsqrt(jnp.mean(xf * xf, axis=-1, keepdims=True) + NORM_EPS)
    return (y * g.astype(jnp.float32)).astype(x.dtype)


def rope(t, cos, sin):
    half = t.shape[-1] // 2
    t1, t2 = t[..., :half], t[..., half:]
    return jnp.concatenate([t1 * cos - t2 * sin, t2 * cos + t1 * sin], axis=-1)


def to_blocks(t, size):
    b, s = t.shape[:2]
    return jnp.swapaxes(t.reshape((b, s // size, size) + t.shape[2:]), 0, 1)


def from_blocks(t):
    nb, b, size = t.shape[:3]
    return jnp.swapaxes(t, 0, 1).reshape((b, nb * size) + t.shape[3:])


def mla_mixer(u, cos, sin, q_norm, w_uq, kv_norm, w_ukv):
    b, s, _ = u.shape
    c_q = u[..., :MLA_Q_LORA]
    c_kv = u[..., MLA_Q_LORA:MLA_Q_LORA + MLA_KV_LORA]
    k_rope = rope(u[..., MLA_Q_LORA + MLA_KV_LORA:], cos, sin)
    q = (rms_norm(c_q, q_norm) @ w_uq).reshape(b, s, MLA_HEADS, MLA_NOPE + MLA_ROPE)
    q_nope = q[..., :MLA_NOPE]
    q_rope = rope(q[..., MLA_NOPE:], cos[:, :, None, :], sin[:, :, None, :])
    kv = (rms_norm(c_kv, kv_norm) @ w_ukv).reshape(b, s, MLA_HEADS, MLA_NOPE + MLA_V)
    k_nope, v = kv[..., :MLA_NOPE], kv[..., MLA_NOPE:]
    scale = (MLA_NOPE + MLA_ROPE) ** -0.5
    k_pos = jnp.arange(s)

    def attend(args):
        qn, qr, blk = args
        scores = (jnp.einsum('bqhd,bkhd->bhqk', qn, k_nope)
                  + jnp.einsum('bqhr,bkr->bhqk', qr, k_rope)).astype(jnp.float32) * scale
        q_pos = blk * BLOCK_Q + jnp.arange(BLOCK_Q)
        causal = k_pos[None, :] <= q_pos[:, None]
        p = jax.nn.softmax(jnp.where(causal, scores, -jnp.inf), axis=-1)
        return jnp.einsum('bhqk,bkhd->bqhd', p.astype(v.dtype), v)

    o = lax.map(attend, (to_blocks(q_nope, BLOCK_Q), to_blocks(q_rope, BLOCK_Q), jnp.arange(s // BLOCK_Q)))
    return from_blocks(o).reshape(b, s, MLA_HEADS * MLA_V)


def mlstm_mixer(u, gate_bias, head_norm):
    b, s, _ = u.shape
    f32 = jnp.float32
    n_qk = ML_HEADS * ML_QK
    n_v = ML_HEADS * ML_V
    q = u[..., :n_qk].reshape(b, s, ML_HEADS, ML_QK).astype(f32)
    k = u[..., n_qk:2 * n_qk].reshape(b, s, ML_HEADS, ML_QK).astype(f32) * ML_QK ** -0.5
    v = u[..., 2 * n_qk:2 * n_qk + n_v].reshape(b, s, ML_HEADS, ML_V).astype(f32)
    o_gate = u[..., 2 * n_qk + n_v:2 * n_qk + 2 * n_v]
    gates = u[..., 2 * n_qk + 2 * n_v:].astype(f32) + gate_bias.astype(f32)
    i_pre = gates[..., :ML_HEADS]
    log_f = jax.nn.log_sigmoid(gates[..., ML_HEADS:])

    def chunks(t):
        return jnp.moveaxis(to_blocks(t, ML_CHUNK), 3, 2)

    tril = jnp.tril(jnp.ones((ML_CHUNK, ML_CHUNK), dtype=bool))

    def step(carry, inp):
        c_state, n_state, m_state = carry
        qc, kc, vc, ic, fc = inp
        cum_f = jnp.cumsum(fc, axis=-1)
        d_mat = jnp.where(tril, cum_f[..., :, None] - cum_f[..., None, :] + ic[..., None, :], -jnp.inf)
        m_inter = cum_f + m_state[..., None]
        m_t = jnp.maximum(m_inter, jnp.max(d_mat, axis=-1))
        inter_scale = jnp.exp(m_inter - m_t)
        sim = jnp.einsum('bhtd,bhsd->bhts', qc, kc) * jnp.exp(d_mat - m_t[..., None])
        num = (jnp.einsum('bhts,bhsv->bhtv', sim, vc)
               + inter_scale[..., None] * jnp.einsum('bhtd,bhvd->bhtv', qc, c_state))
        den = jnp.sum(sim, axis=-1) + inter_scale * jnp.einsum('bhtd,bhd->bht', qc, n_state)
        h = num / jnp.maximum(jnp.abs(den), jnp.exp(-m_t))[..., None]
        f_total = cum_f[..., -1]
        w_log = f_total[..., None] - cum_f + ic
        m_new = jnp.maximum(f_total + m_state, jnp.max(w_log, axis=-1))
        decay = jnp.exp(f_total + m_state - m_new)
        w = jnp.exp(w_log - m_new[..., None])
        c_new = decay[..., None, None] * c_state + jnp.einsum('bhs,bhsv,bhsd->bhvd', w, vc, kc)
        n_new = decay[..., None] * n_state + jnp.einsum('bhs,bhsd->bhd', w, kc)
        return (c_new, n_new, m_new), h

    init = (jnp.zeros((b, ML_HEADS, ML_V, ML_QK), f32),
            jnp.zeros((b, ML_HEADS, ML_QK), f32),
            jnp.zeros((b, ML_HEADS), f32))
    _, h = lax.scan(step, init, (chunks(q), chunks(k), chunks(v), chunks(i_pre), chunks(log_f)))
    h = from_blocks(jnp.moveaxis(h, 2, 3))
    h = h * lax.rsqrt(jnp.mean(h * h, axis=-1, keepdims=True) + NORM_EPS) * head_norm.reshape(ML_HEADS, ML_V).astype(f32)
    return (jax.nn.sigmoid(o_gate.astype(f32)) * h.reshape(b, s, n_v)).astype(u.dtype)


def short_conv_mixer(u, conv_w):
    b_gate = u[..., :SC_WIDTH]
    c_gate = u[..., SC_WIDTH:2 * SC_WIDTH]
    h = u[..., 2 * SC_WIDTH:]
    z = c_gate * h
    y = lax.conv_general_dilated(z, conv_w[:, None, :].astype(z.dtype), window_strides=(1,),
                                 padding=[(SC_KERNEL - 1, 0)], dimension_numbers=('NWC', 'WIO', 'NWC'),
                                 feature_group_count=SC_WIDTH)
    return b_gate * y


def stick_breaking_mixer(u):
    b, s, _ = u.shape
    w = SB_HEADS * SB_HEAD_DIM
    q = u[..., :w].reshape(b, s, SB_HEADS, SB_HEAD_DIM)
    k = u[..., w:2 * w].reshape(b, s, SB_HEADS, SB_HEAD_DIM)
    v = u[..., 2 * w:].reshape(b, s, SB_HEADS, SB_HEAD_DIM)
    scale = SB_HEAD_DIM ** -0.5
    k_pos = jnp.arange(s)

    def attend(args):
        qb, blk = args
        z = jnp.einsum('bqhd,bkhd->bhqk', qb, k).astype(jnp.float32) * scale
        q_pos = blk * BLOCK_Q + jnp.arange(BLOCK_Q)
        strict = k_pos[None, :] < q_pos[:, None]
        log_keep = jnp.where(strict, jax.nn.log_sigmoid(-z), 0.0)
        log_between = lax.cumsum(log_keep, axis=3, reverse=True) - log_keep
        a = jnp.where(strict, jnp.exp(jax.nn.log_sigmoid(z) + log_between), 0.0)
        return jnp.einsum('bhqk,bkhd->bqhd', a.astype(v.dtype), v)

    o = lax.map(attend, (to_blocks(q, BLOCK_Q), jnp.arange(s // BLOCK_Q)))
    return from_blocks(o).reshape(b, s, w)


def swiglu(h, w_gate, w_up, w_down):
    return (jax.nn.silu(h @ w_gate) * (h @ w_up)) @ w_down


def moe_swiglu(h, w_router, b_router, w_gate_e, w_up_e, w_down_e):
    logits = (h @ w_router).astype(jnp.float32) + b_router.astype(jnp.float32)
    top_val, top_idx = lax.top_k(logits, TOP_K)
    top_w = jax.nn.softmax(top_val, axis=-1)
    combine = jnp.sum(jax.nn.one_hot(top_idx, N_EXPERTS, dtype=jnp.float32) * top_w[..., None], axis=-2)
    out = jnp.zeros_like(h)
    for e in range(N_EXPERTS):
        out = out + combine[..., e:e + 1].astype(h.dtype) * swiglu(h, w_gate_e[e], w_up_e[e], w_down_e[e])
    return out


def setup_inputs(seed: int = 0) -> dict:
    key = jax.random.key(seed)
    ks = iter(jax.random.split(key, 40))
    f32 = jnp.float32

    def normal(shape):
        return jax.random.normal(next(ks), shape, f32)

    def dense(shape, fan_in):
        return normal(shape) * fan_in ** -0.5

    def gain(shape):
        return 1.0 + 0.02 * normal(shape)

    x = normal((BATCH, SEQ, D_MODEL))
    offsets = jax.random.randint(next(ks), (BATCH, 1), 0, 4096, dtype=jnp.int32)
    positions = offsets + jnp.arange(SEQ, dtype=jnp.int32)[None, :]
    inputs = {
        'x': x,
        'positions': positions,
        'even_attn_norm': gain((N_EVEN, D_MODEL)),
        'even_w_in': dense((N_EVEN, D_MODEL, EVEN_IN), D_MODEL),
        'even_q_norm': gain((N_EVEN, MLA_Q_LORA)),
        'even_w_uq': dense((N_EVEN, MLA_Q_LORA, MLA_HEADS * (MLA_NOPE + MLA_ROPE)), MLA_Q_LORA),
        'even_kv_norm': gain((N_EVEN, MLA_KV_LORA)),
        'even_w_ukv': dense((N_EVEN, MLA_KV_LORA, MLA_HEADS * (MLA_NOPE + MLA_V)), MLA_KV_LORA),
        'even_ml_gate_bias': jnp.concatenate(
            [0.1 * normal((N_EVEN, ML_HEADS)),
             jnp.linspace(3.0, 6.0, ML_HEADS, dtype=f32)[None, :] + 0.1 * normal((N_EVEN, ML_HEADS))], axis=-1),
        'even_ml_head_norm': gain((N_EVEN, ML_HEADS * ML_V)),
        'even_w_out': dense((N_EVEN, EVEN_MIX, D_MODEL), EVEN_MIX),
        'even_ffn_norm': gain((N_EVEN, D_MODEL)),
        'even_w_gate': dense((N_EVEN, D_MODEL, D_FF), D_MODEL),
        'even_w_up': dense((N_EVEN, D_MODEL, D_FF), D_MODEL),
        'even_w_down': dense((N_EVEN, D_FF, D_MODEL), D_FF),
        'odd_attn_norm': gain((N_ODD, D_MODEL)),
        'odd_w_in': dense((N_ODD, D_MODEL, ODD_IN), D_MODEL),
        'odd_conv_w': dense((N_ODD, SC_KERNEL, SC_WIDTH), SC_KERNEL),
        'odd_w_out': dense((N_ODD, ODD_MIX, D_MODEL), ODD_MIX),
        'odd_ffn_norm': gain((N_ODD, D_MODEL)),
        'odd_w_router': dense((N_ODD, D_MODEL, N_EXPERTS), D_MODEL),
        'odd_b_router': 0.01 * normal((N_ODD, N_EXPERTS)),
        'odd_w_gate_e': dense((N_ODD, N_EXPERTS, D_MODEL, D_FF_EXPERT), D_MODEL),
        'odd_w_up_e': dense((N_ODD, N_EXPERTS, D_MODEL, D_FF_EXPERT), D_MODEL),
        'odd_w_down_e': dense((N_ODD, N_EXPERTS, D_FF_EXPERT, D_MODEL), D_FF_EXPERT),
        'final_norm': gain((D_MODEL,)),
    }
    return inputs


def reference(x, positions, even_attn_norm, even_w_in, even_q_norm, even_w_uq, even_kv_norm, even_w_ukv,
              even_ml_gate_bias, even_ml_head_norm, even_w_out, even_ffn_norm, even_w_gate, even_w_up,
              even_w_down, odd_attn_norm, odd_w_in, odd_conv_w, odd_w_out, odd_ffn_norm, odd_w_router,
              odd_b_router, odd_w_gate_e, odd_w_up_e, odd_w_down_e, final_norm):
    inv_freq = 1.0 / (ROPE_THETA ** (jnp.arange(0, MLA_ROPE, 2, dtype=jnp.float32) / MLA_ROPE))
    angles = positions.astype(jnp.float32)[..., None] * inv_freq
    cos = jnp.cos(angles).astype(x.dtype)
    sin = jnp.sin(angles).astype(x.dtype)
    for layer in range(DEPTH):
        j = layer // 2
        if layer % 2 == 0:
            u = rms_norm(x, even_attn_norm[j]) @ even_w_in[j]
            y_a = mla_mixer(u[..., :MLA_IN], cos, sin, even_q_norm[j], even_w_uq[j], even_kv_norm[j], even_w_ukv[j])
            y_b = mlstm_mixer(u[..., MLA_IN:], even_ml_gate_bias[j], even_ml_head_norm[j])
            x = x + jnp.concatenate([y_a, y_b], axis=-1) @ even_w_out[j]
            x = x + swiglu(rms_norm(x, even_ffn_norm[j]), even_w_gate[j], even_w_up[j], even_w_down[j])
        else:
            u = rms_norm(x, odd_attn_norm[j]) @ odd_w_in[j]
            y_c = short_conv_mixer(u[..., :SC_IN], odd_conv_w[j])
            y_d = stick_breaking_mixer(u[..., SC_IN:])
            x = x + jnp.concatenate([y_c, y_d], axis=-1) @ odd_w_out[j]
            x = x + moe_swiglu(rms_norm(x, odd_ffn_norm[j]), odd_w_router[j], odd_b_router[j],
                               odd_w_gate_e[j], odd_w_up_e[j], odd_w_down_e[j])
    return rms_norm(x, final_norm)
```

```python
import functools

import jax
import jax.numpy as jnp
from jax import lax
from jax.experimental import pallas as pl
from jax.experimental.pallas import tpu as pltpu

F32 = jnp.float32
BF16 = jnp.bfloat16

D_MODEL = 4096
NORM_EPS = 1e-6

MLA_HEADS = 16
MLA_Q_LORA = 1024
MLA_KV_LORA = 512
MLA_NOPE = 128
MLA_ROPE = 64
MLA_V = 128
ROPE_THETA = 10000.0
MLA_IN = MLA_Q_LORA + MLA_KV_LORA + MLA_ROPE
MLA_QK_PAD = 256

ML_HEADS = 4
ML_QK = 256
ML_V = 512
ML_CHUNK = 128
ML_VA = ML_V + 128

SC_WIDTH = 2048
SC_KERNEL = 3

SB_HEADS = 16
SB_HEAD_DIM = 128

D_FF = 11008
N_EXPERTS = 8
TOP_K = 2
D_FF_EXPERT = 4096

LANES = 128
MOE_TM = 512


def _cp(sem, vmem_mb):
    return pltpu.CompilerParams(dimension_semantics=sem, vmem_limit_bytes=vmem_mb * 1024 * 1024)


def _dot(a, b):
    return jnp.dot(a, b, preferred_element_type=F32)


def _dot_nt(a, b):
    return lax.dot_general(a, b, (((1,), (1,)), ((), ())), preferred_element_type=F32)


def _log_sigmoid(x):
    return jnp.minimum(x, 0.0) - jnp.log1p(jnp.exp(-jnp.abs(x)))


def _split_dot(tri, x, pieces):
    acc = None
    r = x
    for p in range(pieces):
        hi = r.astype(BF16)
        part = _dot(tri, hi)
        acc = part if acc is None else acc + part
        if p + 1 < pieces:
            r = r - hi.astype(F32)
    return acc


def _rmsnorm_body(x_ref, g_ref, o_ref):
    x = x_ref[...].astype(F32)
    y = x * lax.rsqrt(jnp.mean(x * x, axis=-1, keepdims=True) + NORM_EPS)
    o_ref[...] = (y * g_ref[...]).astype(o_ref.dtype)


def _rmsnorm(x, g, *, width, col_block=0, out_dtype=BF16, tm=256):
    m = x.shape[0]
    return pl.pallas_call(
        _rmsnorm_body,
        grid=(m // tm,),
        in_specs=[pl.BlockSpec((tm, width), lambda i: (i, col_block)),
                  pl.BlockSpec((1, width), lambda i: (0, 0))],
        out_specs=pl.BlockSpec((tm, width), lambda i: (i, 0)),
        out_shape=jax.ShapeDtypeStruct((m, width), out_dtype),
        compiler_params=_cp(("parallel",), 40),
        name="rmsnorm",
    )(x, g.reshape(1, width).astype(F32))


def _mm_body(*refs, n_lhs, n_acc, n_extra, nk, k_valid_last, epilogue):
    lhs = refs[:n_lhs]
    ws = refs[n_lhs:n_lhs + n_acc * n_lhs]
    extras = refs[n_lhs + n_acc * n_lhs:n_lhs + n_acc * n_lhs + n_extra]
    out = refs[n_lhs + n_acc * n_lhs + n_extra]
    accs = refs[n_lhs + n_acc * n_lhs + n_extra + 1:]
    k = pl.program_id(2)

    def partials(mask_tail):
        parts = []
        for a in range(n_acc):
            s = None
            for l in range(n_lhs):
                x = lhs[l][...]
                w = ws[a * n_lhs + l][...]
                if mask_tail:
                    xc = lax.broadcasted_iota(jnp.int32, x.shape, 1)
                    x = jnp.where(xc < k_valid_last, x, jnp.zeros_like(x))
                    wr = lax.broadcasted_iota(jnp.int32, w.shape, 0)
                    w = jnp.where(wr < k_valid_last, w, jnp.zeros_like(w))
                p = _dot(x.astype(BF16), w.astype(BF16))
                s = p if s is None else s + p
            parts.append(s)
        return parts

    if nk == 1:
        out[...] = epilogue(partials(False), extras).astype(out.dtype)
        return

    ragged = k_valid_last is not None

    @pl.when(k == 0)
    def _():
        for a, p in enumerate(partials(False)):
            accs[a][...] = p

    @pl.when((k > 0) & (k < nk - 1) if ragged else (k > 0))
    def _():
        for a, p in enumerate(partials(False)):
            accs[a][...] += p

    if ragged:
        @pl.when(k == nk - 1)
        def _():
            for a, p in enumerate(partials(True)):
                accs[a][...] += p

    @pl.when(k == nk - 1)
    def _():
        out[...] = epilogue([acc[...] for acc in accs], extras).astype(out.dtype)


def _matmul(lhs, weights, *, n_out, tm, tn, out_dtype, epilogue, extras=(), tk=None,
            weight_stationary=True, vmem_mb=48, name="matmul"):
    m = lhs[0].shape[0]
    kdim = lhs[0].shape[1]
    tm = min(tm, m)
    tk = kdim if tk is None else tk
    nk = pl.cdiv(kdim, tk)
    k_valid_last = None if kdim % tk == 0 else kdim - (nk - 1) * tk
    n_lhs, n_acc = len(lhs), len(weights)
    if weight_stationary:
        grid = (n_out // tn, m // tm, nk)
        ij = lambda g0, g1: (g1, g0)
    else:
        grid = (m // tm, n_out // tn, nk)
        ij = lambda g0, g1: (g0, g1)

    in_specs, args = [], []
    for x in lhs:
        in_specs.append(pl.BlockSpec((tm, tk), lambda g0, g1, k: (ij(g0, g1)[0], k)))
        args.append(x)
    for wl in weights:
        for (w, rb) in wl:
            in_specs.append(pl.BlockSpec((tk, tn), lambda g0, g1, k, rb=rb: (rb * nk + k, ij(g0, g1)[1])))
            args.append(w)
    for (arr, bshape, imap) in extras:
        in_specs.append(pl.BlockSpec(bshape, lambda g0, g1, k, imap=imap: imap(*ij(g0, g1))))
        args.append(arr)
    scratch = [pltpu.VMEM((tm, tn), F32) for _ in range(n_acc)] if nk > 1 else []
    body = functools.partial(_mm_body, n_lhs=n_lhs, n_acc=n_acc, n_extra=len(extras), nk=nk,
                             k_valid_last=k_valid_last, epilogue=epilogue)
    return pl.pallas_call(
        body,
        grid=grid,
        in_specs=in_specs,
        out_specs=pl.BlockSpec((tm, tn), lambda g0, g1, k: ij(g0, g1)),
        out_shape=jax.ShapeDtypeStruct((m, n_out), out_dtype),
        scratch_shapes=scratch,
        compiler_params=_cp(("parallel", "parallel", "arbitrary"), vmem_mb),
        name=name,
    )(*args)


def _epi_plain(parts, extras):
    return parts[0]


def _epi_residual(parts, extras):
    return parts[0] + extras[0][...]


def _epi_swiglu(parts, extras):
    g, u = parts
    return g * jax.nn.sigmoid(g) * u


def _rope_tables_body(pos_ref, invf_ref, c_ref, s1_ref, s2_ref):
    ang = pos_ref[...] * invf_ref[...]
    lane = lax.broadcasted_iota(jnp.int32, ang.shape, 1)
    cos = jnp.cos(ang)
    sin = jnp.sin(ang)
    half = MLA_ROPE // 2
    c_ref[...] = jnp.where(lane < MLA_ROPE, cos, 0.0)
    s1_ref[...] = jnp.where(lane < half, -sin, 0.0)
    s2_ref[...] = jnp.where((lane >= half) & (lane < MLA_ROPE), sin, 0.0)


def _rope_tables(positions):
    t = positions.size
    half = MLA_ROPE // 2
    inv_freq = 1.0 / (ROPE_THETA ** (jnp.arange(0, MLA_ROPE, 2, dtype=F32) / MLA_ROPE))
    invf = jnp.concatenate([inv_freq, inv_freq, jnp.zeros((LANES - 2 * half,), F32)]).reshape(1, LANES)
    pos = jnp.broadcast_to(positions.reshape(t, 1).astype(F32), (t, LANES))
    tm = min(512, t)
    spec = pl.BlockSpec((tm, LANES), lambda i: (i, 0))
    sds = jax.ShapeDtypeStruct((t, LANES), F32)
    return pl.pallas_call(
        _rope_tables_body,
        grid=(t // tm,),
        in_specs=[spec, pl.BlockSpec((1, LANES), lambda i: (0, 0))],
        out_specs=[spec, spec, spec],
        out_shape=[sds, sds, sds],
        compiler_params=_cp(("parallel",), 32),
        name="rope_tables",
    )(pos, invf)


def _rope_lanes(t, c, s1, s2):
    return t * c + pltpu.roll(t, LANES - MLA_ROPE // 2, axis=1) * s1 + pltpu.roll(t, MLA_ROPE // 2, axis=1) * s2


def _epi_mla_q(parts, extras):
    acc = parts[0]
    c, s1, s2 = extras[0][...], extras[1][...], extras[2][...]
    scale = (MLA_NOPE + MLA_ROPE) ** -0.5
    pieces = []
    for h in range(acc.shape[1] // MLA_QK_PAD):
        base = h * MLA_QK_PAD
        pieces.append(acc[:, base:base + MLA_NOPE] * scale)
        pieces.append(_rope_lanes(acc[:, base + MLA_NOPE:base + MLA_QK_PAD], c, s1, s2) * scale)
    return jnp.concatenate(pieces, axis=1)


def _epi_mla_k(parts, extras):
    acc = parts[0]
    kr = _rope_lanes(extras[0][...], extras[1][...], extras[2][...], extras[3][...])
    pieces = []
    for h in range(acc.shape[1] // MLA_QK_PAD):
        base = h * MLA_QK_PAD
        pieces.append(acc[:, base:base + MLA_NOPE])
        pieces.append(acc[:, base + MLA_NOPE:base + MLA_QK_PAD] + kr)
    return jnp.concatenate(pieces, axis=1)


def _mla_attn_body(q_ref, k_ref, v_ref, o_ref, m_sc, l_sc, acc_sc, *, tq, tk):
    qi = pl.program_id(2)
    q = q_ref[...]
    m_sc[...] = jnp.full(m_sc.shape, -jnp.inf, F32)
    l_sc[...] = jnp.zeros(l_sc.shape, F32)
    acc_sc[...] = jnp.zeros(acc_sc.shape, F32)

    def block(j, masked):
        start = pl.multiple_of(j * tk, tk)
        k = k_ref[pl.ds(start, tk), :]
        v = v_ref[pl.ds(start, tk), :]
        s = _dot_nt(q, k)
        if masked:
            row = lax.broadcasted_iota(jnp.int32, s.shape, 0) + qi * tq
            col = lax.broadcasted_iota(jnp.int32, s.shape, 1) + j * tk
            s = jnp.where(col <= row, s, -jnp.inf)
        m_prev = m_sc[...]
        m_new = jnp.maximum(m_prev, jnp.max(s, axis=1, keepdims=True))
        alpha = jnp.exp(m_prev - m_new)
        p = jnp.exp(s - m_new)
        l_sc[...] = alpha * l_sc[...] + jnp.sum(p, axis=1, keepdims=True)
        acc_sc[...] = alpha * acc_sc[...] + _dot(p.astype(BF16), v)
        m_sc[...] = m_new

    ratio = tq // tk
    for d in range(ratio):
        block(qi * ratio + d, True)

    def body(j, carry):
        block(j, False)
        return carry

    lax.fori_loop(0, qi * ratio, body, 0)
    o_ref[...] = (acc_sc[...] / l_sc[...]).astype(o_ref.dtype)


def _mla_attention(q, k, v, batch, seq, *, tq=256, tk=256):
    t = q.shape[0]
    nq = seq // tq
    body = functools.partial(_mla_attn_body, tq=tq, tk=tk)
    return pl.pallas_call(
        body,
        grid=(batch, MLA_HEADS, nq),
        in_specs=[pl.BlockSpec((tq, MLA_QK_PAD), lambda b, h, i: (b * nq + i, h)),
                  pl.BlockSpec((seq, MLA_QK_PAD), lambda b, h, i: (b, h)),
                  pl.BlockSpec((seq, MLA_V), lambda b, h, i: (b, h))],
        out_specs=pl.BlockSpec((tq, MLA_V), lambda b, h, i: (b * nq + i, h)),
        out_shape=jax.ShapeDtypeStruct((t, MLA_HEADS * MLA_V), BF16),
        scratch_shapes=[pltpu.VMEM((tq, 1), F32), pltpu.VMEM((tq, 1), F32), pltpu.VMEM((tq, MLA_V), F32)],
        compiler_params=_cp(("parallel", "parallel", "arbitrary"), 32),
        name="mla_attention",
    )(q, k, v)


def _mlstm_body(q_ref, k_ref, v_ref, o_ref, g_ref, bias_ref, hn_ref, y_ref, ct_ref, m_ref):
    L = ML_CHUNK
    c = pl.program_id(1)

    @pl.when(c == 0)
    def _():
        ct_ref[...] = jnp.zeros(ct_ref.shape, F32)
        m_ref[...] = jnp.zeros(m_ref.shape, F32)

    g = g_ref[...] + bias_ref[...]
    logf = _log_sigmoid(g)
    row = lax.broadcasted_iota(jnp.int32, (L, L), 0)
    col = lax.broadcasted_iota(jnp.int32, (L, L), 1)
    causal = col <= row
    tri = jnp.where(causal, 1.0, 0.0).astype(BF16)
    cum = _split_dot(tri, logf, 3)
    lane = lax.broadcasted_iota(jnp.int32, (L, LANES), 1)
    colq = jnp.where(lane < ML_HEADS, g, cum)
    rowq = colq.T
    ones_col = jnp.where(lane == 0, 1.0, 0.0).astype(BF16)
    scale = ML_QK ** -0.5

    for h in range(ML_HEADS):
        i_c = colq[:, h:h + 1]
        cf_c = colq[:, ML_HEADS + h:ML_HEADS + h + 1]
        i_r = rowq[h:h + 1, :]
        cf_r = rowq[ML_HEADS + h:ML_HEADS + h + 1, :]
        m_prev = m_ref[h][:, 0:1]
        d = jnp.where(causal, cf_c - cf_r + i_r, -jnp.inf)
        m_inter = cf_c + m_prev
        m_t = jnp.maximum(m_inter, jnp.max(d, axis=1, keepdims=True))
        inter = jnp.exp(m_inter - m_t)
        p = jnp.exp(d - m_t)
        qh = q_ref[:, h * ML_QK:(h + 1) * ML_QK]
        kf = k_ref[:, h * ML_QK:(h + 1) * ML_QK].astype(F32) * scale
        kh = kf.astype(BF16)
        s = _dot_nt(qh, kh) * p
        va = jnp.concatenate([v_ref[:, h * ML_V:(h + 1) * ML_V], ones_col], axis=1)
        ct = ct_ref[h]
        na = _dot(s.astype(BF16), va) + inter * _dot(qh, ct.astype(BF16))
        num = na[:, :ML_V]
        den = na[:, ML_V:ML_V + 1]
        hh = num / jnp.maximum(jnp.abs(den), jnp.exp(-m_t))

        f_tot = cf_c[L - 1:L, :]
        w_log = f_tot - cf_c + i_c
        m_new = jnp.maximum(f_tot + m_prev, jnp.max(w_log, axis=0, keepdims=True))
        decay = jnp.exp(f_tot + m_prev - m_new)
        w = jnp.exp(w_log - m_new)
        wv = (va.astype(F32) * w).astype(BF16)
        ct_ref[h] = decay * ct + _dot(kf.T.astype(BF16), wv)
        m_ref[h] = jnp.broadcast_to(m_new, (1, LANES))

        hn = hh * lax.rsqrt(jnp.mean(hh * hh, axis=1, keepdims=True) + NORM_EPS) * hn_ref[:, h * ML_V:(h + 1) * ML_V]
        og = o_ref[:, h * ML_V:(h + 1) * ML_V].astype(F32)
        y_ref[:, h * ML_V:(h + 1) * ML_V] = (jax.nn.sigmoid(og) * hn).astype(y_ref.dtype)


def _mlstm(u_b, gates_src, gate_col_block, bias_pad, head_norm, batch, seq):
    t = u_b.shape[0]
    nc = seq // ML_CHUNK
    nqk = ML_HEADS * ML_QK
    nv = ML_HEADS * ML_V
    rows = lambda b, c: b * nc + c
    return pl.pallas_call(
        _mlstm_body,
        grid=(batch, nc),
        in_specs=[pl.BlockSpec((ML_CHUNK, nqk), lambda b, c: (rows(b, c), 0)),
                  pl.BlockSpec((ML_CHUNK, nqk), lambda b, c: (rows(b, c), 1)),
                  pl.BlockSpec((ML_CHUNK, nv), lambda b, c: (rows(b, c), 1)),
                  pl.BlockSpec((ML_CHUNK, nv), lambda b, c: (rows(b, c), 2)),
                  pl.BlockSpec((ML_CHUNK, LANES), lambda b, c: (rows(b, c), gate_col_block)),
                  pl.BlockSpec((1, LANES), lambda b, c: (0, 0)),
                  pl.BlockSpec((1, nv), lambda b, c: (0, 0))],
        out_specs=pl.BlockSpec((ML_CHUNK, nv), lambda b, c: (rows(b, c), 0)),
        out_shape=jax.ShapeDtypeStruct((t, nv), BF16),
        scratch_shapes=[pltpu.VMEM((ML_HEADS, ML_QK, ML_VA), F32), pltpu.VMEM((ML_HEADS, 1, LANES), F32)],
        compiler_params=_cp(("parallel", "arbitrary"), 32),
        name="mlstm",
    )(u_b, u_b, u_b, u_b, gates_src, bias_pad, head_norm.reshape(1, nv).astype(F32))


def _conv_body(b_ref, c_ref, h_ref, cp_ref, hp_ref, w_ref, y_ref, z_sc, *, ts, halo):
    i = pl.program_id(1)
    z = c_ref[...].astype(F32) * h_ref[...].astype(F32)
    zp = cp_ref[...].astype(F32) * hp_ref[...].astype(F32)
    z_sc[0:halo, :] = jnp.where(i > 0, zp, 0.0)
    z_sc[halo:halo + ts, :] = z
    w = w_ref[...]
    y = w[2:3, :] * z + w[1:2, :] * z_sc[halo - 1:halo - 1 + ts, :] + w[0:1, :] * z_sc[halo - 2:halo - 2 + ts, :]
    y_ref[...] = (b_ref[...].astype(F32) * y).astype(y_ref.dtype)


def _short_conv(u, conv_w, batch, seq, *, ts=256, tw=1024):
    t = u.shape[0]
    halo = 16
    ns = seq // ts
    nw = SC_WIDTH // tw
    wpad = jnp.concatenate([conv_w.astype(F32), jnp.zeros((8 - SC_KERNEL, SC_WIDTH), F32)], axis=0)
    rows = lambda b, i: b * ns + i
    prev = lambda b, i: jnp.maximum((b * seq + i * ts) // halo - 1, 0)
    body = functools.partial(_conv_body, ts=ts, halo=halo)
    return pl.pallas_call(
        body,
        grid=(batch, ns, nw),
        in_specs=[pl.BlockSpec((ts, tw), lambda b, i, j: (rows(b, i), j)),
                  pl.BlockSpec((ts, tw), lambda b, i, j: (rows(b, i), nw + j)),
                  pl.BlockSpec((ts, tw), lambda b, i, j: (rows(b, i), 2 * nw + j)),
                  pl.BlockSpec((halo, tw), lambda b, i, j: (prev(b, i), nw + j)),
                  pl.BlockSpec((halo, tw), lambda b, i, j: (prev(b, i), 2 * nw + j)),
                  pl.BlockSpec((8, tw), lambda b, i, j: (0, j))],
        out_specs=pl.BlockSpec((ts, tw), lambda b, i, j: (rows(b, i), j)),
        out_shape=jax.ShapeDtypeStruct((t, SC_WIDTH), BF16),
        scratch_shapes=[pltpu.VMEM((halo + ts, tw), F32)],
        compiler_params=_cp(("parallel", "parallel", "parallel"), 32),
        name="short_conv",
    )(u, u, u, u, u, wpad)


def _sb_body(q_ref, k_ref, v_ref, o_ref, r_sc, acc_sc, *, tq, tk):
    qi = pl.program_id(2)
    q = q_ref[...]
    scale = SB_HEAD_DIM ** -0.5
    r_sc[...] = jnp.zeros(r_sc.shape, F32)
    acc_sc[...] = jnp.zeros(acc_sc.shape, F32)
    jr = lax.broadcasted_iota(jnp.int32, (tk, tk), 0)
    sc = lax.broadcasted_iota(jnp.int32, (tk, tk), 1)
    upper = jnp.where(jr > sc, 1.0, 0.0).astype(BF16)

    def block(j, masked):
        start = pl.multiple_of(j * tk, tk)
        k = k_ref[pl.ds(start, tk), :]
        v = v_ref[pl.ds(start, tk), :]
        z = _dot_nt(q, k) * scale
        sp = jnp.maximum(z, 0.0) + jnp.log1p(jnp.exp(-jnp.abs(z)))
        lk = -sp
        if masked:
            row = lax.broadcasted_iota(jnp.int32, z.shape, 0) + qi * tq
            col = lax.broadcasted_iota(jnp.int32, z.shape, 1) + j * tk
            strict = col < row
            lk = jnp.where(strict, lk, 0.0)
        between = _split_dot_rhs(lk, upper, 2)
        e = jnp.exp(z - sp + between + r_sc[...])
        if masked:
            e = jnp.where(strict, e, 0.0)
        acc_sc[...] += _dot(e.astype(BF16), v)
        r_sc[...] += between[:, 0:1] + lk[:, 0:1]

    ratio = tq // tk
    for d in range(ratio - 1, -1, -1):
        block(qi * ratio + d, True)

    def body(n, carry):
        block(qi * ratio - 1 - n, False)
        return carry

    lax.fori_loop(0, qi * ratio, body, 0)
    o_ref[...] = acc_sc[...].astype(o_ref.dtype)


def _split_dot_rhs(x, tri, pieces):
    acc = None
    r = x
    for p in range(pieces):
        hi = r.astype(BF16)
        part = _dot(hi, tri)
        acc = part if acc is None else acc + part
        if p + 1 < pieces:
            r = r - hi.astype(F32)
    return acc


def _sb_attention(u, col_base, batch, seq, *, tq=256, tk=256):
    t = u.shape[0]
    nq = seq // tq
    body = functools.partial(_sb_body, tq=tq, tk=tk)
    return pl.pallas_call(
        body,
        grid=(batch, SB_HEADS, nq),
        in_specs=[pl.BlockSpec((tq, SB_HEAD_DIM), lambda b, h, i: (b * nq + i, col_base + h)),
                  pl.BlockSpec((seq, SB_HEAD_DIM), lambda b, h, i: (b, col_base + SB_HEADS + h)),
                  pl.BlockSpec((seq, SB_HEAD_DIM), lambda b, h, i: (b, col_base + 2 * SB_HEADS + h))],
        out_specs=pl.BlockSpec((tq, SB_HEAD_DIM), lambda b, h, i: (b * nq + i, h)),
        out_shape=jax.ShapeDtypeStruct((t, SB_HEADS * SB_HEAD_DIM), BF16),
        scratch_shapes=[pltpu.VMEM((tq, 1), F32), pltpu.VMEM((tq, SB_HEAD_DIM), F32)],
        compiler_params=_cp(("parallel", "parallel", "arbitrary"), 32),
        name="sb_attention",
    )(u, u, u)


def _route_body(x_ref, g_ref, wr_ref, br_ref, h_ref, info_ref, cnt_ref, carry):
    i = pl.program_id(0)

    @pl.when(i == 0)
    def _():
        carry[...] = jnp.zeros(carry.shape, F32)

    x = x_ref[...]
    h = x * lax.rsqrt(jnp.mean(x * x, axis=-1, keepdims=True) + NORM_EPS) * g_ref[...]
    h_ref[...] = h
    wr = wr_ref[...]
    h_hi = h.astype(BF16)
    h_lo = (h - h_hi.astype(F32)).astype(BF16)
    w_hi = wr.astype(BF16)
    w_lo = (wr - w_hi.astype(F32)).astype(BF16)
    logits = (_dot(h_hi, w_hi) + _dot(h_hi, w_lo)) + (_dot(h_lo, w_hi) + _dot(h_lo, w_lo)) + br_ref[...]
    tm = logits.shape[0]
    lane = lax.broadcasted_iota(jnp.int32, logits.shape, 1).astype(F32)
    lg = jnp.where(lane < N_EXPERTS, logits, -jnp.inf)
    m1 = jnp.max(lg, axis=1, keepdims=True)
    i1 = jnp.min(jnp.where(lg == m1, lane, float(LANES)), axis=1, keepdims=True)
    lg2 = jnp.where(lane == i1, -jnp.inf, lg)
    m2 = jnp.max(lg2, axis=1, keepdims=True)
    i2 = jnp.min(jnp.where(lg2 == m2, lane, float(LANES)), axis=1, keepdims=True)
    e = jnp.exp(m2 - m1)
    w1 = 1.0 / (1.0 + e)
    w2 = e / (1.0 + e)
    oh1 = lane == i1
    oh2 = lane == i2
    mask = jnp.where(oh1 | oh2, 1.0, 0.0)
    row = lax.broadcasted_iota(jnp.int32, (tm, tm), 0)
    col = lax.broadcasted_iota(jnp.int32, (tm, tm), 1)
    before = jnp.where(col < row, 1.0, 0.0).astype(BF16)
    rank_mat = _dot(before, mask.astype(BF16)) + carry[...]
    r1 = jnp.sum(jnp.where(oh1, rank_mat, 0.0), axis=1, keepdims=True)
    r2 = jnp.sum(jnp.where(oh2, rank_mat, 0.0), axis=1, keepdims=True)
    carry[...] += jnp.sum(mask, axis=0, keepdims=True)
    info = jnp.where(lane == 0, i1,
           jnp.where(lane == 1, i2,
           jnp.where(lane == 2, r1,
           jnp.where(lane == 3, r2,
           jnp.where(lane == 4, w1,
           jnp.where(lane == 5, w2, 0.0))))))
    info_ref[...] = info
    cnt_ref[...] = jnp.broadcast_to(carry[...], cnt_ref.shape)


def _route(x, g, w_router, b_router, *, tm=256):
    t, d = x.shape
    wr = jnp.concatenate([w_router.astype(F32), jnp.zeros((d, LANES - N_EXPERTS), F32)], axis=1)
    br = jnp.concatenate([b_router.astype(F32), jnp.zeros((LANES - N_EXPERTS,), F32)]).reshape(1, LANES)
    return pl.pallas_call(
        _route_body,
        grid=(t // tm,),
        in_specs=[pl.BlockSpec((tm, d), lambda i: (i, 0)),
                  pl.BlockSpec((1, d), lambda i: (0, 0)),
                  pl.BlockSpec((d, LANES), lambda i: (0, 0)),
                  pl.BlockSpec((1, LANES), lambda i: (0, 0))],
        out_specs=[pl.BlockSpec((tm, d), lambda i: (i, 0)),
                   pl.BlockSpec((tm, LANES), lambda i: (i, 0)),
                   pl.BlockSpec((8, LANES), lambda i: (0, 0))],
        out_shape=[jax.ShapeDtypeStruct((t, d), F32),
                   jax.ShapeDtypeStruct((t, LANES), F32),
                   jax.ShapeDtypeStruct((8, LANES), F32)],
        scratch_shapes=[pltpu.VMEM((1, LANES), F32)],
        compiler_params=_cp(("arbitrary",), 40),
        name="moe_route",
    )(x, g.reshape(1, d).astype(F32), wr, br)


def _invperm_body(pos_ref, inv_ref, *, n_assign, n_slots):
    def clear(p, c):
        inv_ref[p] = 0
        return c

    lax.fori_loop(0, n_slots, clear, 0)

    def put(a, c):
        inv_ref[pos_ref[a]] = lax.shift_right_logical(a, TOP_K.bit_length() - 1)
        return c

    lax.fori_loop(0, n_assign, put, 0)


def _invperm(pos_flat, n_slots):
    n_assign = pos_flat.shape[0]
    return pl.pallas_call(
        functools.partial(_invperm_body, n_assign=n_assign, n_slots=n_slots),
        in_specs=[pl.BlockSpec(memory_space=pltpu.SMEM)],
        out_specs=pl.BlockSpec(memory_space=pltpu.SMEM),
        out_shape=jax.ShapeDtypeStruct((n_slots,), jnp.int32),
        name="moe_invperm",
    )(pos_flat)


def _row_copy(src_hbm, row, dst_vmem, slot, sem):
    return pltpu.make_async_copy(src_hbm.at[pl.ds(row, 1), :], dst_vmem.at[pl.ds(slot, 1), :], sem)


def _dispatch_body(inv_ref, nact_ref, h_hbm, o_ref, buf, sem, *, tm):
    i = pl.program_id(0)

    @pl.when(i < nact_ref[0])
    def _():
        def issue(r, c):
            _row_copy(h_hbm, inv_ref[i * tm + r], buf, r, sem).start()
            return c

        lax.fori_loop(0, tm, issue, 0)

        def wait(r, c):
            _row_copy(h_hbm, 0, buf, r, sem).wait()
            return c

        lax.fori_loop(0, tm, wait, 0)
        o_ref[...] = buf[...].astype(o_ref.dtype)

    @pl.when(i >= nact_ref[0])
    def _():
        o_ref[...] = jnp.zeros(o_ref.shape, o_ref.dtype)


def _dispatch(h, inv, n_active_tiles, n_tiles, *, tm):
    d = h.shape[1]
    grid_spec = pltpu.PrefetchScalarGridSpec(
        num_scalar_prefetch=2,
        grid=(n_tiles,),
        in_specs=[pl.BlockSpec(memory_space=pl.ANY)],
        out_specs=pl.BlockSpec((tm, d), lambda i, inv, na: (i, 0)),
        scratch_shapes=[pltpu.VMEM((tm, d), F32), pltpu.SemaphoreType.DMA(())],
    )
    return pl.pallas_call(
        functools.partial(_dispatch_body, tm=tm),
        grid_spec=grid_spec,
        out_shape=jax.ShapeDtypeStruct((n_tiles * tm, d), BF16),
        compiler_params=_cp(("arbitrary",), 40),
        name="moe_dispatch",
    )(inv, n_active_tiles, h)


def _gmm_body(rt_ref, e_ref, jo_ref, jw_ref, first_ref, act_ref, x_ref, *refs, n_acc, epilogue):
    ws = refs[:n_acc]
    out = refs[n_acc]
    wbf = refs[n_acc + 1:]
    w = pl.program_id(0)

    @pl.when(first_ref[w] == 1)
    def _():
        for a in range(n_acc):
            wbf[a][...] = ws[a][...].astype(BF16)

    @pl.when(act_ref[w] == 1)
    def _():
        x = x_ref[...].astype(BF16)
        out[...] = epilogue([_dot(x, wbf[a][...]) for a in range(n_acc)], ()).astype(out.dtype)

    @pl.when(act_ref[w] == 0)
    def _():
        out[...] = jnp.zeros(out.shape, out.dtype)


def _gmm(xs, weights, tables, *, tm, tn, out_dtype, epilogue, vmem_mb, name):
    p, kdim = xs.shape
    n_out = weights[0].shape[2]
    n_items = tables[0].shape[0]
    n_acc = len(weights)
    grid_spec = pltpu.PrefetchScalarGridSpec(
        num_scalar_prefetch=6,
        grid=(n_items,),
        in_specs=[pl.BlockSpec((tm, kdim), lambda w, rt, e, jo, jw, f, a: (rt[w], 0))]
        + [pl.BlockSpec((None, kdim, tn), lambda w, rt, e, jo, jw, f, a: (e[w], 0, jw[w])) for _ in weights],
        out_specs=pl.BlockSpec((tm, tn), lambda w, rt, e, jo, jw, f, a: (rt[w], jo[w])),
        scratch_shapes=[pltpu.VMEM((kdim, tn), BF16) for _ in weights],
    )
    return pl.pallas_call(
        functools.partial(_gmm_body, n_acc=n_acc, epilogue=epilogue),
        grid_spec=grid_spec,
        out_shape=jax.ShapeDtypeStruct((p, n_out), out_dtype),
        compiler_params=_cp(("arbitrary",), vmem_mb),
        name=name,
    )(*tables, xs, *weights)


def _work_tables(counts, n_col_tiles, tm, n_tiles):
    tiles_e = (counts + tm - 1) // tm
    tile_end = jnp.cumsum(tiles_e)
    tile_start = tile_end - tiles_e
    total_tiles = tile_end[-1]
    n_items = n_col_tiles * n_tiles
    item_end = n_col_tiles * tile_end
    w = jnp.arange(n_items, dtype=jnp.int32)
    n_active = n_col_tiles * total_tiles
    active = w < n_active
    wc = jnp.minimum(w, n_active - 1)
    e = jnp.sum((wc[:, None] >= item_end[None, :]).astype(jnp.int32), axis=1)
    e = jnp.minimum(e, N_EXPERTS - 1)
    te = jnp.maximum(tiles_e[e], 1)
    local = wc - n_col_tiles * tile_start[e]
    j = local // te
    q = local % te
    rt = tile_start[e] + q
    first = ((q == 0) & active).astype(jnp.int32)
    spare = jnp.maximum(w - n_active, 0)
    rt = jnp.where(active, rt, total_tiles + spare // n_col_tiles)
    j_out = jnp.where(active, j, spare % n_col_tiles)
    i32 = lambda a: a.astype(jnp.int32)
    return (i32(rt), i32(e), i32(j_out), i32(j), first, i32(active))


def _combine_body(pos_ref, x_ref, info_ref, g_ref, y_hbm, o_ref, buf, sem, *, tm):
    i = pl.program_id(0)

    def issue(r, c):
        for k in range(TOP_K):
            _row_copy(y_hbm, pos_ref[(i * tm + r) * TOP_K + k], buf.at[k], r, sem).start()
        return c

    lax.fori_loop(0, tm, issue, 0)

    def wait(r, c):
        for k in range(TOP_K):
            _row_copy(y_hbm, 0, buf.at[k], r, sem).wait()
        return c

    lax.fori_loop(0, tm, wait, 0)
    info = info_ref[...]
    acc = x_ref[...] + info[:, 4:5] * buf[0] + info[:, 5:6] * buf[1]
    y = acc * lax.rsqrt(jnp.mean(acc * acc, axis=-1, keepdims=True) + NORM_EPS)
    o_ref[...] = y * g_ref[...]


def _combine(x, info, pos_flat, y_sorted, final_norm, *, tm=128):
    t, d = x.shape
    grid_spec = pltpu.PrefetchScalarGridSpec(
        num_scalar_prefetch=1,
        grid=(t // tm,),
        in_specs=[pl.BlockSpec((tm, d), lambda i, pos: (i, 0)),
                  pl.BlockSpec((tm, LANES), lambda i, pos: (i, 0)),
                  pl.BlockSpec((1, d), lambda i, pos: (0, 0)),
                  pl.BlockSpec(memory_space=pl.ANY)],
        out_specs=pl.BlockSpec((tm, d), lambda i, pos: (i, 0)),
        scratch_shapes=[pltpu.VMEM((TOP_K, tm, d), F32), pltpu.SemaphoreType.DMA(())],
    )
    return pl.pallas_call(
        functools.partial(_combine_body, tm=tm),
        grid_spec=grid_spec,
        out_shape=jax.ShapeDtypeStruct((t, d), F32),
        compiler_params=_cp(("arbitrary",), 40),
        name="moe_combine",
    )(pos_flat, x, info, final_norm.reshape(1, d).astype(F32), y_sorted)


def _even_mixers(x, tables, w_in, attn_norm, q_norm, w_uq, kv_norm, w_ukv, gate_bias, head_norm, batch, seq):
    t = x.shape[0]
    c_tab, s1_tab, s2_tab = tables
    n_ml_main = 2 * ML_HEADS * ML_QK + 2 * ML_HEADS * ML_V
    gate_lo = MLA_IN + n_ml_main
    w_a = jnp.concatenate(
        [w_in[:, :MLA_IN], jnp.zeros((D_MODEL, LANES - MLA_ROPE), w_in.dtype),
         w_in[:, gate_lo:], jnp.zeros((D_MODEL, LANES - 2 * ML_HEADS), w_in.dtype)], axis=1).astype(BF16)
    w_b = w_in[:, MLA_IN:gate_lo].astype(BF16)
    n_a = w_a.shape[1]
    w_q = jnp.pad(w_uq.reshape(MLA_Q_LORA, MLA_HEADS, MLA_NOPE + MLA_ROPE),
                  ((0, 0), (0, 0), (0, MLA_QK_PAD - MLA_NOPE - MLA_ROPE))).reshape(MLA_Q_LORA, -1).astype(BF16)
    w_kv3 = w_ukv.reshape(MLA_KV_LORA, MLA_HEADS, MLA_NOPE + MLA_V)
    w_k = jnp.pad(w_kv3[:, :, :MLA_NOPE], ((0, 0), (0, 0), (0, MLA_QK_PAD - MLA_NOPE))).reshape(MLA_KV_LORA, -1).astype(BF16)
    w_v = w_kv3[:, :, MLA_NOPE:].reshape(MLA_KV_LORA, -1).astype(BF16)

    xn = _rmsnorm(x, attn_norm, width=D_MODEL)
    u_a = _matmul([xn], [[(w_a, 0)]], n_out=n_a, tm=1024, tn=n_a // 2, out_dtype=F32, epilogue=_epi_plain,
                  name="even_in_a")
    u_b = _matmul([xn], [[(w_b, 0)]], n_out=n_ml_main, tm=1024, tn=512, out_dtype=BF16, epilogue=_epi_plain,
                  name="even_in_b")

    cqn = _rmsnorm(u_a, q_norm, width=MLA_Q_LORA, col_block=0)
    ckvn = _rmsnorm(u_a, kv_norm, width=MLA_KV_LORA, col_block=MLA_Q_LORA // MLA_KV_LORA)
    tm_p = min(1024, t)
    tab_specs = [(tab, (tm_p, LANES), lambda i, j: (i, 0)) for tab in (c_tab, s1_tab, s2_tab)]
    n_qk = MLA_HEADS * MLA_QK_PAD
    q_full = _matmul([cqn], [[(w_q, 0)]], n_out=n_qk, tm=tm_p, tn=1024, out_dtype=BF16, epilogue=_epi_mla_q,
                     extras=tab_specs, name="mla_q")
    kr_spec = (u_a, (tm_p, LANES), lambda i, j: (i, (MLA_Q_LORA + MLA_KV_LORA) // LANES))
    k_full = _matmul([ckvn], [[(w_k, 0)]], n_out=n_qk, tm=tm_p, tn=1024, out_dtype=BF16, epilogue=_epi_mla_k,
                     extras=[kr_spec] + tab_specs, name="mla_k")
    v = _matmul([ckvn], [[(w_v, 0)]], n_out=MLA_HEADS * MLA_V, tm=tm_p, tn=1024, out_dtype=BF16,
                epilogue=_epi_plain, name="mla_v")
    y_a = _mla_attention(q_full, k_full, v, batch, seq)

    bias_pad = jnp.concatenate([gate_bias.astype(F32), jnp.zeros((LANES - 2 * ML_HEADS,), F32)]).reshape(1, LANES)
    y_b = _mlstm(u_b, u_a, (MLA_IN + LANES - MLA_ROPE) // LANES, bias_pad, head_norm, batch, seq)
    return y_a, y_b


def _even_layer(x, tables, w_in, attn_norm, q_norm, w_uq, kv_norm, w_ukv, gate_bias, head_norm, w_out,
                ffn_norm, w_gate, w_up, w_down, batch, seq):
    y_a, y_b = _even_mixers(x, tables, w_in, attn_norm, q_norm, w_uq, kv_norm, w_ukv, gate_bias, head_norm,
                            batch, seq)
    res_spec = lambda arr, tm, tn: [(arr, (tm, tn), lambda i, j: (i, j))]
    x1 = _matmul([y_a, y_b], [[(w_out, 0), (w_out, 1)]], n_out=D_MODEL, tm=1024, tn=512, out_dtype=F32,
                 epilogue=_epi_residual, extras=res_spec(x, 1024, 512), name="even_out")
    hn = _rmsnorm(x1, ffn_norm, width=D_MODEL)
    a = _matmul([hn], [[(w_gate, 0)], [(w_up, 0)]], n_out=D_FF, tm=1024, tn=256, out_dtype=BF16,
                epilogue=_epi_swiglu, name="ffn_gate_up")
    x2 = _matmul([a], [[(w_down, 0)]], n_out=D_MODEL, tm=1024, tn=1024, tk=1024, out_dtype=F32,
                 epilogue=_epi_residual, extras=res_spec(x1, 1024, 1024), weight_stationary=False,
                 name="ffn_down")
    return x2


def _odd_layer(x, w_in, attn_norm, conv_w, w_out, ffn_norm, w_router, b_router, w_gate_e, w_up_e, w_down_e,
               final_norm, batch, seq):
    t = x.shape[0]
    xn = _rmsnorm(x, attn_norm, width=D_MODEL)
    n_in = w_in.shape[1]
    u = _matmul([xn], [[(w_in, 0)]], n_out=n_in, tm=1024, tn=512, out_dtype=BF16, epilogue=_epi_plain,
                name="odd_in")
    y_c = _short_conv(u, conv_w, batch, seq)
    y_d = _sb_attention(u, 3 * SC_WIDTH // LANES, batch, seq)
    x1 = _matmul([y_c, y_d], [[(w_out, 0), (w_out, 1)]], n_out=D_MODEL, tm=1024, tn=512, out_dtype=F32,
                 epilogue=_epi_residual, extras=[(x, (1024, 512), lambda i, j: (i, j))], name="odd_out")
    return _moe(x1, ffn_norm, w_router, b_router, w_gate_e, w_up_e, w_down_e, final_norm)


def _moe(x1, ffn_norm, w_router, b_router, w_gate_e, w_up_e, w_down_e, final_norm):
    t = x1.shape[0]
    h, info, cnt = _route(x1, ffn_norm, w_router, b_router)
    tm = MOE_TM
    n_tiles = (t * TOP_K) // tm + N_EXPERTS
    counts = cnt[0, :N_EXPERTS].astype(jnp.int32)
    tiles_e = (counts + tm - 1) // tm
    offs = (jnp.cumsum(tiles_e) - tiles_e) * tm
    idx = info[:, 0:TOP_K].astype(jnp.int32)
    rank = info[:, TOP_K:2 * TOP_K].astype(jnp.int32)
    pos_flat = (offs[idx] + rank).reshape(-1)
    n_active = jnp.sum(tiles_e).astype(jnp.int32).reshape(1)

    inv = _invperm(pos_flat, n_tiles * tm)
    xs = _dispatch(h, inv, n_active, n_tiles, tm=tm)
    tab_gu = _work_tables(counts, D_FF_EXPERT // 256, tm, n_tiles)
    a_s = _gmm(xs, [w_gate_e, w_up_e], tab_gu, tm=tm, tn=256, out_dtype=BF16, epilogue=_epi_swiglu,
               vmem_mb=48, name="moe_gate_up")
    tab_d = _work_tables(counts, D_MODEL // 512, tm, n_tiles)
    y_s = _gmm(a_s, [w_down_e], tab_d, tm=tm, tn=512, out_dtype=F32, epilogue=_epi_plain,
               vmem_mb=48, name="moe_down")
    return _combine(x1, info, pos_flat, y_s, final_norm)


def kernel(x, positions, even_attn_norm, even_w_in, even_q_norm, even_w_uq, even_kv_norm, even_w_ukv, even_ml_gate_bias, even_ml_head_norm, even_w_out, even_ffn_norm, even_w_gate, even_w_up, even_w_down, odd_attn_norm, odd_w_in, odd_conv_w, odd_w_out, odd_ffn_norm, odd_w_router, odd_b_router, odd_w_gate_e, odd_w_up_e, odd_w_down_e, final_norm):
    batch, seq, d = x.shape
    assert even_w_in.shape[0] == 1 and odd_w_in.shape[0] == 1, "kernel is written for one even and one odd layer"
    xf = x.reshape(batch * seq, d)
    tables = _rope_tables(positions)
    xf = _even_layer(xf, tables, even_w_in[0], even_attn_norm[0], even_q_norm[0], even_w_uq[0], even_kv_norm[0],
                     even_w_ukv[0], even_ml_gate_bias[0], even_ml_head_norm[0], even_w_out[0], even_ffn_norm[0],
                     even_w_gate[0], even_w_up[0], even_w_down[0], batch, seq)
    out = _odd_layer(xf, odd_w_in[0], odd_attn_norm[0], odd_conv_w[0], odd_w_out[0], odd_ffn_norm[0],
                     odd_w_router[0], odd_b_router[0], odd_w_gate_e[0], odd_w_up_e[0], odd_w_down_e[0],
                     final_norm, batch, seq)
    return out.reshape(batch, seq, d)
```

```python
import functools

import jax
import jax.numpy as jnp
from jax import lax
from jax.experimental import pallas as pl
from jax.experimental.pallas import tpu as pltpu

F32 = jnp.float32
BF16 = jnp.bfloat16

D_MODEL = 4096
NORM_EPS = 1e-6

MLA_HEADS = 16
MLA_Q_LORA = 1024
MLA_KV_LORA = 512
MLA_NOPE = 128
MLA_ROPE = 64
MLA_V = 128
ROPE_THETA = 10000.0
MLA_IN = MLA_Q_LORA + MLA_KV_LORA + MLA_ROPE
MLA_QK_PAD = 256

ML_HEADS = 4
ML_QK = 256
ML_V = 512
ML_CHUNK = 128
ML_VA = ML_V + 128

SC_WIDTH = 2048
SC_KERNEL = 3

SB_HEADS = 16
SB_HEAD_DIM = 128

D_FF = 11008
N_EXPERTS = 8
TOP_K = 2
D_FF_EXPERT = 4096

LANES = 128
MOE_TM = 512
SCALAR_UNROLL = 8


def _cp(sem, vmem_mb):
    return pltpu.CompilerParams(dimension_semantics=sem, vmem_limit_bytes=vmem_mb * 1024 * 1024)


def _dot(a, b):
    return jnp.dot(a, b, preferred_element_type=F32)


def _dot_nt(a, b):
    return lax.dot_general(a, b, (((1,), (1,)), ((), ())), preferred_element_type=F32)


def _log_sigmoid(x):
    return jnp.minimum(x, 0.0) - jnp.log1p(jnp.exp(-jnp.abs(x)))


def _split_dot(tri, x, pieces):
    acc = None
    r = x
    for p in range(pieces):
        hi = r.astype(BF16)
        part = _dot(tri, hi)
        acc = part if acc is None else acc + part
        if p + 1 < pieces:
            r = r - hi.astype(F32)
    return acc


def _rmsnorm_body(x_ref, g_ref, o_ref):
    x = x_ref[...].astype(F32)
    y = x * lax.rsqrt(jnp.mean(x * x, axis=-1, keepdims=True) + NORM_EPS)
    o_ref[...] = (y * g_ref[...]).astype(o_ref.dtype)


def _rmsnorm(x, g, *, width, col_block=0, out_dtype=BF16, tm=256):
    m = x.shape[0]
    return pl.pallas_call(
        _rmsnorm_body,
        grid=(m // tm,),
        in_specs=[pl.BlockSpec((tm, width), lambda i: (i, col_block)),
                  pl.BlockSpec((1, width), lambda i: (0, 0))],
        out_specs=pl.BlockSpec((tm, width), lambda i: (i, 0)),
        out_shape=jax.ShapeDtypeStruct((m, width), out_dtype),
        compiler_params=_cp(("parallel",), 40),
        name="rmsnorm",
    )(x, g.reshape(1, width).astype(F32))


def _mm_body(*refs, n_lhs, n_acc, n_extra, nk, k_valid_last, epilogue):
    lhs = refs[:n_lhs]
    ws = refs[n_lhs:n_lhs + n_acc * n_lhs]
    extras = refs[n_lhs + n_acc * n_lhs:n_lhs + n_acc * n_lhs + n_extra]
    out = refs[n_lhs + n_acc * n_lhs + n_extra]
    accs = refs[n_lhs + n_acc * n_lhs + n_extra + 1:]
    k = pl.program_id(2)

    def partials(mask_tail):
        parts = []
        for a in range(n_acc):
            s = None
            for l in range(n_lhs):
                x = lhs[l][...]
                w = ws[a * n_lhs + l][...]
                if mask_tail:
                    xc = lax.broadcasted_iota(jnp.int32, x.shape, 1)
                    x = jnp.where(xc < k_valid_last, x, jnp.zeros_like(x))
                    wr = lax.broadcasted_iota(jnp.int32, w.shape, 0)
                    w = jnp.where(wr < k_valid_last, w, jnp.zeros_like(w))
                p = _dot(x.astype(BF16), w.astype(BF16))
                s = p if s is None else s + p
            parts.append(s)
        return parts

    if nk == 1:
        out[...] = epilogue(partials(False), extras).astype(out.dtype)
        return

    ragged = k_valid_last is not None

    @pl.when(k == 0)
    def _():
        for a, p in enumerate(partials(False)):
            accs[a][...] = p

    @pl.when((k > 0) & (k < nk - 1) if ragged else (k > 0))
    def _():
        for a, p in enumerate(partials(False)):
            accs[a][...] += p

    if ragged:
        @pl.when(k == nk - 1)
        def _():
            for a, p in enumerate(partials(True)):
                accs[a][...] += p

    @pl.when(k == nk - 1)
    def _():
        out[...] = epilogue([acc[...] for acc in accs], extras).astype(out.dtype)


def _matmul(lhs, weights, *, n_out, tm, tn, out_dtype, epilogue, extras=(), tk=None,
            weight_stationary=True, vmem_mb=48, name="matmul"):
    m = lhs[0].shape[0]
    kdim = lhs[0].shape[1]
    tm = min(tm, m)
    tk = kdim if tk is None else tk
    nk = pl.cdiv(kdim, tk)
    k_valid_last = None if kdim % tk == 0 else kdim - (nk - 1) * tk
    n_lhs, n_acc = len(lhs), len(weights)
    if weight_stationary:
        grid = (n_out // tn, m // tm, nk)
        ij = lambda g0, g1: (g1, g0)
    else:
        grid = (m // tm, n_out // tn, nk)
        ij = lambda g0, g1: (g0, g1)

    in_specs, args = [], []
    for x in lhs:
        in_specs.append(pl.BlockSpec((tm, tk), lambda g0, g1, k: (ij(g0, g1)[0], k)))
        args.append(x)
    for wl in weights:
        for (w, rb) in wl:
            in_specs.append(pl.BlockSpec((tk, tn), lambda g0, g1, k, rb=rb: (rb * nk + k, ij(g0, g1)[1])))
            args.append(w)
    for (arr, bshape, imap) in extras:
        in_specs.append(pl.BlockSpec(bshape, lambda g0, g1, k, imap=imap: imap(*ij(g0, g1))))
        args.append(arr)
    scratch = [pltpu.VMEM((tm, tn), F32) for _ in range(n_acc)] if nk > 1 else []
    body = functools.partial(_mm_body, n_lhs=n_lhs, n_acc=n_acc, n_extra=len(extras), nk=nk,
                             k_valid_last=k_valid_last, epilogue=epilogue)
    return pl.pallas_call(
        body,
        grid=grid,
        in_specs=in_specs,
        out_specs=pl.BlockSpec((tm, tn), lambda g0, g1, k: ij(g0, g1)),
        out_shape=jax.ShapeDtypeStruct((m, n_out), out_dtype),
        scratch_shapes=scratch,
        compiler_params=_cp(("parallel", "parallel", "arbitrary"), vmem_mb),
        name=name,
    )(*args)


def _epi_plain(parts, extras):
    return parts[0]


def _epi_residual(parts, extras):
    return parts[0] + extras[0][...]


def _epi_swiglu(parts, extras):
    g, u = parts
    return g * jax.nn.sigmoid(g) * u


def _rope_tables_body(pos_ref, invf_ref, c_ref, s1_ref, s2_ref):
    ang = pos_ref[...] * invf_ref[...]
    lane = lax.broadcasted_iota(jnp.int32, ang.shape, 1)
    cos = jnp.cos(ang)
    sin = jnp.sin(ang)
    half = MLA_ROPE // 2
    c_ref[...] = jnp.where(lane < MLA_ROPE, cos, 0.0)
    s1_ref[...] = jnp.where(lane < half, -sin, 0.0)
    s2_ref[...] = jnp.where((lane >= half) & (lane < MLA_ROPE), sin, 0.0)


def _rope_tables(positions):
    t = positions.size
    half = MLA_ROPE // 2
    inv_freq = 1.0 / (ROPE_THETA ** (jnp.arange(0, MLA_ROPE, 2, dtype=F32) / MLA_ROPE))
    invf = jnp.concatenate([inv_freq, inv_freq, jnp.zeros((LANES - 2 * half,), F32)]).reshape(1, LANES)
    pos = jnp.broadcast_to(positions.reshape(t, 1).astype(F32), (t, LANES))
    tm = min(512, t)
    spec = pl.BlockSpec((tm, LANES), lambda i: (i, 0))
    sds = jax.ShapeDtypeStruct((t, LANES), F32)
    return pl.pallas_call(
        _rope_tables_body,
        grid=(t // tm,),
        in_specs=[spec, pl.BlockSpec((1, LANES), lambda i: (0, 0))],
        out_specs=[spec, spec, spec],
        out_shape=[sds, sds, sds],
        compiler_params=_cp(("parallel",), 32),
        name="rope_tables",
    )(pos, invf)


def _rope_lanes(t, c, s1, s2):
    return t * c + pltpu.roll(t, LANES - MLA_ROPE // 2, axis=1) * s1 + pltpu.roll(t, MLA_ROPE // 2, axis=1) * s2


def _epi_mla_q(parts, extras):
    acc = parts[0]
    c, s1, s2 = extras[0][...], extras[1][...], extras[2][...]
    scale = (MLA_NOPE + MLA_ROPE) ** -0.5
    pieces = []
    for h in range(acc.shape[1] // MLA_QK_PAD):
        base = h * MLA_QK_PAD
        pieces.append(acc[:, base:base + MLA_NOPE] * scale)
        pieces.append(_rope_lanes(acc[:, base + MLA_NOPE:base + MLA_QK_PAD], c, s1, s2) * scale)
    return jnp.concatenate(pieces, axis=1)


def _epi_mla_k(parts, extras):
    acc = parts[0]
    kr = _rope_lanes(extras[0][...], extras[1][...], extras[2][...], extras[3][...])
    pieces = []
    for h in range(acc.shape[1] // MLA_QK_PAD):
        base = h * MLA_QK_PAD
        pieces.append(acc[:, base:base + MLA_NOPE])
        pieces.append(acc[:, base + MLA_NOPE:base + MLA_QK_PAD] + kr)
    return jnp.concatenate(pieces, axis=1)


def _mla_attn_body(q_ref, k_ref, v_ref, o_ref, *, tq, seq):
    row = lax.broadcasted_iota(jnp.int32, (tq, tq), 0)
    col = lax.broadcasted_iota(jnp.int32, (tq, tq), 1)
    causal = col <= row
    for qi in range(seq // tq):
        n = (qi + 1) * tq
        q = q_ref[qi * tq:(qi + 1) * tq, :]
        s = _dot_nt(q, k_ref[0:n, :])
        diag = jnp.where(causal, s[:, n - tq:], -jnp.inf)
        s = diag if qi == 0 else jnp.concatenate([s[:, :n - tq], diag], axis=1)
        m = jnp.max(s, axis=1, keepdims=True)
        p = jnp.exp(s - m)
        l = jnp.sum(p, axis=1, keepdims=True)
        o = _dot(p.astype(BF16), v_ref[0:n, :]) / l
        o_ref[qi * tq:(qi + 1) * tq, :] = o.astype(o_ref.dtype)


def _mla_attention(q, k, v, batch, seq, *, tq=256):
    t = q.shape[0]
    tq = min(tq, seq)
    body = functools.partial(_mla_attn_body, tq=tq, seq=seq)
    return pl.pallas_call(
        body,
        grid=(batch, MLA_HEADS),
        in_specs=[pl.BlockSpec((seq, MLA_QK_PAD), lambda b, h: (b, h)),
                  pl.BlockSpec((seq, MLA_QK_PAD), lambda b, h: (b, h)),
                  pl.BlockSpec((seq, MLA_V), lambda b, h: (b, h))],
        out_specs=pl.BlockSpec((seq, MLA_V), lambda b, h: (b, h)),
        out_shape=jax.ShapeDtypeStruct((t, MLA_HEADS * MLA_V), BF16),
        compiler_params=_cp(("parallel", "parallel"), 48),
        name="mla_attention",
    )(q, k, v)


def _mlstm_body(q_ref, k_ref, v_ref, o_ref, g_ref, bias_ref, hn_ref, y_ref, ct_ref, m_ref):
    L = ML_CHUNK
    c = pl.program_id(1)

    @pl.when(c == 0)
    def _():
        ct_ref[...] = jnp.zeros(ct_ref.shape, F32)
        m_ref[...] = jnp.zeros(m_ref.shape, F32)

    g = g_ref[...] + bias_ref[...]
    logf = _log_sigmoid(g)
    row = lax.broadcasted_iota(jnp.int32, (L, L), 0)
    col = lax.broadcasted_iota(jnp.int32, (L, L), 1)
    causal = col <= row
    tri = jnp.where(causal, 1.0, 0.0).astype(BF16)
    cum = _split_dot(tri, logf, 3)
    lane = lax.broadcasted_iota(jnp.int32, (L, LANES), 1)
    colq = jnp.where(lane < ML_HEADS, g, cum)
    rowq = colq.T
    ones_col = jnp.where(lane == 0, 1.0, 0.0).astype(BF16)
    scale = ML_QK ** -0.5

    for h in range(ML_HEADS):
        i_c = colq[:, h:h + 1]
        cf_c = colq[:, ML_HEADS + h:ML_HEADS + h + 1]
        i_r = rowq[h:h + 1, :]
        cf_r = rowq[ML_HEADS + h:ML_HEADS + h + 1, :]
        m_prev = m_ref[h][:, 0:1]
        d = jnp.where(causal, cf_c - cf_r + i_r, -jnp.inf)
        m_inter = cf_c + m_prev
        m_t = jnp.maximum(m_inter, jnp.max(d, axis=1, keepdims=True))
        inter = jnp.exp(m_inter - m_t)
        p = jnp.exp(d - m_t)
        qh = q_ref[:, h * ML_QK:(h + 1) * ML_QK]
        kf = k_ref[:, h * ML_QK:(h + 1) * ML_QK].astype(F32) * scale
        kh = kf.astype(BF16)
        s = _dot_nt(qh, kh) * p
        va = jnp.concatenate([v_ref[:, h * ML_V:(h + 1) * ML_V], ones_col], axis=1)
        ct = ct_ref[h]
        na = _dot(s.astype(BF16), va) + inter * _dot(qh, ct.astype(BF16))
        num = na[:, :ML_V]
        den = na[:, ML_V:ML_V + 1]
        hh = num / jnp.maximum(jnp.abs(den), jnp.exp(-m_t))

        f_tot = cf_c[L - 1:L, :]
        w_log = f_tot - cf_c + i_c
        m_new = jnp.maximum(f_tot + m_prev, jnp.max(w_log, axis=0, keepdims=True))
        decay = jnp.exp(f_tot + m_prev - m_new)
        w = jnp.exp(w_log - m_new)
        wv = (va.astype(F32) * w).astype(BF16)
        ct_ref[h] = decay * ct + _dot(kf.T.astype(BF16), wv)
        m_ref[h] = jnp.broadcast_to(m_new, (1, LANES))

        hn = hh * lax.rsqrt(jnp.mean(hh * hh, axis=1, keepdims=True) + NORM_EPS) * hn_ref[:, h * ML_V:(h + 1) * ML_V]
        og = o_ref[:, h * ML_V:(h + 1) * ML_V].astype(F32)
        y_ref[:, h * ML_V:(h + 1) * ML_V] = (jax.nn.sigmoid(og) * hn).astype(y_ref.dtype)


def _mlstm(u_b, gates_src, gate_col_block, bias_pad, head_norm, batch, seq):
    t = u_b.shape[0]
    nc = seq // ML_CHUNK
    nqk = ML_HEADS * ML_QK
    nv = ML_HEADS * ML_V
    rows = lambda b, c: b * nc + c
    return pl.pallas_call(
        _mlstm_body,
        grid=(batch, nc),
        in_specs=[pl.BlockSpec((ML_CHUNK, nqk), lambda b, c: (rows(b, c), 0)),
                  pl.BlockSpec((ML_CHUNK, nqk), lambda b, c: (rows(b, c), 1)),
                  pl.BlockSpec((ML_CHUNK, nv), lambda b, c: (rows(b, c), 1)),
                  pl.BlockSpec((ML_CHUNK, nv), lambda b, c: (rows(b, c), 2)),
                  pl.BlockSpec((ML_CHUNK, LANES), lambda b, c: (rows(b, c), gate_col_block)),
                  pl.BlockSpec((1, LANES), lambda b, c: (0, 0)),
                  pl.BlockSpec((1, nv), lambda b, c: (0, 0))],
        out_specs=pl.BlockSpec((ML_CHUNK, nv), lambda b, c: (rows(b, c), 0)),
        out_shape=jax.ShapeDtypeStruct((t, nv), BF16),
        scratch_shapes=[pltpu.VMEM((ML_HEADS, ML_QK, ML_VA), F32), pltpu.VMEM((ML_HEADS, 1, LANES), F32)],
        compiler_params=_cp(("parallel", "arbitrary"), 32),
        name="mlstm",
    )(u_b, u_b, u_b, u_b, gates_src, bias_pad, head_norm.reshape(1, nv).astype(F32))


def _conv_body(b_ref, c_ref, h_ref, cp_ref, hp_ref, w_ref, y_ref, z_sc, *, ts, halo):
    i = pl.program_id(1)
    z = c_ref[...].astype(F32) * h_ref[...].astype(F32)
    zp = cp_ref[...].astype(F32) * hp_ref[...].astype(F32)
    z_sc[0:halo, :] = jnp.where(i > 0, zp, 0.0)
    z_sc[halo:halo + ts, :] = z
    w = w_ref[...]
    y = w[2:3, :] * z + w[1:2, :] * z_sc[halo - 1:halo - 1 + ts, :] + w[0:1, :] * z_sc[halo - 2:halo - 2 + ts, :]
    y_ref[...] = (b_ref[...].astype(F32) * y).astype(y_ref.dtype)


def _short_conv(u, conv_w, batch, seq, *, ts=256, tw=1024):
    t = u.shape[0]
    halo = 16
    ns = seq // ts
    nw = SC_WIDTH // tw
    wpad = jnp.concatenate([conv_w.astype(F32), jnp.zeros((8 - SC_KERNEL, SC_WIDTH), F32)], axis=0)
    rows = lambda b, i: b * ns + i
    prev = lambda b, i: jnp.maximum((b * seq + i * ts) // halo - 1, 0)
    body = functools.partial(_conv_body, ts=ts, halo=halo)
    return pl.pallas_call(
        body,
        grid=(batch, ns, nw),
        in_specs=[pl.BlockSpec((ts, tw), lambda b, i, j: (rows(b, i), j)),
                  pl.BlockSpec((ts, tw), lambda b, i, j: (rows(b, i), nw + j)),
                  pl.BlockSpec((ts, tw), lambda b, i, j: (rows(b, i), 2 * nw + j)),
                  pl.BlockSpec((halo, tw), lambda b, i, j: (prev(b, i), nw + j)),
                  pl.BlockSpec((halo, tw), lambda b, i, j: (prev(b, i), 2 * nw + j)),
                  pl.BlockSpec((8, tw), lambda b, i, j: (0, j))],
        out_specs=pl.BlockSpec((ts, tw), lambda b, i, j: (rows(b, i), j)),
        out_shape=jax.ShapeDtypeStruct((t, SC_WIDTH), BF16),
        scratch_shapes=[pltpu.VMEM((halo + ts, tw), F32)],
        compiler_params=_cp(("parallel", "parallel", "parallel"), 32),
        name="short_conv",
    )(u, u, u, u, u, wpad)


def _sb_body(q_ref, k_ref, v_ref, o_ref, *, tq, seq):
    scale = SB_HEAD_DIM ** -0.5
    row = lax.broadcasted_iota(jnp.int32, (tq, tq), 0)
    col = lax.broadcasted_iota(jnp.int32, (tq, tq), 1)
    strict = col < row
    upper = jnp.where(row > col, 1.0, 0.0).astype(BF16)
    for qi in range(seq // tq):
        nb = qi + 1
        q = q_ref[qi * tq:(qi + 1) * tq, :]
        z = _dot_nt(q, k_ref[0:nb * tq, :]) * scale
        sp = jnp.maximum(z, 0.0) + jnp.log(1.0 + jnp.exp(-jnp.abs(z)))
        later = jnp.zeros((tq, 1), F32)
        blocks = [None] * nb
        for j in range(nb - 1, -1, -1):
            zj = z[:, j * tq:(j + 1) * tq]
            spj = sp[:, j * tq:(j + 1) * tq]
            lk = -spj
            if j == nb - 1:
                lk = jnp.where(strict, lk, 0.0)
            between = _split_dot_rhs(lk, upper, 2)
            e = jnp.exp(zj - spj + between + later)
            if j == nb - 1:
                e = jnp.where(strict, e, 0.0)
            blocks[j] = e.astype(BF16)
            later = later + (between[:, 0:1] + lk[:, 0:1])
        a = blocks[0] if nb == 1 else jnp.concatenate(blocks, axis=1)
        o_ref[qi * tq:(qi + 1) * tq, :] = _dot(a, v_ref[0:nb * tq, :]).astype(o_ref.dtype)


def _split_dot_rhs(x, tri, pieces):
    acc = None
    r = x
    for p in range(pieces):
        hi = r.astype(BF16)
        part = _dot(hi, tri)
        acc = part if acc is None else acc + part
        if p + 1 < pieces:
            r = r - hi.astype(F32)
    return acc


def _sb_attention(u, col_base, batch, seq, *, tq=256):
    t = u.shape[0]
    tq = min(tq, seq)
    body = functools.partial(_sb_body, tq=tq, seq=seq)
    spec = lambda off: pl.BlockSpec((seq, SB_HEAD_DIM), lambda b, h: (b, col_base + off + h))
    return pl.pallas_call(
        body,
        grid=(batch, SB_HEADS),
        in_specs=[spec(0), spec(SB_HEADS), spec(2 * SB_HEADS)],
        out_specs=pl.BlockSpec((seq, SB_HEAD_DIM), lambda b, h: (b, h)),
        out_shape=jax.ShapeDtypeStruct((t, SB_HEADS * SB_HEAD_DIM), BF16),
        compiler_params=_cp(("parallel", "parallel"), 48),
        name="sb_attention",
    )(u, u, u)


def _route_body(x_ref, g_ref, wr_ref, br_ref, h_ref, info_ref, cnt_ref, carry):
    i = pl.program_id(0)

    @pl.when(i == 0)
    def _():
        carry[...] = jnp.zeros(carry.shape, F32)

    x = x_ref[...]
    h = x * lax.rsqrt(jnp.mean(x * x, axis=-1, keepdims=True) + NORM_EPS) * g_ref[...]
    h_ref[...] = h
    wr = wr_ref[...]
    h_hi = h.astype(BF16)
    h_lo = (h - h_hi.astype(F32)).astype(BF16)
    w_hi = wr.astype(BF16)
    w_lo = (wr - w_hi.astype(F32)).astype(BF16)
    logits = (_dot(h_hi, w_hi) + _dot(h_hi, w_lo)) + (_dot(h_lo, w_hi) + _dot(h_lo, w_lo)) + br_ref[...]
    tm = logits.shape[0]
    lane = lax.broadcasted_iota(jnp.int32, logits.shape, 1).astype(F32)
    lg = jnp.where(lane < N_EXPERTS, logits, -jnp.inf)
    m1 = jnp.max(lg, axis=1, keepdims=True)
    i1 = jnp.min(jnp.where(lg == m1, lane, float(LANES)), axis=1, keepdims=True)
    lg2 = jnp.where(lane == i1, -jnp.inf, lg)
    m2 = jnp.max(lg2, axis=1, keepdims=True)
    i2 = jnp.min(jnp.where(lg2 == m2, lane, float(LANES)), axis=1, keepdims=True)
    e = jnp.exp(m2 - m1)
    w1 = 1.0 / (1.0 + e)
    w2 = e / (1.0 + e)
    oh1 = lane == i1
    oh2 = lane == i2
    mask = jnp.where(oh1 | oh2, 1.0, 0.0)
    row = lax.broadcasted_iota(jnp.int32, (tm, tm), 0)
    col = lax.broadcasted_iota(jnp.int32, (tm, tm), 1)
    before = jnp.where(col < row, 1.0, 0.0).astype(BF16)
    rank_mat = _dot(before, mask.astype(BF16)) + carry[...]
    r1 = jnp.sum(jnp.where(oh1, rank_mat, 0.0), axis=1, keepdims=True)
    r2 = jnp.sum(jnp.where(oh2, rank_mat, 0.0), axis=1, keepdims=True)
    carry[...] += jnp.sum(mask, axis=0, keepdims=True)
    info = jnp.where(lane == 0, i1,
           jnp.where(lane == 1, i2,
           jnp.where(lane == 2, r1,
           jnp.where(lane == 3, r2,
           jnp.where(lane == 4, w1,
           jnp.where(lane == 5, w2, 0.0))))))
    info_ref[...] = info
    cnt_ref[...] = jnp.broadcast_to(carry[...], cnt_ref.shape)


def _route(x, g, w_router, b_router, *, tm=256):
    t, d = x.shape
    wr = jnp.concatenate([w_router.astype(F32), jnp.zeros((d, LANES - N_EXPERTS), F32)], axis=1)
    br = jnp.concatenate([b_router.astype(F32), jnp.zeros((LANES - N_EXPERTS,), F32)]).reshape(1, LANES)
    return pl.pallas_call(
        _route_body,
        grid=(t // tm,),
        in_specs=[pl.BlockSpec((tm, d), lambda i: (i, 0)),
                  pl.BlockSpec((1, d), lambda i: (0, 0)),
                  pl.BlockSpec((d, LANES), lambda i: (0, 0)),
                  pl.BlockSpec((1, LANES), lambda i: (0, 0))],
        out_specs=[pl.BlockSpec((tm, d), lambda i: (i, 0)),
                   pl.BlockSpec((tm, LANES), lambda i: (i, 0)),
                   pl.BlockSpec((8, LANES), lambda i: (0, 0))],
        out_shape=[jax.ShapeDtypeStruct((t, d), F32),
                   jax.ShapeDtypeStruct((t, LANES), F32),
                   jax.ShapeDtypeStruct((8, LANES), F32)],
        scratch_shapes=[pltpu.VMEM((1, LANES), F32)],
        compiler_params=_cp(("arbitrary",), 40),
        name="moe_route",
    )(x, g.reshape(1, d).astype(F32), wr, br)


def _invperm_body(pos_ref, inv_ref, *, n_assign, n_slots):
    def clear(p, c):
        inv_ref[p] = 0
        return c

    lax.fori_loop(0, n_slots, clear, 0, unroll=SCALAR_UNROLL)

    def put(a, c):
        inv_ref[pos_ref[a]] = lax.shift_right_logical(a, TOP_K.bit_length() - 1)
        return c

    lax.fori_loop(0, n_assign, put, 0, unroll=SCALAR_UNROLL)


def _invperm(pos_flat, n_slots):
    n_assign = pos_flat.shape[0]
    return pl.pallas_call(
        functools.partial(_invperm_body, n_assign=n_assign, n_slots=n_slots),
        in_specs=[pl.BlockSpec(memory_space=pltpu.SMEM)],
        out_specs=pl.BlockSpec(memory_space=pltpu.SMEM),
        out_shape=jax.ShapeDtypeStruct((n_slots,), jnp.int32),
        name="moe_invperm",
    )(pos_flat)


def _row_copy(src_hbm, row, dst_vmem, slot, sem):
    return pltpu.make_async_copy(src_hbm.at[pl.ds(row, 1), :], dst_vmem.at[pl.ds(slot, 1), :], sem)


def _wait_rows(src_hbm, dst_vmem, sem, n):
    def wait(r, c):
        _row_copy(src_hbm, 0, dst_vmem, r, sem).wait()
        return c

    lax.fori_loop(0, n, wait, 0, unroll=SCALAR_UNROLL)


def _dispatch_body(inv_ref, nact_ref, h_hbm, o_ref, buf, sem, *, tm):
    i = pl.program_id(0)
    nact = nact_ref[0]

    def gather(tile, slot):
        def issue(r, c):
            _row_copy(h_hbm, inv_ref[tile * tm + r], buf.at[slot], r, sem.at[slot]).start()
            return c

        lax.fori_loop(0, tm, issue, 0, unroll=SCALAR_UNROLL)

    @pl.when((i == 0) & (nact > 0))
    def _():
        gather(0, 0)

    for slot in range(2):
        @pl.when((i + 1 < nact) & ((i + 1) % 2 == slot))
        def _():
            gather(i + 1, slot)

    for slot in range(2):
        @pl.when((i < nact) & (i % 2 == slot))
        def _():
            _wait_rows(h_hbm, buf.at[slot], sem.at[slot], tm)
            o_ref[...] = buf[slot].astype(o_ref.dtype)

    @pl.when(i >= nact)
    def _():
        o_ref[...] = jnp.zeros(o_ref.shape, o_ref.dtype)


def _dispatch(h, inv, n_active_tiles, n_tiles, *, tm):
    d = h.shape[1]
    grid_spec = pltpu.PrefetchScalarGridSpec(
        num_scalar_prefetch=2,
        grid=(n_tiles,),
        in_specs=[pl.BlockSpec(memory_space=pl.ANY)],
        out_specs=pl.BlockSpec((tm, d), lambda i, inv, na: (i, 0)),
        scratch_shapes=[pltpu.VMEM((2, tm, d), F32), pltpu.SemaphoreType.DMA((2,))],
    )
    return pl.pallas_call(
        functools.partial(_dispatch_body, tm=tm),
        grid_spec=grid_spec,
        out_shape=jax.ShapeDtypeStruct((n_tiles * tm, d), BF16),
        compiler_params=_cp(("arbitrary",), 40),
        name="moe_dispatch",
    )(inv, n_active_tiles, h)


def _gmm_body(rt_ref, e_ref, jo_ref, jw_ref, first_ref, act_ref, x_ref, *refs, n_acc, epilogue):
    ws = refs[:n_acc]
    out = refs[n_acc]
    wbf = refs[n_acc + 1:]
    w = pl.program_id(0)

    @pl.when(first_ref[w] == 1)
    def _():
        for a in range(n_acc):
            wbf[a][...] = ws[a][...].astype(BF16)

    @pl.when(act_ref[w] == 1)
    def _():
        x = x_ref[...].astype(BF16)
        out[...] = epilogue([_dot(x, wbf[a][...]) for a in range(n_acc)], ()).astype(out.dtype)

    @pl.when(act_ref[w] == 0)
    def _():
        out[...] = jnp.zeros(out.shape, out.dtype)


def _gmm(xs, weights, tables, *, tm, tn, out_dtype, epilogue, vmem_mb, name):
    p, kdim = xs.shape
    n_out = weights[0].shape[2]
    n_items = tables[0].shape[0]
    n_acc = len(weights)
    grid_spec = pltpu.PrefetchScalarGridSpec(
        num_scalar_prefetch=6,
        grid=(n_items,),
        in_specs=[pl.BlockSpec((tm, kdim), lambda w, rt, e, jo, jw, f, a: (rt[w], 0))]
        + [pl.BlockSpec((None, kdim, tn), lambda w, rt, e, jo, jw, f, a: (e[w], 0, jw[w])) for _ in weights],
        out_specs=pl.BlockSpec((tm, tn), lambda w, rt, e, jo, jw, f, a: (rt[w], jo[w])),
        scratch_shapes=[pltpu.VMEM((kdim, tn), BF16) for _ in weights],
    )
    return pl.pallas_call(
        functools.partial(_gmm_body, n_acc=n_acc, epilogue=epilogue),
        grid_spec=grid_spec,
        out_shape=jax.ShapeDtypeStruct((p, n_out), out_dtype),
        compiler_params=_cp(("arbitrary",), vmem_mb),
        name=name,
    )(*tables, xs, *weights)


def _work_tables(counts, n_col_tiles, tm, n_tiles):
    tiles_e = (counts + tm - 1) // tm
    tile_end = jnp.cumsum(tiles_e)
    tile_start = tile_end - tiles_e
    total_tiles = tile_end[-1]
    n_items = n_col_tiles * n_tiles
    item_end = n_col_tiles * tile_end
    w = jnp.arange(n_items, dtype=jnp.int32)
    n_active = n_col_tiles * total_tiles
    active = w < n_active
    wc = jnp.minimum(w, n_active - 1)
    e = jnp.sum((wc[:, None] >= item_end[None, :]).astype(jnp.int32), axis=1)
    e = jnp.minimum(e, N_EXPERTS - 1)
    te = jnp.maximum(tiles_e[e], 1)
    local = wc - n_col_tiles * tile_start[e]
    j = local // te
    q = local % te
    rt = tile_start[e] + q
    first = ((q == 0) & active).astype(jnp.int32)
    spare = jnp.maximum(w - n_active, 0)
    rt = jnp.where(active, rt, total_tiles + spare // n_col_tiles)
    j_out = jnp.where(active, j, spare % n_col_tiles)
    i32 = lambda a: a.astype(jnp.int32)
    return (i32(rt), i32(e), i32(j_out), i32(j), first, i32(active))


def _combine_body(pos_ref, x_ref, info_ref, g_ref, y_hbm, o_ref, buf, sem, *, tm):
    i = pl.program_id(0)
    n = pl.num_programs(0)

    def gather(tile, slot):
        def issue(r, c):
            for k in range(TOP_K):
                _row_copy(y_hbm, pos_ref[(tile * tm + r) * TOP_K + k], buf.at[slot, k], r, sem.at[slot]).start()
            return c

        lax.fori_loop(0, tm, issue, 0, unroll=SCALAR_UNROLL)

    @pl.when(i == 0)
    def _():
        gather(0, 0)

    for slot in range(2):
        @pl.when((i + 1 < n) & ((i + 1) % 2 == slot))
        def _():
            gather(i + 1, slot)

    for slot in range(2):
        @pl.when(i % 2 == slot)
        def _():
            for k in range(TOP_K):
                _wait_rows(y_hbm, buf.at[slot, k], sem.at[slot], tm)
            info = info_ref[...]
            acc = x_ref[...] + info[:, 4:5] * buf[slot, 0] + info[:, 5:6] * buf[slot, 1]
            y = acc * lax.rsqrt(jnp.mean(acc * acc, axis=-1, keepdims=True) + NORM_EPS)
            o_ref[...] = y * g_ref[...]


def _combine(x, info, pos_flat, y_sorted, final_norm, *, tm=128):
    t, d = x.shape
    grid_spec = pltpu.PrefetchScalarGridSpec(
        num_scalar_prefetch=1,
        grid=(t // tm,),
        in_specs=[pl.BlockSpec((tm, d), lambda i, pos: (i, 0)),
                  pl.BlockSpec((tm, LANES), lambda i, pos: (i, 0)),
                  pl.BlockSpec((1, d), lambda i, pos: (0, 0)),
                  pl.BlockSpec(memory_space=pl.ANY)],
        out_specs=pl.BlockSpec((tm, d), lambda i, pos: (i, 0)),
        scratch_shapes=[pltpu.VMEM((2, TOP_K, tm, d), F32), pltpu.SemaphoreType.DMA((2,))],
    )
    return pl.pallas_call(
        functools.partial(_combine_body, tm=tm),
        grid_spec=grid_spec,
        out_shape=jax.ShapeDtypeStruct((t, d), F32),
        compiler_params=_cp(("arbitrary",), 40),
        name="moe_combine",
    )(pos_flat, x, info, final_norm.reshape(1, d).astype(F32), y_sorted)


def _even_mixers(x, tables, w_in, attn_norm, q_norm, w_uq, kv_norm, w_ukv, gate_bias, head_norm, batch, seq):
    t = x.shape[0]
    c_tab, s1_tab, s2_tab = tables
    n_ml_main = 2 * ML_HEADS * ML_QK + 2 * ML_HEADS * ML_V
    gate_lo = MLA_IN + n_ml_main
    w_a = jnp.concatenate(
        [w_in[:, :MLA_IN], jnp.zeros((D_MODEL, LANES - MLA_ROPE), w_in.dtype),
         w_in[:, gate_lo:], jnp.zeros((D_MODEL, LANES - 2 * ML_HEADS), w_in.dtype)], axis=1).astype(BF16)
    w_b = w_in[:, MLA_IN:gate_lo].astype(BF16)
    n_a = w_a.shape[1]
    w_q = jnp.pad(w_uq.reshape(MLA_Q_LORA, MLA_HEADS, MLA_NOPE + MLA_ROPE),
                  ((0, 0), (0, 0), (0, MLA_QK_PAD - MLA_NOPE - MLA_ROPE))).reshape(MLA_Q_LORA, -1).astype(BF16)
    w_kv3 = w_ukv.reshape(MLA_KV_LORA, MLA_HEADS, MLA_NOPE + MLA_V)
    w_k = jnp.pad(w_kv3[:, :, :MLA_NOPE], ((0, 0), (0, 0), (0, MLA_QK_PAD - MLA_NOPE))).reshape(MLA_KV_LORA, -1).astype(BF16)
    w_v = w_kv3[:, :, MLA_NOPE:].reshape(MLA_KV_LORA, -1).astype(BF16)

    xn = _rmsnorm(x, attn_norm, width=D_MODEL)
    u_a = _matmul([xn], [[(w_a, 0)]], n_out=n_a, tm=1024, tn=n_a // 2, out_dtype=F32, epilogue=_epi_plain,
                  name="even_in_a")
    u_b = _matmul([xn], [[(w_b, 0)]], n_out=n_ml_main, tm=1024, tn=512, out_dtype=BF16, epilogue=_epi_plain,
                  name="even_in_b")

    cqn = _rmsnorm(u_a, q_norm, width=MLA_Q_LORA, col_block=0)
    ckvn = _rmsnorm(u_a, kv_norm, width=MLA_KV_LORA, col_block=MLA_Q_LORA // MLA_KV_LORA)
    tm_p = min(1024, t)
    tab_specs = [(tab, (tm_p, LANES), lambda i, j: (i, 0)) for tab in (c_tab, s1_tab, s2_tab)]
    n_qk = MLA_HEADS * MLA_QK_PAD
    q_full = _matmul([cqn], [[(w_q, 0)]], n_out=n_qk, tm=tm_p, tn=1024, out_dtype=BF16, epilogue=_epi_mla_q,
                     extras=tab_specs, name="mla_q")
    kr_spec = (u_a, (tm_p, LANES), lambda i, j: (i, (MLA_Q_LORA + MLA_KV_LORA) // LANES))
    k_full = _matmul([ckvn], [[(w_k, 0)]], n_out=n_qk, tm=tm_p, tn=1024, out_dtype=BF16, epilogue=_epi_mla_k,
                     extras=[kr_spec] + tab_specs, name="mla_k")
    v = _matmul([ckvn], [[(w_v, 0)]], n_out=MLA_HEADS * MLA_V, tm=tm_p, tn=1024, out_dtype=BF16,
                epilogue=_epi_plain, name="mla_v")
    y_a = _mla_attention(q_full, k_full, v, batch, seq)

    bias_pad = jnp.concatenate([gate_bias.astype(F32), jnp.zeros((LANES - 2 * ML_HEADS,), F32)]).reshape(1, LANES)
    y_b = _mlstm(u_b, u_a, (MLA_IN + LANES - MLA_ROPE) // LANES, bias_pad, head_norm, batch, seq)
    return y_a, y_b


def _even_layer(x, tables, w_in, attn_norm, q_norm, w_uq, kv_norm, w_ukv, gate_bias, head_norm, w_out,
                ffn_norm, w_gate, w_up, w_down, batch, seq):
    y_a, y_b = _even_mixers(x, tables, w_in, attn_norm, q_norm, w_uq, kv_norm, w_ukv, gate_bias, head_norm,
                            batch, seq)
    res_spec = lambda arr, tm, tn: [(arr, (tm, tn), lambda i, j: (i, j))]
    x1 = _matmul([y_a, y_b], [[(w_out, 0), (w_out, 1)]], n_out=D_MODEL, tm=1024, tn=512, out_dtype=F32,
                 epilogue=_epi_residual, extras=res_spec(x, 1024, 512), name="even_out")
    hn = _rmsnorm(x1, ffn_norm, width=D_MODEL)
    a = _matmul([hn], [[(w_gate, 0)], [(w_up, 0)]], n_out=D_FF, tm=1024, tn=256, out_dtype=BF16,
                epilogue=_epi_swiglu, name="ffn_gate_up")
    x2 = _matmul([a], [[(w_down, 0)]], n_out=D_MODEL, tm=1024, tn=1024, tk=2048, out_dtype=F32,
                 epilogue=_epi_residual, extras=res_spec(x1, 1024, 1024), weight_stationary=False,
                 vmem_mb=56, name="ffn_down")
    return x2


def _odd_layer(x, w_in, attn_norm, conv_w, w_out, ffn_norm, w_router, b_router, w_gate_e, w_up_e, w_down_e,
               final_norm, batch, seq):
    t = x.shape[0]
    xn = _rmsnorm(x, attn_norm, width=D_MODEL)
    n_in = w_in.shape[1]
    u = _matmul([xn], [[(w_in, 0)]], n_out=n_in, tm=1024, tn=512, out_dtype=BF16, epilogue=_epi_plain,
                name="odd_in")
    y_c = _short_conv(u, conv_w, batch, seq)
    y_d = _sb_attention(u, 3 * SC_WIDTH // LANES, batch, seq)
    x1 = _matmul([y_c, y_d], [[(w_out, 0), (w_out, 1)]], n_out=D_MODEL, tm=1024, tn=512, out_dtype=F32,
                 epilogue=_epi_residual, extras=[(x, (1024, 512), lambda i, j: (i, j))], name="odd_out")
    return _moe(x1, ffn_norm, w_router, b_router, w_gate_e, w_up_e, w_down_e, final_norm)


def _moe(x1, ffn_norm, w_router, b_router, w_gate_e, w_up_e, w_down_e, final_norm):
    t = x1.shape[0]
    h, info, cnt = _route(x1, ffn_norm, w_router, b_router)
    tm = MOE_TM
    n_tiles = (t * TOP_K) // tm + N_EXPERTS
    counts = cnt[0, :N_EXPERTS].astype(jnp.int32)
    tiles_e = (counts + tm - 1) // tm
    offs = (jnp.cumsum(tiles_e) - tiles_e) * tm
    idx = info[:, 0:TOP_K].astype(jnp.int32)
    rank = info[:, TOP_K:2 * TOP_K].astype(jnp.int32)
    pos_flat = (offs[idx] + rank).reshape(-1)
    n_active = jnp.sum(tiles_e).astype(jnp.int32).reshape(1)

    inv = _invperm(pos_flat, n_tiles * tm)
    xs = _dispatch(h, inv, n_active, n_tiles, tm=tm)
    tab_gu = _work_tables(counts, D_FF_EXPERT // 256, tm, n_tiles)
    a_s = _gmm(xs, [w_gate_e, w_up_e], tab_gu, tm=tm, tn=256, out_dtype=BF16, epilogue=_epi_swiglu,
               vmem_mb=48, name="moe_gate_up")
    tab_d = _work_tables(counts, D_MODEL // 512, tm, n_tiles)
    y_s = _gmm(a_s, [w_down_e], tab_d, tm=tm, tn=512, out_dtype=F32, epilogue=_epi_plain,
               vmem_mb=48, name="moe_down")
    return _combine(x1, info, pos_flat, y_s, final_norm)


def kernel(x, positions, even_attn_norm, even_w_in, even_q_norm, even_w_uq, even_kv_norm, even_w_ukv, even_ml_gate_bias, even_ml_head_norm, even_w_out, even_ffn_norm, even_w_gate, even_w_up, even_w_down, odd_attn_norm, odd_w_in, odd_conv_w, odd_w_out, odd_ffn_norm, odd_w_router, odd_b_router, odd_w_gate_e, odd_w_up_e, odd_w_down_e, final_norm):
    batch, seq, d = x.shape
    assert even_w_in.shape[0] == 1 and odd_w_in.shape[0] == 1, "kernel is written for one even and one odd layer"
    xf = x.reshape(batch * seq, d)
    tables = _rope_tables(positions)
    xf = _even_layer(xf, tables, even_w_in[0], even_attn_norm[0], even_q_norm[0], even_w_uq[0], even_kv_norm[0],
                     even_w_ukv[0], even_ml_gate_bias[0], even_ml_head_norm[0], even_w_out[0], even_ffn_norm[0],
                     even_w_gate[0], even_w_up[0], even_w_down[0], batch, seq)
    out = _odd_layer(xf, odd_w_in[0], odd_attn_norm[0], odd_conv_w[0], odd_w_out[0], odd_ffn_norm[0],
                     odd_w_router[0], odd_b_router[0], odd_w_gate_e[0], odd_w_up_e[0], odd_w_down_e[0],
                     final_norm, batch, seq)
    return out.reshape(batch, seq, d)
```

```python
import functools

import jax
import jax.numpy as jnp
from jax import lax
from jax.experimental import pallas as pl
from jax.experimental.pallas import tpu as pltpu

F32 = jnp.float32
BF16 = jnp.bfloat16

D_MODEL = 4096
NORM_EPS = 1e-6

MLA_HEADS = 16
MLA_Q_LORA = 1024
MLA_KV_LORA = 512
MLA_NOPE = 128
MLA_ROPE = 64
MLA_V = 128
ROPE_THETA = 10000.0
MLA_IN = MLA_Q_LORA + MLA_KV_LORA + MLA_ROPE
MLA_QK_PAD = 256

ML_HEADS = 4
ML_QK = 256
ML_V = 512
ML_CHUNK = 128
ML_VA = ML_V + 128

SC_WIDTH = 2048
SC_KERNEL = 3

SB_HEADS = 16
SB_HEAD_DIM = 128

D_FF = 11008
N_EXPERTS = 8
TOP_K = 2
D_FF_EXPERT = 4096

LANES = 128
MOE_TM = 512
DMA_QUEUES = 2
LOG2_E = 1.4426950408889634
SCALAR_UNROLL = 8


def _cp(sem, vmem_mb):
    return pltpu.CompilerParams(dimension_semantics=sem, vmem_limit_bytes=vmem_mb * 1024 * 1024)


def _dot(a, b):
    return jnp.dot(a, b, preferred_element_type=F32)


def _dot_nt(a, b):
    return lax.dot_general(a, b, (((1,), (1,)), ((), ())), preferred_element_type=F32)


def _log_sigmoid(x):
    return jnp.minimum(x, 0.0) - jnp.log1p(jnp.exp(-jnp.abs(x)))


def _split_dot(tri, x, pieces):
    acc = None
    r = x
    for p in range(pieces):
        hi = r.astype(BF16)
        part = _dot(tri, hi)
        acc = part if acc is None else acc + part
        if p + 1 < pieces:
            r = r - hi.astype(F32)
    return acc


def _rmsnorm_body(x_ref, g_ref, o_ref):
    x = x_ref[...].astype(F32)
    y = x * lax.rsqrt(jnp.mean(x * x, axis=-1, keepdims=True) + NORM_EPS)
    o_ref[...] = (y * g_ref[...]).astype(o_ref.dtype)


def _rmsnorm(x, g, *, width, col_block=0, out_dtype=BF16, tm=256):
    m = x.shape[0]
    return pl.pallas_call(
        _rmsnorm_body,
        grid=(m // tm,),
        in_specs=[pl.BlockSpec((tm, width), lambda i: (i, col_block)),
                  pl.BlockSpec((1, width), lambda i: (0, 0))],
        out_specs=pl.BlockSpec((tm, width), lambda i: (i, 0)),
        out_shape=jax.ShapeDtypeStruct((m, width), out_dtype),
        compiler_params=_cp(("parallel",), 40),
        name="rmsnorm",
    )(x, g.reshape(1, width).astype(F32))


def _mm_body(*refs, n_lhs, n_acc, n_extra, nk, k_valid_last, epilogue):
    lhs = refs[:n_lhs]
    ws = refs[n_lhs:n_lhs + n_acc * n_lhs]
    extras = refs[n_lhs + n_acc * n_lhs:n_lhs + n_acc * n_lhs + n_extra]
    out = refs[n_lhs + n_acc * n_lhs + n_extra]
    accs = refs[n_lhs + n_acc * n_lhs + n_extra + 1:]
    k = pl.program_id(2)

    def partials(mask_tail):
        parts = []
        for a in range(n_acc):
            s = None
            for l in range(n_lhs):
                x = lhs[l][...]
                w = ws[a * n_lhs + l][...]
                if mask_tail:
                    xc = lax.broadcasted_iota(jnp.int32, x.shape, 1)
                    x = jnp.where(xc < k_valid_last, x, jnp.zeros_like(x))
                    wr = lax.broadcasted_iota(jnp.int32, w.shape, 0)
                    w = jnp.where(wr < k_valid_last, w, jnp.zeros_like(w))
                p = _dot(x.astype(BF16), w.astype(BF16))
                s = p if s is None else s + p
            parts.append(s)
        return parts

    if nk == 1:
        out[...] = epilogue(partials(False), extras).astype(out.dtype)
        return

    ragged = k_valid_last is not None

    @pl.when(k == 0)
    def _():
        for a, p in enumerate(partials(False)):
            accs[a][...] = p

    @pl.when((k > 0) & (k < nk - 1) if ragged else (k > 0))
    def _():
        for a, p in enumerate(partials(False)):
            accs[a][...] += p

    if ragged:
        @pl.when(k == nk - 1)
        def _():
            for a, p in enumerate(partials(True)):
                accs[a][...] += p

    @pl.when(k == nk - 1)
    def _():
        out[...] = epilogue([acc[...] for acc in accs], extras).astype(out.dtype)


def _matmul(lhs, weights, *, n_out, tm, tn, out_dtype, epilogue, extras=(), tk=None,
            weight_stationary=True, vmem_mb=48, name="matmul"):
    m = lhs[0].shape[0]
    kdim = lhs[0].shape[1]
    tm = min(tm, m)
    tk = kdim if tk is None else tk
    nk = pl.cdiv(kdim, tk)
    k_valid_last = None if kdim % tk == 0 else kdim - (nk - 1) * tk
    n_lhs, n_acc = len(lhs), len(weights)
    if weight_stationary:
        grid = (n_out // tn, m // tm, nk)
        ij = lambda g0, g1: (g1, g0)
    else:
        grid = (m // tm, n_out // tn, nk)
        ij = lambda g0, g1: (g0, g1)

    in_specs, args = [], []
    for x in lhs:
        in_specs.append(pl.BlockSpec((tm, tk), lambda g0, g1, k: (ij(g0, g1)[0], k)))
        args.append(x)
    for wl in weights:
        for (w, rb) in wl:
            in_specs.append(pl.BlockSpec((tk, tn), lambda g0, g1, k, rb=rb: (rb * nk + k, ij(g0, g1)[1])))
            args.append(w)
    for (arr, bshape, imap) in extras:
        in_specs.append(pl.BlockSpec(bshape, lambda g0, g1, k, imap=imap: imap(*ij(g0, g1))))
        args.append(arr)
    scratch = [pltpu.VMEM((tm, tn), F32) for _ in range(n_acc)] if nk > 1 else []
    body = functools.partial(_mm_body, n_lhs=n_lhs, n_acc=n_acc, n_extra=len(extras), nk=nk,
                             k_valid_last=k_valid_last, epilogue=epilogue)
    return pl.pallas_call(
        body,
        grid=grid,
        in_specs=in_specs,
        out_specs=pl.BlockSpec((tm, tn), lambda g0, g1, k: ij(g0, g1)),
        out_shape=jax.ShapeDtypeStruct((m, n_out), out_dtype),
        scratch_shapes=scratch,
        compiler_params=_cp(("parallel", "parallel", "arbitrary"), vmem_mb),
        name=name,
    )(*args)


def _epi_plain(parts, extras):
    return parts[0]


def _epi_residual(parts, extras):
    return parts[0] + extras[0][...]


def _epi_swiglu(parts, extras):
    g, u = parts
    return g * jax.nn.sigmoid(g) * u


def _rope_tables_body(pos_ref, invf_ref, c_ref, s1_ref, s2_ref):
    ang = pos_ref[...] * invf_ref[...]
    lane = lax.broadcasted_iota(jnp.int32, ang.shape, 1)
    cos = jnp.cos(ang)
    sin = jnp.sin(ang)
    half = MLA_ROPE // 2
    c_ref[...] = jnp.where(lane < MLA_ROPE, cos, 0.0)
    s1_ref[...] = jnp.where(lane < half, -sin, 0.0)
    s2_ref[...] = jnp.where((lane >= half) & (lane < MLA_ROPE), sin, 0.0)


def _rope_tables(positions):
    t = positions.size
    half = MLA_ROPE // 2
    inv_freq = 1.0 / (ROPE_THETA ** (jnp.arange(0, MLA_ROPE, 2, dtype=F32) / MLA_ROPE))
    invf = jnp.concatenate([inv_freq, inv_freq, jnp.zeros((LANES - 2 * half,), F32)]).reshape(1, LANES)
    pos = jnp.broadcast_to(positions.reshape(t, 1).astype(F32), (t, LANES))
    tm = min(512, t)
    spec = pl.BlockSpec((tm, LANES), lambda i: (i, 0))
    sds = jax.ShapeDtypeStruct((t, LANES), F32)
    return pl.pallas_call(
        _rope_tables_body,
        grid=(t // tm,),
        in_specs=[spec, pl.BlockSpec((1, LANES), lambda i: (0, 0))],
        out_specs=[spec, spec, spec],
        out_shape=[sds, sds, sds],
        compiler_params=_cp(("parallel",), 32),
        name="rope_tables",
    )(pos, invf)


def _rope_lanes(t, c, s1, s2):
    return t * c + pltpu.roll(t, LANES - MLA_ROPE // 2, axis=1) * s1 + pltpu.roll(t, MLA_ROPE // 2, axis=1) * s2


def _epi_mla_q(parts, extras):
    acc = parts[0]
    c, s1, s2 = extras[0][...], extras[1][...], extras[2][...]
    scale = (MLA_NOPE + MLA_ROPE) ** -0.5
    pieces = []
    for h in range(acc.shape[1] // MLA_QK_PAD):
        base = h * MLA_QK_PAD
        pieces.append(acc[:, base:base + MLA_NOPE] * scale)
        pieces.append(_rope_lanes(acc[:, base + MLA_NOPE:base + MLA_QK_PAD], c, s1, s2) * scale)
    return jnp.concatenate(pieces, axis=1)


def _epi_mla_k(parts, extras):
    acc = parts[0]
    kr = _rope_lanes(extras[0][...], extras[1][...], extras[2][...], extras[3][...])
    pieces = []
    for h in range(acc.shape[1] // MLA_QK_PAD):
        base = h * MLA_QK_PAD
        pieces.append(acc[:, base:base + MLA_NOPE])
        pieces.append(acc[:, base + MLA_NOPE:base + MLA_QK_PAD] + kr)
    return jnp.concatenate(pieces, axis=1)


def _mla_attn_body(q_ref, k_ref, v_ref, o_ref, *, tq, seq):
    row = lax.broadcasted_iota(jnp.int32, (tq, tq), 0)
    col = lax.broadcasted_iota(jnp.int32, (tq, tq), 1)
    causal = col <= row
    for qi in range(seq // tq):
        n = (qi + 1) * tq
        q = q_ref[qi * tq:(qi + 1) * tq, :]
        s = _dot_nt(q, k_ref[0:n, :])
        diag = jnp.where(causal, s[:, n - tq:], -jnp.inf)
        s = diag if qi == 0 else jnp.concatenate([s[:, :n - tq], diag], axis=1)
        m = jnp.max(s, axis=1, keepdims=True)
        p = jnp.exp(s - m)
        l = jnp.sum(p, axis=1, keepdims=True)
        o = _dot(p.astype(BF16), v_ref[0:n, :]) / l
        o_ref[qi * tq:(qi + 1) * tq, :] = o.astype(o_ref.dtype)


def _mla_attention(q, k, v, batch, seq, *, tq=256):
    t = q.shape[0]
    tq = min(tq, seq)
    body = functools.partial(_mla_attn_body, tq=tq, seq=seq)
    return pl.pallas_call(
        body,
        grid=(batch, MLA_HEADS),
        in_specs=[pl.BlockSpec((seq, MLA_QK_PAD), lambda b, h: (b, h)),
                  pl.BlockSpec((seq, MLA_QK_PAD), lambda b, h: (b, h)),
                  pl.BlockSpec((seq, MLA_V), lambda b, h: (b, h))],
        out_specs=pl.BlockSpec((seq, MLA_V), lambda b, h: (b, h)),
        out_shape=jax.ShapeDtypeStruct((t, MLA_HEADS * MLA_V), BF16),
        compiler_params=_cp(("parallel", "parallel"), 48),
        name="mla_attention",
    )(q, k, v)


def _mlstm_body(q_ref, k_ref, v_ref, o_ref, g_ref, bias_ref, hn_ref, y_ref, ct_ref, m_ref):
    L = ML_CHUNK
    c = pl.program_id(1)

    @pl.when(c == 0)
    def _():
        ct_ref[...] = jnp.zeros(ct_ref.shape, F32)
        m_ref[...] = jnp.zeros(m_ref.shape, F32)

    g = g_ref[...] + bias_ref[...]
    logf = _log_sigmoid(g)
    row = lax.broadcasted_iota(jnp.int32, (L, L), 0)
    col = lax.broadcasted_iota(jnp.int32, (L, L), 1)
    causal = col <= row
    tri = jnp.where(causal, 1.0, 0.0).astype(BF16)
    cum = _split_dot(tri, logf, 3)
    lane = lax.broadcasted_iota(jnp.int32, (L, LANES), 1)
    colq = jnp.where(lane < ML_HEADS, g, cum)
    rowq = colq.T
    ones_col = jnp.where(lane == 0, 1.0, 0.0).astype(BF16)
    scale = ML_QK ** -0.5

    for h in range(ML_HEADS):
        i_c = colq[:, h:h + 1]
        cf_c = colq[:, ML_HEADS + h:ML_HEADS + h + 1]
        i_r = rowq[h:h + 1, :]
        cf_r = rowq[ML_HEADS + h:ML_HEADS + h + 1, :]
        m_prev = m_ref[h][:, 0:1]
        d = jnp.where(causal, cf_c - cf_r + i_r, -jnp.inf)
        m_inter = cf_c + m_prev
        m_t = jnp.maximum(m_inter, jnp.max(d, axis=1, keepdims=True))
        inter = jnp.exp(m_inter - m_t)
        p = jnp.exp(d - m_t)
        qh = q_ref[:, h * ML_QK:(h + 1) * ML_QK]
        kf = k_ref[:, h * ML_QK:(h + 1) * ML_QK].astype(F32) * scale
        kh = kf.astype(BF16)
        s = _dot_nt(qh, kh) * p
        va = jnp.concatenate([v_ref[:, h * ML_V:(h + 1) * ML_V], ones_col], axis=1)
        ct = ct_ref[h]
        na = _dot(s.astype(BF16), va) + inter * _dot(qh, ct.astype(BF16))
        num = na[:, :ML_V]
        den = na[:, ML_V:ML_V + 1]
        hh = num / jnp.maximum(jnp.abs(den), jnp.exp(-m_t))

        f_tot = cf_c[L - 1:L, :]
        w_log = f_tot - cf_c + i_c
        m_new = jnp.maximum(f_tot + m_prev, jnp.max(w_log, axis=0, keepdims=True))
        decay = jnp.exp(f_tot + m_prev - m_new)
        w = jnp.exp(w_log - m_new)
        wv = (va.astype(F32) * w).astype(BF16)
        ct_ref[h] = decay * ct + _dot(kf.T.astype(BF16), wv)
        m_ref[h] = jnp.broadcast_to(m_new, (1, LANES))

        hn = hh * lax.rsqrt(jnp.mean(hh * hh, axis=1, keepdims=True) + NORM_EPS) * hn_ref[:, h * ML_V:(h + 1) * ML_V]
        og = o_ref[:, h * ML_V:(h + 1) * ML_V].astype(F32)
        y_ref[:, h * ML_V:(h + 1) * ML_V] = (jax.nn.sigmoid(og) * hn).astype(y_ref.dtype)


def _mlstm(u_b, gates_src, gate_col_block, bias_pad, head_norm, batch, seq):
    t = u_b.shape[0]
    nc = seq // ML_CHUNK
    nqk = ML_HEADS * ML_QK
    nv = ML_HEADS * ML_V
    rows = lambda b, c: b * nc + c
    return pl.pallas_call(
        _mlstm_body,
        grid=(batch, nc),
        in_specs=[pl.BlockSpec((ML_CHUNK, nqk), lambda b, c: (rows(b, c), 0)),
                  pl.BlockSpec((ML_CHUNK, nqk), lambda b, c: (rows(b, c), 1)),
                  pl.BlockSpec((ML_CHUNK, nv), lambda b, c: (rows(b, c), 1)),
                  pl.BlockSpec((ML_CHUNK, nv), lambda b, c: (rows(b, c), 2)),
                  pl.BlockSpec((ML_CHUNK, LANES), lambda b, c: (rows(b, c), gate_col_block)),
                  pl.BlockSpec((1, LANES), lambda b, c: (0, 0)),
                  pl.BlockSpec((1, nv), lambda b, c: (0, 0))],
        out_specs=pl.BlockSpec((ML_CHUNK, nv), lambda b, c: (rows(b, c), 0)),
        out_shape=jax.ShapeDtypeStruct((t, nv), BF16),
        scratch_shapes=[pltpu.VMEM((ML_HEADS, ML_QK, ML_VA), F32), pltpu.VMEM((ML_HEADS, 1, LANES), F32)],
        compiler_params=_cp(("parallel", "arbitrary"), 32),
        name="mlstm",
    )(u_b, u_b, u_b, u_b, gates_src, bias_pad, head_norm.reshape(1, nv).astype(F32))


def _conv_body(b_ref, c_ref, h_ref, cp_ref, hp_ref, w_ref, y_ref, z_sc, *, ts, halo):
    i = pl.program_id(1)
    z = c_ref[...].astype(F32) * h_ref[...].astype(F32)
    zp = cp_ref[...].astype(F32) * hp_ref[...].astype(F32)
    z_sc[0:halo, :] = jnp.where(i > 0, zp, 0.0)
    z_sc[halo:halo + ts, :] = z
    w = w_ref[...]
    y = w[2:3, :] * z + w[1:2, :] * z_sc[halo - 1:halo - 1 + ts, :] + w[0:1, :] * z_sc[halo - 2:halo - 2 + ts, :]
    y_ref[...] = (b_ref[...].astype(F32) * y).astype(y_ref.dtype)


def _short_conv(u, conv_w, batch, seq, *, ts=256, tw=1024):
    t = u.shape[0]
    halo = 16
    ns = seq // ts
    nw = SC_WIDTH // tw
    wpad = jnp.concatenate([conv_w.astype(F32), jnp.zeros((8 - SC_KERNEL, SC_WIDTH), F32)], axis=0)
    rows = lambda b, i: b * ns + i
    prev = lambda b, i: jnp.maximum((b * seq + i * ts) // halo - 1, 0)
    body = functools.partial(_conv_body, ts=ts, halo=halo)
    return pl.pallas_call(
        body,
        grid=(batch, ns, nw),
        in_specs=[pl.BlockSpec((ts, tw), lambda b, i, j: (rows(b, i), j)),
                  pl.BlockSpec((ts, tw), lambda b, i, j: (rows(b, i), nw + j)),
                  pl.BlockSpec((ts, tw), lambda b, i, j: (rows(b, i), 2 * nw + j)),
                  pl.BlockSpec((halo, tw), lambda b, i, j: (prev(b, i), nw + j)),
                  pl.BlockSpec((halo, tw), lambda b, i, j: (prev(b, i), 2 * nw + j)),
                  pl.BlockSpec((8, tw), lambda b, i, j: (0, j))],
        out_specs=pl.BlockSpec((ts, tw), lambda b, i, j: (rows(b, i), j)),
        out_shape=jax.ShapeDtypeStruct((t, SC_WIDTH), BF16),
        scratch_shapes=[pltpu.VMEM((halo + ts, tw), F32)],
        compiler_params=_cp(("parallel", "parallel", "parallel"), 32),
        name="short_conv",
    )(u, u, u, u, u, wpad)


def _sb_body(q_ref, k_ref, v_ref, o_ref, *, tq, seq):
    scale = SB_HEAD_DIM ** -0.5 * LOG2_E
    row = lax.broadcasted_iota(jnp.int32, (tq, tq), 0)
    col = lax.broadcasted_iota(jnp.int32, (tq, tq), 1)
    strict = col < row
    upper = jnp.where(row > col, 1.0, 0.0).astype(BF16)
    for qi in range(seq // tq):
        nb = qi + 1
        q = q_ref[qi * tq:(qi + 1) * tq, :]
        z = _dot_nt(q, k_ref[0:nb * tq, :]) * scale
        sp = jnp.maximum(z, 0.0) + jnp.log2(1.0 + jnp.exp2(-jnp.abs(z)))
        later = jnp.zeros((tq, 1), F32)
        blocks = [None] * nb
        for j in range(nb - 1, -1, -1):
            zj = z[:, j * tq:(j + 1) * tq]
            spj = sp[:, j * tq:(j + 1) * tq]
            lk = -spj
            if j == nb - 1:
                lk = jnp.where(strict, lk, 0.0)
            between = _split_dot_rhs(lk, upper, 2)
            e = jnp.exp2(zj - spj + between + later)
            if j == nb - 1:
                e = jnp.where(strict, e, 0.0)
            blocks[j] = e.astype(BF16)
            later = later + (between[:, 0:1] + lk[:, 0:1])
        a = blocks[0] if nb == 1 else jnp.concatenate(blocks, axis=1)
        o_ref[qi * tq:(qi + 1) * tq, :] = _dot(a, v_ref[0:nb * tq, :]).astype(o_ref.dtype)


def _split_dot_rhs(x, tri, pieces):
    acc = None
    r = x
    for p in range(pieces):
        hi = r.astype(BF16)
        part = _dot(hi, tri)
        acc = part if acc is None else acc + part
        if p + 1 < pieces:
            r = r - hi.astype(F32)
    return acc


def _sb_attention(u, col_base, batch, seq, *, tq=256):
    t = u.shape[0]
    tq = min(tq, seq)
    body = functools.partial(_sb_body, tq=tq, seq=seq)
    spec = lambda off: pl.BlockSpec((seq, SB_HEAD_DIM), lambda b, h: (b, col_base + off + h))
    return pl.pallas_call(
        body,
        grid=(batch, SB_HEADS),
        in_specs=[spec(0), spec(SB_HEADS), spec(2 * SB_HEADS)],
        out_specs=pl.BlockSpec((seq, SB_HEAD_DIM), lambda b, h: (b, h)),
        out_shape=jax.ShapeDtypeStruct((t, SB_HEADS * SB_HEAD_DIM), BF16),
        compiler_params=_cp(("parallel", "parallel"), 48),
        name="sb_attention",
    )(u, u, u)


def _route_body(x_ref, g_ref, wr_ref, br_ref, h_ref, info_ref, cnt_ref, carry):
    i = pl.program_id(0)

    @pl.when(i == 0)
    def _():
        carry[...] = jnp.zeros(carry.shape, F32)

    x = x_ref[...]
    h = x * lax.rsqrt(jnp.mean(x * x, axis=-1, keepdims=True) + NORM_EPS) * g_ref[...]
    h_ref[...] = h
    wr = wr_ref[...]
    h_hi = h.astype(BF16)
    h_lo = (h - h_hi.astype(F32)).astype(BF16)
    w_hi = wr.astype(BF16)
    w_lo = (wr - w_hi.astype(F32)).astype(BF16)
    logits = (_dot(h_hi, w_hi) + _dot(h_hi, w_lo)) + (_dot(h_lo, w_hi) + _dot(h_lo, w_lo)) + br_ref[...]
    tm = logits.shape[0]
    lane = lax.broadcasted_iota(jnp.int32, logits.shape, 1).astype(F32)
    lg = jnp.where(lane < N_EXPERTS, logits, -jnp.inf)
    m1 = jnp.max(lg, axis=1, keepdims=True)
    i1 = jnp.min(jnp.where(lg == m1, lane, float(LANES)), axis=1, keepdims=True)
    lg2 = jnp.where(lane == i1, -jnp.inf, lg)
    m2 = jnp.max(lg2, axis=1, keepdims=True)
    i2 = jnp.min(jnp.where(lg2 == m2, lane, float(LANES)), axis=1, keepdims=True)
    e = jnp.exp(m2 - m1)
    w1 = 1.0 / (1.0 + e)
    w2 = e / (1.0 + e)
    oh1 = lane == i1
    oh2 = lane == i2
    mask = jnp.where(oh1 | oh2, 1.0, 0.0)
    row = lax.broadcasted_iota(jnp.int32, (tm, tm), 0)
    col = lax.broadcasted_iota(jnp.int32, (tm, tm), 1)
    before = jnp.where(col < row, 1.0, 0.0).astype(BF16)
    rank_mat = _dot(before, mask.astype(BF16)) + carry[...]
    r1 = jnp.sum(jnp.where(oh1, rank_mat, 0.0), axis=1, keepdims=True)
    r2 = jnp.sum(jnp.where(oh2, rank_mat, 0.0), axis=1, keepdims=True)
    carry[...] += jnp.sum(mask, axis=0, keepdims=True)
    info = jnp.where(lane == 0, i1,
           jnp.where(lane == 1, i2,
           jnp.where(lane == 2, r1,
           jnp.where(lane == 3, r2,
           jnp.where(lane == 4, w1,
           jnp.where(lane == 5, w2, 0.0))))))
    info_ref[...] = info
    cnt_ref[...] = jnp.broadcast_to(carry[...], cnt_ref.shape)


def _route(x, g, w_router, b_router, *, tm=256):
    t, d = x.shape
    wr = jnp.concatenate([w_router.astype(F32), jnp.zeros((d, LANES - N_EXPERTS), F32)], axis=1)
    br = jnp.concatenate([b_router.astype(F32), jnp.zeros((LANES - N_EXPERTS,), F32)]).reshape(1, LANES)
    return pl.pallas_call(
        _route_body,
        grid=(t // tm,),
        in_specs=[pl.BlockSpec((tm, d), lambda i: (i, 0)),
                  pl.BlockSpec((1, d), lambda i: (0, 0)),
                  pl.BlockSpec((d, LANES), lambda i: (0, 0)),
                  pl.BlockSpec((1, LANES), lambda i: (0, 0))],
        out_specs=[pl.BlockSpec((tm, d), lambda i: (i, 0)),
                   pl.BlockSpec((tm, LANES), lambda i: (i, 0)),
                   pl.BlockSpec((8, LANES), lambda i: (0, 0))],
        out_shape=[jax.ShapeDtypeStruct((t, d), F32),
                   jax.ShapeDtypeStruct((t, LANES), F32),
                   jax.ShapeDtypeStruct((8, LANES), F32)],
        scratch_shapes=[pltpu.VMEM((1, LANES), F32)],
        compiler_params=_cp(("arbitrary",), 40),
        name="moe_route",
    )(x, g.reshape(1, d).astype(F32), wr, br)


def _invperm_body(pos_ref, inv_ref, *, n_assign, n_slots):
    def clear(p, c):
        inv_ref[p] = 0
        return c

    lax.fori_loop(0, n_slots, clear, 0, unroll=SCALAR_UNROLL)

    def put(a, c):
        inv_ref[pos_ref[a]] = lax.shift_right_logical(a, TOP_K.bit_length() - 1)
        return c

    lax.fori_loop(0, n_assign, put, 0, unroll=SCALAR_UNROLL)


def _invperm(pos_flat, n_slots):
    n_assign = pos_flat.shape[0]
    return pl.pallas_call(
        functools.partial(_invperm_body, n_assign=n_assign, n_slots=n_slots),
        in_specs=[pl.BlockSpec(memory_space=pltpu.SMEM)],
        out_specs=pl.BlockSpec(memory_space=pltpu.SMEM),
        out_shape=jax.ShapeDtypeStruct((n_slots,), jnp.int32),
        name="moe_invperm",
    )(pos_flat)


def _row_copy(src_hbm, row, dst_vmem, slot, sem):
    return pltpu.make_async_copy(src_hbm.at[pl.ds(row, 1), :], dst_vmem.at[pl.ds(slot, 1), :], sem)


def _wait_rows(src_hbm, dst_vmem, sem, n):
    def wait(r, c):
        _row_copy(src_hbm, 0, dst_vmem, r, sem).wait()
        return c

    lax.fori_loop(0, n, wait, 0, unroll=SCALAR_UNROLL)


def _dispatch_body(inv_ref, nact_ref, h_hbm, o_ref, buf, sem, *, tm):
    i = pl.program_id(0)
    nact = nact_ref[0]

    def gather(tile, slot):
        def issue(rr, c):
            for u in range(DMA_QUEUES):
                r = rr * DMA_QUEUES + u
                _row_copy(h_hbm, inv_ref[tile * tm + r], buf.at[slot], r, sem.at[slot]).start(priority=u)
            return c

        lax.fori_loop(0, tm // DMA_QUEUES, issue, 0, unroll=SCALAR_UNROLL // DMA_QUEUES)

    @pl.when((i == 0) & (nact > 0))
    def _():
        gather(0, 0)

    for slot in range(2):
        @pl.when((i + 1 < nact) & ((i + 1) % 2 == slot))
        def _():
            gather(i + 1, slot)

    for slot in range(2):
        @pl.when((i < nact) & (i % 2 == slot))
        def _():
            _wait_rows(h_hbm, buf.at[slot], sem.at[slot], tm)
            o_ref[...] = buf[slot].astype(o_ref.dtype)

    @pl.when(i >= nact)
    def _():
        o_ref[...] = jnp.zeros(o_ref.shape, o_ref.dtype)


def _dispatch(h, inv, n_active_tiles, n_tiles, *, tm):
    d = h.shape[1]
    grid_spec = pltpu.PrefetchScalarGridSpec(
        num_scalar_prefetch=2,
        grid=(n_tiles,),
        in_specs=[pl.BlockSpec(memory_space=pl.ANY)],
        out_specs=pl.BlockSpec((tm, d), lambda i, inv, na: (i, 0)),
        scratch_shapes=[pltpu.VMEM((2, tm, d), F32), pltpu.SemaphoreType.DMA((2,))],
    )
    return pl.pallas_call(
        functools.partial(_dispatch_body, tm=tm),
        grid_spec=grid_spec,
        out_shape=jax.ShapeDtypeStruct((n_tiles * tm, d), BF16),
        compiler_params=_cp(("arbitrary",), 40),
        name="moe_dispatch",
    )(inv, n_active_tiles, h)


def _gmm_body(rt_ref, e_ref, jo_ref, jw_ref, first_ref, act_ref, x_ref, *refs, n_acc, epilogue):
    ws = refs[:n_acc]
    out = refs[n_acc]
    wbf = refs[n_acc + 1:]
    w = pl.program_id(0)

    @pl.when(first_ref[w] == 1)
    def _():
        for a in range(n_acc):
            wbf[a][...] = ws[a][...].astype(BF16)

    @pl.when(act_ref[w] == 1)
    def _():
        x = x_ref[...].astype(BF16)
        out[...] = epilogue([_dot(x, wbf[a][...]) for a in range(n_acc)], ()).astype(out.dtype)

    @pl.when(act_ref[w] == 0)
    def _():
        out[...] = jnp.zeros(out.shape, out.dtype)


def _gmm(xs, weights, tables, *, tm, tn, out_dtype, epilogue, vmem_mb, name):
    p, kdim = xs.shape
    n_out = weights[0].shape[2]
    n_items = tables[0].shape[0]
    n_acc = len(weights)
    grid_spec = pltpu.PrefetchScalarGridSpec(
        num_scalar_prefetch=6,
        grid=(n_items,),
        in_specs=[pl.BlockSpec((tm, kdim), lambda w, rt, e, jo, jw, f, a: (rt[w], 0))]
        + [pl.BlockSpec((None, kdim, tn), lambda w, rt, e, jo, jw, f, a: (e[w], 0, jw[w])) for _ in weights],
        out_specs=pl.BlockSpec((tm, tn), lambda w, rt, e, jo, jw, f, a: (rt[w], jo[w])),
        scratch_shapes=[pltpu.VMEM((kdim, tn), BF16) for _ in weights],
    )
    return pl.pallas_call(
        functools.partial(_gmm_body, n_acc=n_acc, epilogue=epilogue),
        grid_spec=grid_spec,
        out_shape=jax.ShapeDtypeStruct((p, n_out), out_dtype),
        compiler_params=_cp(("arbitrary",), vmem_mb),
        name=name,
    )(*tables, xs, *weights)


def _work_tables(counts, n_col_tiles, tm, n_tiles):
    tiles_e = (counts + tm - 1) // tm
    tile_end = jnp.cumsum(tiles_e)
    tile_start = tile_end - tiles_e
    total_tiles = tile_end[-1]
    n_items = n_col_tiles * n_tiles
    item_end = n_col_tiles * tile_end
    w = jnp.arange(n_items, dtype=jnp.int32)
    n_active = n_col_tiles * total_tiles
    active = w < n_active
    wc = jnp.minimum(w, n_active - 1)
    e = jnp.sum((wc[:, None] >= item_end[None, :]).astype(jnp.int32), axis=1)
    e = jnp.minimum(e, N_EXPERTS - 1)
    te = jnp.maximum(tiles_e[e], 1)
    local = wc - n_col_tiles * tile_start[e]
    j = local // te
    q = local % te
    rt = tile_start[e] + q
    first = ((q == 0) & active).astype(jnp.int32)
    spare = jnp.maximum(w - n_active, 0)
    rt = jnp.where(active, rt, total_tiles + spare // n_col_tiles)
    j_out = jnp.where(active, j, spare % n_col_tiles)
    i32 = lambda a: a.astype(jnp.int32)
    return (i32(rt), i32(e), i32(j_out), i32(j), first, i32(active))


def _combine_body(pos_ref, x_ref, info_ref, g_ref, y_hbm, o_ref, buf, sem, *, tm):
    i = pl.program_id(0)
    n = pl.num_programs(0)

    def gather(tile, slot):
        def issue(r, c):
            for k in range(TOP_K):
                _row_copy(y_hbm, pos_ref[(tile * tm + r) * TOP_K + k], buf.at[slot, k], r,
                          sem.at[slot]).start(priority=k % DMA_QUEUES)
            return c

        lax.fori_loop(0, tm, issue, 0, unroll=SCALAR_UNROLL)

    @pl.when(i == 0)
    def _():
        gather(0, 0)

    for slot in range(2):
        @pl.when((i + 1 < n) & ((i + 1) % 2 == slot))
        def _():
            gather(i + 1, slot)

    for slot in range(2):
        @pl.when(i % 2 == slot)
        def _():
            for k in range(TOP_K):
                _wait_rows(y_hbm, buf.at[slot, k], sem.at[slot], tm)
            info = info_ref[...]
            acc = x_ref[...] + info[:, 4:5] * buf[slot, 0] + info[:, 5:6] * buf[slot, 1]
            y = acc * lax.rsqrt(jnp.mean(acc * acc, axis=-1, keepdims=True) + NORM_EPS)
            o_ref[...] = y * g_ref[...]


def _combine(x, info, pos_flat, y_sorted, final_norm, *, tm=128):
    t, d = x.shape
    grid_spec = pltpu.PrefetchScalarGridSpec(
        num_scalar_prefetch=1,
        grid=(t // tm,),
        in_specs=[pl.BlockSpec((tm, d), lambda i, pos: (i, 0)),
                  pl.BlockSpec((tm, LANES), lambda i, pos: (i, 0)),
                  pl.BlockSpec((1, d), lambda i, pos: (0, 0)),
                  pl.BlockSpec(memory_space=pl.ANY)],
        out_specs=pl.BlockSpec((tm, d), lambda i, pos: (i, 0)),
        scratch_shapes=[pltpu.VMEM((2, TOP_K, tm, d), F32), pltpu.SemaphoreType.DMA((2,))],
    )
    return pl.pallas_call(
        functools.partial(_combine_body, tm=tm),
        grid_spec=grid_spec,
        out_shape=jax.ShapeDtypeStruct((t, d), F32),
        compiler_params=_cp(("arbitrary",), 40),
        name="moe_combine",
    )(pos_flat, x, info, final_norm.reshape(1, d).astype(F32), y_sorted)


def _even_mixers(x, tables, w_in, attn_norm, q_norm, w_uq, kv_norm, w_ukv, gate_bias, head_norm, batch, seq):
    t = x.shape[0]
    c_tab, s1_tab, s2_tab = tables
    n_ml_main = 2 * ML_HEADS * ML_QK + 2 * ML_HEADS * ML_V
    gate_lo = MLA_IN + n_ml_main
    w_a = jnp.concatenate(
        [w_in[:, :MLA_IN], jnp.zeros((D_MODEL, LANES - MLA_ROPE), w_in.dtype),
         w_in[:, gate_lo:], jnp.zeros((D_MODEL, LANES - 2 * ML_HEADS), w_in.dtype)], axis=1).astype(BF16)
    w_b = w_in[:, MLA_IN:gate_lo].astype(BF16)
    n_a = w_a.shape[1]
    w_q = jnp.pad(w_uq.reshape(MLA_Q_LORA, MLA_HEADS, MLA_NOPE + MLA_ROPE),
                  ((0, 0), (0, 0), (0, MLA_QK_PAD - MLA_NOPE - MLA_ROPE))).reshape(MLA_Q_LORA, -1).astype(BF16)
    w_kv3 = w_ukv.reshape(MLA_KV_LORA, MLA_HEADS, MLA_NOPE + MLA_V)
    w_k = jnp.pad(w_kv3[:, :, :MLA_NOPE], ((0, 0), (0, 0), (0, MLA_QK_PAD - MLA_NOPE))).reshape(MLA_KV_LORA, -1).astype(BF16)
    w_v = w_kv3[:, :, MLA_NOPE:].reshape(MLA_KV_LORA, -1).astype(BF16)

    xn = _rmsnorm(x, attn_norm, width=D_MODEL)
    u_a = _matmul([xn], [[(w_a, 0)]], n_out=n_a, tm=1024, tn=n_a // 2, out_dtype=F32, epilogue=_epi_plain,
                  name="even_in_a")
    u_b = _matmul([xn], [[(w_b, 0)]], n_out=n_ml_main, tm=1024, tn=512, out_dtype=BF16, epilogue=_epi_plain,
                  name="even_in_b")

    cqn = _rmsnorm(u_a, q_norm, width=MLA_Q_LORA, col_block=0)
    ckvn = _rmsnorm(u_a, kv_norm, width=MLA_KV_LORA, col_block=MLA_Q_LORA // MLA_KV_LORA)
    tm_p = min(1024, t)
    tab_specs = [(tab, (tm_p, LANES), lambda i, j: (i, 0)) for tab in (c_tab, s1_tab, s2_tab)]
    n_qk = MLA_HEADS * MLA_QK_PAD
    q_full = _matmul([cqn], [[(w_q, 0)]], n_out=n_qk, tm=tm_p, tn=1024, out_dtype=BF16, epilogue=_epi_mla_q,
                     extras=tab_specs, name="mla_q")
    kr_spec = (u_a, (tm_p, LANES), lambda i, j: (i, (MLA_Q_LORA + MLA_KV_LORA) // LANES))
    k_full = _matmul([ckvn], [[(w_k, 0)]], n_out=n_qk, tm=tm_p, tn=1024, out_dtype=BF16, epilogue=_epi_mla_k,
                     extras=[kr_spec] + tab_specs, name="mla_k")
    v = _matmul([ckvn], [[(w_v, 0)]], n_out=MLA_HEADS * MLA_V, tm=tm_p, tn=1024, out_dtype=BF16,
                epilogue=_epi_plain, name="mla_v")
    y_a = _mla_attention(q_full, k_full, v, batch, seq)

    bias_pad = jnp.concatenate([gate_bias.astype(F32), jnp.zeros((LANES - 2 * ML_HEADS,), F32)]).reshape(1, LANES)
    y_b = _mlstm(u_b, u_a, (MLA_IN + LANES - MLA_ROPE) // LANES, bias_pad, head_norm, batch, seq)
    return y_a, y_b


def _even_layer(x, tables, w_in, attn_norm, q_norm, w_uq, kv_norm, w_ukv, gate_bias, head_norm, w_out,
                ffn_norm, w_gate, w_up, w_down, batch, seq):
    y_a, y_b = _even_mixers(x, tables, w_in, attn_norm, q_norm, w_uq, kv_norm, w_ukv, gate_bias, head_norm,
                            batch, seq)
    res_spec = lambda arr, tm, tn: [(arr, (tm, tn), lambda i, j: (i, j))]
    x1 = _matmul([y_a, y_b], [[(w_out, 0), (w_out, 1)]], n_out=D_MODEL, tm=1024, tn=512, out_dtype=F32,
                 epilogue=_epi_residual, extras=res_spec(x, 1024, 512), name="even_out")
    hn = _rmsnorm(x1, ffn_norm, width=D_MODEL)
    a = _matmul([hn], [[(w_gate, 0)], [(w_up, 0)]], n_out=D_FF, tm=1024, tn=256, out_dtype=BF16,
                epilogue=_epi_swiglu, name="ffn_gate_up")
    x2 = _matmul([a], [[(w_down, 0)]], n_out=D_MODEL, tm=1024, tn=1024, tk=2048, out_dtype=F32,
                 epilogue=_epi_residual, extras=res_spec(x1, 1024, 1024), weight_stationary=False,
                 vmem_mb=56, name="ffn_down")
    return x2


def _odd_layer(x, w_in, attn_norm, conv_w, w_out, ffn_norm, w_router, b_router, w_gate_e, w_up_e, w_down_e,
               final_norm, batch, seq):
    t = x.shape[0]
    xn = _rmsnorm(x, attn_norm, width=D_MODEL)
    n_in = w_in.shape[1]
    u = _matmul([xn], [[(w_in, 0)]], n_out=n_in, tm=1024, tn=512, out_dtype=BF16, epilogue=_epi_plain,
                name="odd_in")
    y_c = _short_conv(u, conv_w, batch, seq)
    y_d = _sb_attention(u, 3 * SC_WIDTH // LANES, batch, seq)
    x1 = _matmul([y_c, y_d], [[(w_out, 0), (w_out, 1)]], n_out=D_MODEL, tm=1024, tn=512, out_dtype=F32,
                 epilogue=_epi_residual, extras=[(x, (1024, 512), lambda i, j: (i, j))], name="odd_out")
    return _moe(x1, ffn_norm, w_router, b_router, w_gate_e, w_up_e, w_down_e, final_norm)


def _moe(x1, ffn_norm, w_router, b_router, w_gate_e, w_up_e, w_down_e, final_norm):
    t = x1.shape[0]
    h, info, cnt = _route(x1, ffn_norm, w_router, b_router)
    tm = MOE_TM
    n_tiles = (t * TOP_K) // tm + N_EXPERTS
    counts = cnt[0, :N_EXPERTS].astype(jnp.int32)
    tiles_e = (counts + tm - 1) // tm
    offs = (jnp.cumsum(tiles_e) - tiles_e) * tm
    idx = info[:, 0:TOP_K].astype(jnp.int32)
    rank = info[:, TOP_K:2 * TOP_K].astype(jnp.int32)
    pos_flat = (offs[idx] + rank).reshape(-1)
    n_active = jnp.sum(tiles_e).astype(jnp.int32).reshape(1)

    inv = _invperm(pos_flat, n_tiles * tm)
    xs = _dispatch(h, inv, n_active, n_tiles, tm=tm)
    tn_gu, tn_d = 512, 1024
    tab_gu = _work_tables(counts, D_FF_EXPERT // tn_gu, tm, n_tiles)
    a_s = _gmm(xs, [w_gate_e, w_up_e], tab_gu, tm=tm, tn=tn_gu, out_dtype=BF16, epilogue=_epi_swiglu,
               vmem_mb=58, name="moe_gate_up")
    tab_d = _work_tables(counts, D_MODEL // tn_d, tm, n_tiles)
    y_s = _gmm(a_s, [w_down_e], tab_d, tm=tm, tn=tn_d, out_dtype=F32, epilogue=_epi_plain,
               vmem_mb=58, name="moe_down")
    return _combine(x1, info, pos_flat, y_s, final_norm)


def kernel(x, positions, even_attn_norm, even_w_in, even_q_norm, even_w_uq, even_kv_norm, even_w_ukv, even_ml_gate_bias, even_ml_head_norm, even_w_out, even_ffn_norm, even_w_gate, even_w_up, even_w_down, odd_attn_norm, odd_w_in, odd_conv_w, odd_w_out, odd_ffn_norm, odd_w_router, odd_b_router, odd_w_gate_e, odd_w_up_e, odd_w_down_e, final_norm):
    batch, seq, d = x.shape
    assert even_w_in.shape[0] == 1 and odd_w_in.shape[0] == 1, "kernel is written for one even and one odd layer"
    xf = x.reshape(batch * seq, d)
    tables = _rope_tables(positions)
    xf = _even_layer(xf, tables, even_w_in[0], even_attn_norm[0], even_q_norm[0], even_w_uq[0], even_kv_norm[0],
                     even_w_ukv[0], even_ml_gate_bias[0], even_ml_head_norm[0], even_w_out[0], even_ffn_norm[0],
                     even_w_gate[0], even_w_up[0], even_w_down[0], batch, seq)
    out = _odd_layer(xf, odd_w_in[0], odd_attn_norm[0], odd_conv_w[0], odd_w_out[0], odd_ffn_norm[0],
                     odd_w_router[0], odd_b_router[0], odd_w_gate_e[0], odd_w_up_e[0], odd_w_down_e[0],
                     final_norm, batch, seq)
    return out.reshape(batch, seq, d)
```

```python
import functools

import jax
import jax.numpy as jnp
from jax import lax
from jax.experimental import pallas as pl
from jax.experimental.pallas import tpu as pltpu

F32 = jnp.float32
BF16 = jnp.bfloat16

D_MODEL = 4096
NORM_EPS = 1e-6

MLA_HEADS = 16
MLA_Q_LORA = 1024
MLA_KV_LORA = 512
MLA_NOPE = 128
MLA_ROPE = 64
MLA_V = 128
ROPE_THETA = 10000.0
MLA_IN = MLA_Q_LORA + MLA_KV_LORA + MLA_ROPE
MLA_QK_PAD = 256

ML_HEADS = 4
ML_QK = 256
ML_V = 512
ML_CHUNK = 128
ML_VA = ML_V + 128

SC_WIDTH = 2048
SC_KERNEL = 3

SB_HEADS = 16
SB_HEAD_DIM = 128

D_FF = 11008
N_EXPERTS = 8
TOP_K = 2
D_FF_EXPERT = 4096

LANES = 128
MOE_TM = 512
DMA_QUEUES = 2
LOG2_E = 1.4426950408889634
SCALAR_UNROLL = 8


def _cp(sem, vmem_mb):
    return pltpu.CompilerParams(dimension_semantics=sem, vmem_limit_bytes=vmem_mb * 1024 * 1024)


def _dot(a, b):
    return jnp.dot(a, b, preferred_element_type=F32)


def _dot_nt(a, b):
    return lax.dot_general(a, b, (((1,), (1,)), ((), ())), preferred_element_type=F32)


def _log_sigmoid(x):
    return jnp.minimum(x, 0.0) - jnp.log1p(jnp.exp(-jnp.abs(x)))


def _split_dot(tri, x, pieces):
    acc = None
    r = x
    for p in range(pieces):
        hi = r.astype(BF16)
        part = _dot(tri, hi)
        acc = part if acc is None else acc + part
        if p + 1 < pieces:
            r = r - hi.astype(F32)
    return acc


def _rmsnorm_body(x_ref, g_ref, o_ref):
    x = x_ref[...].astype(F32)
    y = x * lax.rsqrt(jnp.mean(x * x, axis=-1, keepdims=True) + NORM_EPS)
    o_ref[...] = (y * g_ref[...]).astype(o_ref.dtype)


def _rmsnorm(x, g, *, width, col_block=0, out_dtype=BF16, tm=256):
    m = x.shape[0]
    return pl.pallas_call(
        _rmsnorm_body,
        grid=(m // tm,),
        in_specs=[pl.BlockSpec((tm, width), lambda i: (i, col_block)),
                  pl.BlockSpec((1, width), lambda i: (0, 0))],
        out_specs=pl.BlockSpec((tm, width), lambda i: (i, 0)),
        out_shape=jax.ShapeDtypeStruct((m, width), out_dtype),
        compiler_params=_cp(("parallel",), 40),
        name="rmsnorm",
    )(x, g.reshape(1, width).astype(F32))


def _mm_body(*refs, n_lhs, n_acc, n_extra, nk, k_valid_last, epilogue):
    lhs = refs[:n_lhs]
    ws = refs[n_lhs:n_lhs + n_acc * n_lhs]
    extras = refs[n_lhs + n_acc * n_lhs:n_lhs + n_acc * n_lhs + n_extra]
    out = refs[n_lhs + n_acc * n_lhs + n_extra]
    accs = refs[n_lhs + n_acc * n_lhs + n_extra + 1:]
    k = pl.program_id(2)

    def partials(mask_tail):
        parts = []
        for a in range(n_acc):
            s = None
            for l in range(n_lhs):
                x = lhs[l][...]
                w = ws[a * n_lhs + l][...]
                if mask_tail:
                    xc = lax.broadcasted_iota(jnp.int32, x.shape, 1)
                    x = jnp.where(xc < k_valid_last, x, jnp.zeros_like(x))
                    wr = lax.broadcasted_iota(jnp.int32, w.shape, 0)
                    w = jnp.where(wr < k_valid_last, w, jnp.zeros_like(w))
                p = _dot(x.astype(BF16), w.astype(BF16))
                s = p if s is None else s + p
            parts.append(s)
        return parts

    if nk == 1:
        out[...] = epilogue(partials(False), extras).astype(out.dtype)
        return

    ragged = k_valid_last is not None

    @pl.when(k == 0)
    def _():
        for a, p in enumerate(partials(False)):
            accs[a][...] = p

    @pl.when((k > 0) & (k < nk - 1) if ragged else (k > 0))
    def _():
        for a, p in enumerate(partials(False)):
            accs[a][...] += p

    if ragged:
        @pl.when(k == nk - 1)
        def _():
            for a, p in enumerate(partials(True)):
                accs[a][...] += p

    @pl.when(k == nk - 1)
    def _():
        out[...] = epilogue([acc[...] for acc in accs], extras).astype(out.dtype)


def _matmul(lhs, weights, *, n_out, tm, tn, tk, out_dtype, epilogue, extras=(), vmem_mb=48, name="matmul"):
    m = lhs[0].shape[0]
    kdim = lhs[0].shape[1]
    tm = min(tm, m)
    nk = pl.cdiv(kdim, tk)
    k_valid_last = None if kdim % tk == 0 else kdim - (nk - 1) * tk
    n_lhs, n_acc = len(lhs), len(weights)
    grid = (m // tm, n_out // tn, nk)
    ij = lambda g0, g1: (g0, g1)

    in_specs, args = [], []
    for x in lhs:
        in_specs.append(pl.BlockSpec((tm, tk), lambda g0, g1, k: (ij(g0, g1)[0], k)))
        args.append(x)
    for wl in weights:
        for (w, rb) in wl:
            in_specs.append(pl.BlockSpec((tk, tn), lambda g0, g1, k, rb=rb: (rb * nk + k, ij(g0, g1)[1])))
            args.append(w)
    for (arr, bshape, imap) in extras:
        in_specs.append(pl.BlockSpec(bshape, lambda g0, g1, k, imap=imap: imap(*ij(g0, g1))))
        args.append(arr)
    scratch = [pltpu.VMEM((tm, tn), F32) for _ in range(n_acc)] if nk > 1 else []
    body = functools.partial(_mm_body, n_lhs=n_lhs, n_acc=n_acc, n_extra=len(extras), nk=nk,
                             k_valid_last=k_valid_last, epilogue=epilogue)
    return pl.pallas_call(
        body,
        grid=grid,
        in_specs=in_specs,
        out_specs=pl.BlockSpec((tm, tn), lambda g0, g1, k: ij(g0, g1)),
        out_shape=jax.ShapeDtypeStruct((m, n_out), out_dtype),
        scratch_shapes=scratch,
        compiler_params=_cp(("parallel", "parallel", "arbitrary"), vmem_mb),
        name=name,
    )(*args)


def _mm_ws_body(*refs, n_lhs, n_acc, n_extra, tn, kdims, row_blocks, cast, epilogue):
    n_w = n_acc * n_lhs
    lhs = refs[:n_lhs]
    w_hbm = refs[n_lhs:n_lhs + n_w]
    extras = refs[n_lhs + n_w:n_lhs + n_w + n_extra]
    out = refs[n_lhs + n_w + n_extra]
    scratch = refs[n_lhs + n_w + n_extra + 1:]
    stage, sem, wbf = scratch[:n_w], scratch[n_w], scratch[n_w + 1:]
    j = pl.program_id(0)
    i = pl.program_id(1)
    nj = pl.num_programs(0)
    slot = j % 2

    def fetch(widx, jj, s):
        kd = kdims[widx % n_lhs]
        col = pl.multiple_of(jj * tn, tn)
        src = w_hbm[widx].at[pl.ds(row_blocks[widx] * kd, kd), pl.ds(col, tn)]
        return pltpu.make_async_copy(src, stage[widx].at[s], sem.at[widx, s])

    @pl.when(i == 0)
    def _():
        @pl.when(j == 0)
        def _():
            for widx in range(n_w):
                fetch(widx, 0, 0).start()

        @pl.when(j + 1 < nj)
        def _():
            for widx in range(n_w):
                fetch(widx, j + 1, 1 - slot).start()

        for widx in range(n_w):
            fetch(widx, j, slot).wait()
            if cast:
                wbf[widx][...] = stage[widx][slot].astype(BF16)

    parts = []
    for a in range(n_acc):
        s = None
        for l in range(n_lhs):
            widx = a * n_lhs + l
            w = wbf[widx][...] if cast else stage[widx][slot]
            p = _dot(lhs[l][...].astype(BF16), w)
            s = p if s is None else s + p
        parts.append(s)
    out[...] = epilogue(parts, extras).astype(out.dtype)


def _matmul_ws(lhs, weights, *, n_out, tm, tn, out_dtype, epilogue, extras=(), vmem_mb=48, name="matmul_ws"):
    m = lhs[0].shape[0]
    tm = min(tm, m)
    n_lhs, n_acc = len(lhs), len(weights)
    kdims = tuple(x.shape[1] for x in lhs)
    flat_w = [w for wl in weights for (w, _) in wl]
    row_blocks = tuple(rb for wl in weights for (_, rb) in wl)
    wdtype = flat_w[0].dtype
    cast = wdtype != BF16
    in_specs = [pl.BlockSpec((tm, kd), lambda j, i: (i, 0)) for kd in kdims]
    in_specs += [pl.BlockSpec(memory_space=pl.ANY) for _ in flat_w]
    in_specs += [pl.BlockSpec(bshape, lambda j, i, imap=imap: imap(i, j)) for (_, bshape, imap) in extras]
    w_kdims = [kdims[widx % n_lhs] for widx in range(len(flat_w))]
    scratch = [pltpu.VMEM((2, kd, tn), wdtype) for kd in w_kdims]
    scratch.append(pltpu.SemaphoreType.DMA((len(flat_w), 2)))
    if cast:
        scratch += [pltpu.VMEM((kd, tn), BF16) for kd in w_kdims]
    body = functools.partial(_mm_ws_body, n_lhs=n_lhs, n_acc=n_acc, n_extra=len(extras), tn=tn, kdims=kdims,
                             row_blocks=row_blocks, cast=cast, epilogue=epilogue)
    return pl.pallas_call(
        body,
        grid=(n_out // tn, m // tm),
        in_specs=in_specs,
        out_specs=pl.BlockSpec((tm, tn), lambda j, i: (i, j)),
        out_shape=jax.ShapeDtypeStruct((m, n_out), out_dtype),
        scratch_shapes=scratch,
        compiler_params=_cp(("arbitrary", "arbitrary"), vmem_mb),
        name=name,
    )(*lhs, *flat_w, *[arr for (arr, _, _) in extras])


def _epi_plain(parts, extras):
    return parts[0]


def _epi_residual(parts, extras):
    return parts[0] + extras[0][...]


def _epi_swiglu(parts, extras):
    g, u = parts
    return g * jax.nn.sigmoid(g) * u


def _rope_tables_body(pos_ref, invf_ref, c_ref, s1_ref, s2_ref):
    ang = pos_ref[...] * invf_ref[...]
    lane = lax.broadcasted_iota(jnp.int32, ang.shape, 1)
    cos = jnp.cos(ang)
    sin = jnp.sin(ang)
    half = MLA_ROPE // 2
    c_ref[...] = jnp.where(lane < MLA_ROPE, cos, 0.0)
    s1_ref[...] = jnp.where(lane < half, -sin, 0.0)
    s2_ref[...] = jnp.where((lane >= half) & (lane < MLA_ROPE), sin, 0.0)


def _rope_tables(positions):
    t = positions.size
    half = MLA_ROPE // 2
    inv_freq = 1.0 / (ROPE_THETA ** (jnp.arange(0, MLA_ROPE, 2, dtype=F32) / MLA_ROPE))
    invf = jnp.concatenate([inv_freq, inv_freq, jnp.zeros((LANES - 2 * half,), F32)]).reshape(1, LANES)
    pos = jnp.broadcast_to(positions.reshape(t, 1).astype(F32), (t, LANES))
    tm = min(512, t)
    spec = pl.BlockSpec((tm, LANES), lambda i: (i, 0))
    sds = jax.ShapeDtypeStruct((t, LANES), F32)
    return pl.pallas_call(
        _rope_tables_body,
        grid=(t // tm,),
        in_specs=[spec, pl.BlockSpec((1, LANES), lambda i: (0, 0))],
        out_specs=[spec, spec, spec],
        out_shape=[sds, sds, sds],
        compiler_params=_cp(("parallel",), 32),
        name="rope_tables",
    )(pos, invf)


def _rope_lanes(t, c, s1, s2):
    return t * c + pltpu.roll(t, LANES - MLA_ROPE // 2, axis=1) * s1 + pltpu.roll(t, MLA_ROPE // 2, axis=1) * s2


def _epi_mla_q(parts, extras):
    acc = parts[0]
    c, s1, s2 = extras[0][...], extras[1][...], extras[2][...]
    scale = (MLA_NOPE + MLA_ROPE) ** -0.5
    pieces = []
    for h in range(acc.shape[1] // MLA_QK_PAD):
        base = h * MLA_QK_PAD
        pieces.append(acc[:, base:base + MLA_NOPE] * scale)
        pieces.append(_rope_lanes(acc[:, base + MLA_NOPE:base + MLA_QK_PAD], c, s1, s2) * scale)
    return jnp.concatenate(pieces, axis=1)


def _epi_mla_k(parts, extras):
    acc = parts[0]
    kr = _rope_lanes(extras[0][...], extras[1][...], extras[2][...], extras[3][...])
    pieces = []
    for h in range(acc.shape[1] // MLA_QK_PAD):
        base = h * MLA_QK_PAD
        pieces.append(acc[:, base:base + MLA_NOPE])
        pieces.append(acc[:, base + MLA_NOPE:base + MLA_QK_PAD] + kr)
    return jnp.concatenate(pieces, axis=1)


def _mla_attn_body(q_ref, k_ref, v_ref, o_ref, *, tq, seq):
    row = lax.broadcasted_iota(jnp.int32, (tq, tq), 0)
    col = lax.broadcasted_iota(jnp.int32, (tq, tq), 1)
    causal = col <= row
    for qi in range(seq // tq):
        n = (qi + 1) * tq
        q = q_ref[qi * tq:(qi + 1) * tq, :]
        s = _dot_nt(q, k_ref[0:n, :])
        diag = jnp.where(causal, s[:, n - tq:], -jnp.inf)
        s = diag if qi == 0 else jnp.concatenate([s[:, :n - tq], diag], axis=1)
        m = jnp.max(s, axis=1, keepdims=True)
        p = jnp.exp(s - m)
        l = jnp.sum(p, axis=1, keepdims=True)
        o = _dot(p.astype(BF16), v_ref[0:n, :]) / l
        o_ref[qi * tq:(qi + 1) * tq, :] = o.astype(o_ref.dtype)


def _mla_attention(q, k, v, batch, seq, *, tq=256):
    t = q.shape[0]
    tq = min(tq, seq)
    body = functools.partial(_mla_attn_body, tq=tq, seq=seq)
    return pl.pallas_call(
        body,
        grid=(batch, MLA_HEADS),
        in_specs=[pl.BlockSpec((seq, MLA_QK_PAD), lambda b, h: (b, h)),
                  pl.BlockSpec((seq, MLA_QK_PAD), lambda b, h: (b, h)),
                  pl.BlockSpec((seq, MLA_V), lambda b, h: (b, h))],
        out_specs=pl.BlockSpec((seq, MLA_V), lambda b, h: (b, h)),
        out_shape=jax.ShapeDtypeStruct((t, MLA_HEADS * MLA_V), BF16),
        compiler_params=_cp(("parallel", "parallel"), 48),
        name="mla_attention",
    )(q, k, v)


def _mlstm_body(q_ref, k_ref, v_ref, o_ref, g_ref, bias_ref, hn_ref, y_ref, ct_ref, m_ref):
    L = ML_CHUNK
    c = pl.program_id(1)

    @pl.when(c == 0)
    def _():
        ct_ref[...] = jnp.zeros(ct_ref.shape, F32)
        m_ref[...] = jnp.zeros(m_ref.shape, F32)

    g = g_ref[...] + bias_ref[...]
    logf = _log_sigmoid(g)
    row = lax.broadcasted_iota(jnp.int32, (L, L), 0)
    col = lax.broadcasted_iota(jnp.int32, (L, L), 1)
    causal = col <= row
    tri = jnp.where(causal, 1.0, 0.0).astype(BF16)
    cum = _split_dot(tri, logf, 3)
    lane = lax.broadcasted_iota(jnp.int32, (L, LANES), 1)
    colq = jnp.where(lane < ML_HEADS, g, cum)
    rowq = colq.T
    ones_col = jnp.where(lane == 0, 1.0, 0.0).astype(BF16)
    scale = ML_QK ** -0.5

    for h in range(ML_HEADS):
        i_c = colq[:, h:h + 1]
        cf_c = colq[:, ML_HEADS + h:ML_HEADS + h + 1]
        i_r = rowq[h:h + 1, :]
        cf_r = rowq[ML_HEADS + h:ML_HEADS + h + 1, :]
        m_prev = m_ref[h][:, 0:1]
        d = jnp.where(causal, cf_c - cf_r + i_r, -jnp.inf)
        m_inter = cf_c + m_prev
        m_t = jnp.maximum(m_inter, jnp.max(d, axis=1, keepdims=True))
        inter = jnp.exp(m_inter - m_t)
        p = jnp.exp(d - m_t)
        qh = q_ref[:, h * ML_QK:(h + 1) * ML_QK]
        kf = k_ref[:, h * ML_QK:(h + 1) * ML_QK].astype(F32) * scale
        kh = kf.astype(BF16)
        s = _dot_nt(qh, kh) * p
        va = jnp.concatenate([v_ref[:, h * ML_V:(h + 1) * ML_V], ones_col], axis=1)
        ct = ct_ref[h]
        na = _dot(s.astype(BF16), va) + inter * _dot(qh, ct.astype(BF16))
        num = na[:, :ML_V]
        den = na[:, ML_V:ML_V + 1]
        hh = num / jnp.maximum(jnp.abs(den), jnp.exp(-m_t))

        f_tot = cf_c[L - 1:L, :]
        w_log = f_tot - cf_c + i_c
        m_new = jnp.maximum(f_tot + m_prev, jnp.max(w_log, axis=0, keepdims=True))
        decay = jnp.exp(f_tot + m_prev - m_new)
        w = jnp.exp(w_log - m_new)
        wv = (va.astype(F32) * w).astype(BF16)
        ct_ref[h] = decay * ct + _dot(kf.T.astype(BF16), wv)
        m_ref[h] = jnp.broadcast_to(m_new, (1, LANES))

        hn = hh * lax.rsqrt(jnp.mean(hh * hh, axis=1, keepdims=True) + NORM_EPS) * hn_ref[:, h * ML_V:(h + 1) * ML_V]
        og = o_ref[:, h * ML_V:(h + 1) * ML_V].astype(F32)
        y_ref[:, h * ML_V:(h + 1) * ML_V] = (jax.nn.sigmoid(og) * hn).astype(y_ref.dtype)


def _mlstm(u_b, gates_src, gate_col_block, bias_pad, head_norm, batch, seq):
    t = u_b.shape[0]
    nc = seq // ML_CHUNK
    nqk = ML_HEADS * ML_QK
    nv = ML_HEADS * ML_V
    rows = lambda b, c: b * nc + c
    return pl.pallas_call(
        _mlstm_body,
        grid=(batch, nc),
        in_specs=[pl.BlockSpec((ML_CHUNK, nqk), lambda b, c: (rows(b, c), 0)),
                  pl.BlockSpec((ML_CHUNK, nqk), lambda b, c: (rows(b, c), 1)),
                  pl.BlockSpec((ML_CHUNK, nv), lambda b, c: (rows(b, c), 1)),
                  pl.BlockSpec((ML_CHUNK, nv), lambda b, c: (rows(b, c), 2)),
                  pl.BlockSpec((ML_CHUNK, LANES), lambda b, c: (rows(b, c), gate_col_block)),
                  pl.BlockSpec((1, LANES), lambda b, c: (0, 0)),
                  pl.BlockSpec((1, nv), lambda b, c: (0, 0))],
        out_specs=pl.BlockSpec((ML_CHUNK, nv), lambda b, c: (rows(b, c), 0)),
        out_shape=jax.ShapeDtypeStruct((t, nv), BF16),
        scratch_shapes=[pltpu.VMEM((ML_HEADS, ML_QK, ML_VA), F32), pltpu.VMEM((ML_HEADS, 1, LANES), F32)],
        compiler_params=_cp(("parallel", "arbitrary"), 32),
        name="mlstm",
    )(u_b, u_b, u_b, u_b, gates_src, bias_pad, head_norm.reshape(1, nv).astype(F32))


def _conv_body(b_ref, c_ref, h_ref, cp_ref, hp_ref, w_ref, y_ref, z_sc, *, ts, halo):
    i = pl.program_id(1)
    z = c_ref[...].astype(F32) * h_ref[...].astype(F32)
    zp = cp_ref[...].astype(F32) * hp_ref[...].astype(F32)
    z_sc[0:halo, :] = jnp.where(i > 0, zp, 0.0)
    z_sc[halo:halo + ts, :] = z
    w = w_ref[...]
    y = w[2:3, :] * z + w[1:2, :] * z_sc[halo - 1:halo - 1 + ts, :] + w[0:1, :] * z_sc[halo - 2:halo - 2 + ts, :]
    y_ref[...] = (b_ref[...].astype(F32) * y).astype(y_ref.dtype)


def _short_conv(u, conv_w, batch, seq, *, ts=256, tw=1024):
    t = u.shape[0]
    halo = 16
    ns = seq // ts
    nw = SC_WIDTH // tw
    wpad = jnp.concatenate([conv_w.astype(F32), jnp.zeros((8 - SC_KERNEL, SC_WIDTH), F32)], axis=0)
    rows = lambda b, i: b * ns + i
    prev = lambda b, i: jnp.maximum((b * seq + i * ts) // halo - 1, 0)
    body = functools.partial(_conv_body, ts=ts, halo=halo)
    return pl.pallas_call(
        body,
        grid=(batch, ns, nw),
        in_specs=[pl.BlockSpec((ts, tw), lambda b, i, j: (rows(b, i), j)),
                  pl.BlockSpec((ts, tw), lambda b, i, j: (rows(b, i), nw + j)),
                  pl.BlockSpec((ts, tw), lambda b, i, j: (rows(b, i), 2 * nw + j)),
                  pl.BlockSpec((halo, tw), lambda b, i, j: (prev(b, i), nw + j)),
                  pl.BlockSpec((halo, tw), lambda b, i, j: (prev(b, i), 2 * nw + j)),
                  pl.BlockSpec((8, tw), lambda b, i, j: (0, j))],
        out_specs=pl.BlockSpec((ts, tw), lambda b, i, j: (rows(b, i), j)),
        out_shape=jax.ShapeDtypeStruct((t, SC_WIDTH), BF16),
        scratch_shapes=[pltpu.VMEM((halo + ts, tw), F32)],
        compiler_params=_cp(("parallel", "parallel", "parallel"), 32),
        name="short_conv",
    )(u, u, u, u, u, wpad)


def _sb_body(q_ref, k_ref, v_ref, o_ref, *, tq, seq):
    scale = SB_HEAD_DIM ** -0.5 * LOG2_E
    row = lax.broadcasted_iota(jnp.int32, (tq, tq), 0)
    col = lax.broadcasted_iota(jnp.int32, (tq, tq), 1)
    strict = col < row
    upper = jnp.where(row > col, 1.0, 0.0).astype(BF16)
    for qi in range(seq // tq):
        nb = qi + 1
        q = q_ref[qi * tq:(qi + 1) * tq, :]
        z = _dot_nt(q, k_ref[0:nb * tq, :]) * scale
        sp = jnp.maximum(z, 0.0) + jnp.log2(1.0 + jnp.exp2(-jnp.abs(z)))
        later = jnp.zeros((tq, 1), F32)
        blocks = [None] * nb
        for j in range(nb - 1, -1, -1):
            zj = z[:, j * tq:(j + 1) * tq]
            spj = sp[:, j * tq:(j + 1) * tq]
            lk = -spj
            if j == nb - 1:
                lk = jnp.where(strict, lk, 0.0)
            between = _split_dot_rhs(lk, upper, 2)
            e = jnp.exp2(zj - spj + between + later)
            if j == nb - 1:
                e = jnp.where(strict, e, 0.0)
            blocks[j] = e.astype(BF16)
            later = later + (between[:, 0:1] + lk[:, 0:1])
        a = blocks[0] if nb == 1 else jnp.concatenate(blocks, axis=1)
        o_ref[qi * tq:(qi + 1) * tq, :] = _dot(a, v_ref[0:nb * tq, :]).astype(o_ref.dtype)


def _split_dot_rhs(x, tri, pieces):
    acc = None
    r = x
    for p in range(pieces):
        hi = r.astype(BF16)
        part = _dot(hi, tri)
        acc = part if acc is None else acc + part
        if p + 1 < pieces:
            r = r - hi.astype(F32)
    return acc


def _sb_attention(u, col_base, batch, seq, *, tq=256):
    t = u.shape[0]
    tq = min(tq, seq)
    body = functools.partial(_sb_body, tq=tq, seq=seq)
    spec = lambda off: pl.BlockSpec((seq, SB_HEAD_DIM), lambda b, h: (b, col_base + off + h))
    return pl.pallas_call(
        body,
        grid=(batch, SB_HEADS),
        in_specs=[spec(0), spec(SB_HEADS), spec(2 * SB_HEADS)],
        out_specs=pl.BlockSpec((seq, SB_HEAD_DIM), lambda b, h: (b, h)),
        out_shape=jax.ShapeDtypeStruct((t, SB_HEADS * SB_HEAD_DIM), BF16),
        compiler_params=_cp(("parallel", "parallel"), 48),
        name="sb_attention",
    )(u, u, u)


def _route_body(x_ref, g_ref, wr_ref, br_ref, h_ref, info_ref, cnt_ref, carry):
    i = pl.program_id(0)

    @pl.when(i == 0)
    def _():
        carry[...] = jnp.zeros(carry.shape, F32)

    x = x_ref[...]
    h = x * lax.rsqrt(jnp.mean(x * x, axis=-1, keepdims=True) + NORM_EPS) * g_ref[...]
    h_ref[...] = h
    wr = wr_ref[...]
    h_hi = h.astype(BF16)
    h_lo = (h - h_hi.astype(F32)).astype(BF16)
    w_hi = wr.astype(BF16)
    w_lo = (wr - w_hi.astype(F32)).astype(BF16)
    logits = (_dot(h_hi, w_hi) + _dot(h_hi, w_lo)) + (_dot(h_lo, w_hi) + _dot(h_lo, w_lo)) + br_ref[...]
    tm = logits.shape[0]
    lane = lax.broadcasted_iota(jnp.int32, logits.shape, 1).astype(F32)
    lg = jnp.where(lane < N_EXPERTS, logits, -jnp.inf)
    m1 = jnp.max(lg, axis=1, keepdims=True)
    i1 = jnp.min(jnp.where(lg == m1, lane, float(LANES)), axis=1, keepdims=True)
    lg2 = jnp.where(lane == i1, -jnp.inf, lg)
    m2 = jnp.max(lg2, axis=1, keepdims=True)
    i2 = jnp.min(jnp.where(lg2 == m2, lane, float(LANES)), axis=1, keepdims=True)
    e = jnp.exp(m2 - m1)
    w1 = 1.0 / (1.0 + e)
    w2 = e / (1.0 + e)
    oh1 = lane == i1
    oh2 = lane == i2
    mask = jnp.where(oh1 | oh2, 1.0, 0.0)
    row = lax.broadcasted_iota(jnp.int32, (tm, tm), 0)
    col = lax.broadcasted_iota(jnp.int32, (tm, tm), 1)
    before = jnp.where(col < row, 1.0, 0.0).astype(BF16)
    rank_mat = _dot(before, mask.astype(BF16)) + carry[...]
    r1 = jnp.sum(jnp.where(oh1, rank_mat, 0.0), axis=1, keepdims=True)
    r2 = jnp.sum(jnp.where(oh2, rank_mat, 0.0), axis=1, keepdims=True)
    carry[...] += jnp.sum(mask, axis=0, keepdims=True)
    info = jnp.where(lane == 0, i1,
           jnp.where(lane == 1, i2,
           jnp.where(lane == 2, r1,
           jnp.where(lane == 3, r2,
           jnp.where(lane == 4, w1,
           jnp.where(lane == 5, w2, 0.0))))))
    info_ref[...] = info
    cnt_ref[...] = jnp.broadcast_to(carry[...], cnt_ref.shape)


def _route(x, g, w_router, b_router, *, tm=256):
    t, d = x.shape
    wr = jnp.concatenate([w_router.astype(F32), jnp.zeros((d, LANES - N_EXPERTS), F32)], axis=1)
    br = jnp.concatenate([b_router.astype(F32), jnp.zeros((LANES - N_EXPERTS,), F32)]).reshape(1, LANES)
    return pl.pallas_call(
        _route_body,
        grid=(t // tm,),
        in_specs=[pl.BlockSpec((tm, d), lambda i: (i, 0)),
                  pl.BlockSpec((1, d), lambda i: (0, 0)),
                  pl.BlockSpec((d, LANES), lambda i: (0, 0)),
                  pl.BlockSpec((1, LANES), lambda i: (0, 0))],
        out_specs=[pl.BlockSpec((tm, d), lambda i: (i, 0)),
                   pl.BlockSpec((tm, LANES), lambda i: (i, 0)),
                   pl.BlockSpec((8, LANES), lambda i: (0, 0))],
        out_shape=[jax.ShapeDtypeStruct((t, d), F32),
                   jax.ShapeDtypeStruct((t, LANES), F32),
                   jax.ShapeDtypeStruct((8, LANES), F32)],
        scratch_shapes=[pltpu.VMEM((1, LANES), F32)],
        compiler_params=_cp(("arbitrary",), 40),
        name="moe_route",
    )(x, g.reshape(1, d).astype(F32), wr, br)


def _invperm_body(pos_ref, inv_ref, *, n_assign, n_slots):
    def clear(p, c):
        inv_ref[p] = 0
        return c

    lax.fori_loop(0, n_slots, clear, 0, unroll=SCALAR_UNROLL)

    def put(a, c):
        inv_ref[pos_ref[a]] = lax.shift_right_logical(a, TOP_K.bit_length() - 1)
        return c

    lax.fori_loop(0, n_assign, put, 0, unroll=SCALAR_UNROLL)


def _invperm(pos_flat, n_slots):
    n_assign = pos_flat.shape[0]
    return pl.pallas_call(
        functools.partial(_invperm_body, n_assign=n_assign, n_slots=n_slots),
        in_specs=[pl.BlockSpec(memory_space=pltpu.SMEM)],
        out_specs=pl.BlockSpec(memory_space=pltpu.SMEM),
        out_shape=jax.ShapeDtypeStruct((n_slots,), jnp.int32),
        name="moe_invperm",
    )(pos_flat)


def _row_copy(src_hbm, row, dst_vmem, slot, sem):
    return pltpu.make_async_copy(src_hbm.at[pl.ds(row, 1), :], dst_vmem.at[pl.ds(slot, 1), :], sem)


def _wait_rows(src_hbm, dst_vmem, sem, n):
    def wait(r, c):
        _row_copy(src_hbm, 0, dst_vmem, r, sem).wait()
        return c

    lax.fori_loop(0, n, wait, 0, unroll=SCALAR_UNROLL)


def _dispatch_body(inv_ref, nact_ref, h_hbm, o_ref, buf, sem, *, tm):
    i = pl.program_id(0)
    nact = nact_ref[0]

    def gather(tile, slot):
        def issue(rr, c):
            for u in range(DMA_QUEUES):
                r = rr * DMA_QUEUES + u
                _row_copy(h_hbm, inv_ref[tile * tm + r], buf.at[slot], r, sem.at[slot]).start(priority=u)
            return c

        lax.fori_loop(0, tm // DMA_QUEUES, issue, 0, unroll=SCALAR_UNROLL // DMA_QUEUES)

    @pl.when((i == 0) & (nact > 0))
    def _():
        gather(0, 0)

    for slot in range(2):
        @pl.when((i + 1 < nact) & ((i + 1) % 2 == slot))
        def _():
            gather(i + 1, slot)

    for slot in range(2):
        @pl.when((i < nact) & (i % 2 == slot))
        def _():
            _wait_rows(h_hbm, buf.at[slot], sem.at[slot], tm)
            o_ref[...] = buf[slot].astype(o_ref.dtype)

    @pl.when(i >= nact)
    def _():
        o_ref[...] = jnp.zeros(o_ref.shape, o_ref.dtype)


def _dispatch(h, inv, n_active_tiles, n_tiles, *, tm):
    d = h.shape[1]
    grid_spec = pltpu.PrefetchScalarGridSpec(
        num_scalar_prefetch=2,
        grid=(n_tiles,),
        in_specs=[pl.BlockSpec(memory_space=pl.ANY)],
        out_specs=pl.BlockSpec((tm, d), lambda i, inv, na: (i, 0)),
        scratch_shapes=[pltpu.VMEM((2, tm, d), F32), pltpu.SemaphoreType.DMA((2,))],
    )
    return pl.pallas_call(
        functools.partial(_dispatch_body, tm=tm),
        grid_spec=grid_spec,
        out_shape=jax.ShapeDtypeStruct((n_tiles * tm, d), BF16),
        compiler_params=_cp(("arbitrary",), 40),
        name="moe_dispatch",
    )(inv, n_active_tiles, h)


def _gmm_body(rt_ref, e_ref, jo_ref, jw_ref, first_ref, act_ref, slot_ref, ne_ref, nj_ref, hn_ref, x_ref, *refs,
              n_acc, tn, epilogue):
    w_hbm = refs[:n_acc]
    out = refs[n_acc]
    scratch = refs[n_acc + 1:]
    stage, sem, wbf = scratch[:n_acc], scratch[n_acc], scratch[n_acc + 1:]
    w = pl.program_id(0)

    def fetch(a, expert, col_tile, s):
        col = pl.multiple_of(col_tile * tn, tn)
        return pltpu.make_async_copy(w_hbm[a].at[expert, :, pl.ds(col, tn)], stage[a].at[s], sem.at[a, s])

    @pl.when(first_ref[w] == 1)
    def _():
        s = slot_ref[w]

        @pl.when(w == 0)
        def _():
            for a in range(n_acc):
                fetch(a, e_ref[0], jw_ref[0], 0).start()

        @pl.when(hn_ref[w] == 1)
        def _():
            for a in range(n_acc):
                fetch(a, ne_ref[w], nj_ref[w], 1 - s).start()

        for a in range(n_acc):
            fetch(a, e_ref[w], jw_ref[w], s).wait()
            wbf[a][...] = stage[a][s].astype(BF16)

    @pl.when(act_ref[w] == 1)
    def _():
        x = x_ref[...].astype(BF16)
        out[...] = epilogue([_dot(x, wbf[a][...]) for a in range(n_acc)], ()).astype(out.dtype)

    @pl.when(act_ref[w] == 0)
    def _():
        out[...] = jnp.zeros(out.shape, out.dtype)


def _gmm(xs, weights, tables, *, tm, tn, out_dtype, epilogue, vmem_mb, name):
    p, kdim = xs.shape
    n_out = weights[0].shape[2]
    n_items = tables[0].shape[0]
    n_acc = len(weights)
    grid_spec = pltpu.PrefetchScalarGridSpec(
        num_scalar_prefetch=len(tables),
        grid=(n_items,),
        in_specs=[pl.BlockSpec((tm, kdim), lambda w, rt, *_: (rt[w], 0))]
        + [pl.BlockSpec(memory_space=pl.ANY) for _ in weights],
        out_specs=pl.BlockSpec((tm, tn), lambda w, rt, e, jo, *_: (rt[w], jo[w])),
        scratch_shapes=[pltpu.VMEM((2, kdim, tn), weights[0].dtype) for _ in weights]
        + [pltpu.SemaphoreType.DMA((n_acc, 2))]
        + [pltpu.VMEM((kdim, tn), BF16) for _ in weights],
    )
    return pl.pallas_call(
        functools.partial(_gmm_body, n_acc=n_acc, tn=tn, epilogue=epilogue),
        grid_spec=grid_spec,
        out_shape=jax.ShapeDtypeStruct((p, n_out), out_dtype),
        compiler_params=_cp(("arbitrary",), vmem_mb),
        name=name,
    )(*tables, xs, *weights)


def _work_tables(counts, n_col_tiles, tm, n_tiles):
    tiles_e = (counts + tm - 1) // tm
    tile_end = jnp.cumsum(tiles_e)
    tile_start = tile_end - tiles_e
    total_tiles = tile_end[-1]
    n_items = n_col_tiles * n_tiles
    item_end = n_col_tiles * tile_end
    w = jnp.arange(n_items, dtype=jnp.int32)
    n_active = n_col_tiles * total_tiles
    active = w < n_active
    wc = jnp.minimum(w, n_active - 1)
    e = jnp.sum((wc[:, None] >= item_end[None, :]).astype(jnp.int32), axis=1)
    e = jnp.minimum(e, N_EXPERTS - 1)
    te = jnp.maximum(tiles_e[e], 1)
    local = wc - n_col_tiles * tile_start[e]
    j = local // te
    q = local % te
    rt = tile_start[e] + q
    first = (q == 0) & active
    slot = (jnp.cumsum(first.astype(jnp.int32)) - 1) % 2
    nxt = lax.cummin(jnp.where(first, w, n_items), axis=0, reverse=True)
    nxt = jnp.concatenate([nxt[1:], jnp.full((1,), n_items, nxt.dtype)])
    has_next = nxt < n_items
    nxt = jnp.minimum(nxt, n_items - 1)
    spare = jnp.maximum(w - n_active, 0)
    rt = jnp.where(active, rt, total_tiles + spare // n_col_tiles)
    j_out = jnp.where(active, j, spare % n_col_tiles)
    i32 = lambda a: a.astype(jnp.int32)
    return (i32(rt), i32(e), i32(j_out), i32(j), i32(first), i32(active), i32(slot), i32(e[nxt]), i32(j[nxt]),
            i32(has_next))


def _combine_body(pos_ref, x_ref, info_ref, g_ref, y_hbm, o_ref, buf, sem, *, tm):
    i = pl.program_id(0)
    n = pl.num_programs(0)

    def gather(tile, slot):
        def issue(r, c):
            for k in range(TOP_K):
                _row_copy(y_hbm, pos_ref[(tile * tm + r) * TOP_K + k], buf.at[slot, k], r,
                          sem.at[slot]).start(priority=k % DMA_QUEUES)
            return c

        lax.fori_loop(0, tm, issue, 0, unroll=SCALAR_UNROLL)

    @pl.when(i == 0)
    def _():
        gather(0, 0)

    for slot in range(2):
        @pl.when((i + 1 < n) & ((i + 1) % 2 == slot))
        def _():
            gather(i + 1, slot)

    for slot in range(2):
        @pl.when(i % 2 == slot)
        def _():
            for k in range(TOP_K):
                _wait_rows(y_hbm, buf.at[slot, k], sem.at[slot], tm)
            info = info_ref[...]
            acc = x_ref[...] + info[:, 4:5] * buf[slot, 0] + info[:, 5:6] * buf[slot, 1]
            y = acc * lax.rsqrt(jnp.mean(acc * acc, axis=-1, keepdims=True) + NORM_EPS)
            o_ref[...] = y * g_ref[...]


def _combine(x, info, pos_flat, y_sorted, final_norm, *, tm=128):
    t, d = x.shape
    grid_spec = pltpu.PrefetchScalarGridSpec(
        num_scalar_prefetch=1,
        grid=(t // tm,),
        in_specs=[pl.BlockSpec((tm, d), lambda i, pos: (i, 0)),
                  pl.BlockSpec((tm, LANES), lambda i, pos: (i, 0)),
                  pl.BlockSpec((1, d), lambda i, pos: (0, 0)),
                  pl.BlockSpec(memory_space=pl.ANY)],
        out_specs=pl.BlockSpec((tm, d), lambda i, pos: (i, 0)),
        scratch_shapes=[pltpu.VMEM((2, TOP_K, tm, d), F32), pltpu.SemaphoreType.DMA((2,))],
    )
    return pl.pallas_call(
        functools.partial(_combine_body, tm=tm),
        grid_spec=grid_spec,
        out_shape=jax.ShapeDtypeStruct((t, d), F32),
        compiler_params=_cp(("arbitrary",), 40),
        name="moe_combine",
    )(pos_flat, x, info, final_norm.reshape(1, d).astype(F32), y_sorted)


def _even_mixers(x, tables, w_in, attn_norm, q_norm, w_uq, kv_norm, w_ukv, gate_bias, head_norm, batch, seq):
    t = x.shape[0]
    c_tab, s1_tab, s2_tab = tables
    n_ml_main = 2 * ML_HEADS * ML_QK + 2 * ML_HEADS * ML_V
    gate_lo = MLA_IN + n_ml_main
    w_a = jnp.concatenate(
        [w_in[:, :MLA_IN], jnp.zeros((D_MODEL, LANES - MLA_ROPE), w_in.dtype),
         w_in[:, gate_lo:], jnp.zeros((D_MODEL, LANES - 2 * ML_HEADS), w_in.dtype)], axis=1).astype(BF16)
    w_b = w_in[:, MLA_IN:gate_lo].astype(BF16)
    n_a = w_a.shape[1]
    w_q = jnp.pad(w_uq.reshape(MLA_Q_LORA, MLA_HEADS, MLA_NOPE + MLA_ROPE),
                  ((0, 0), (0, 0), (0, MLA_QK_PAD - MLA_NOPE - MLA_ROPE))).reshape(MLA_Q_LORA, -1).astype(BF16)
    w_kv3 = w_ukv.reshape(MLA_KV_LORA, MLA_HEADS, MLA_NOPE + MLA_V)
    w_k = jnp.pad(w_kv3[:, :, :MLA_NOPE], ((0, 0), (0, 0), (0, MLA_QK_PAD - MLA_NOPE))).reshape(MLA_KV_LORA, -1).astype(BF16)
    w_v = w_kv3[:, :, MLA_NOPE:].reshape(MLA_KV_LORA, -1).astype(BF16)

    xn = _rmsnorm(x, attn_norm, width=D_MODEL)
    u_a = _matmul_ws([xn], [[(w_a, 0)]], n_out=n_a, tm=1024, tn=n_a // 2, out_dtype=F32, epilogue=_epi_plain,
                     name="even_in_a")
    u_b = _matmul_ws([xn], [[(w_b, 0)]], n_out=n_ml_main, tm=1024, tn=512, out_dtype=BF16, epilogue=_epi_plain,
                     name="even_in_b")

    cqn = _rmsnorm(u_a, q_norm, width=MLA_Q_LORA, col_block=0)
    ckvn = _rmsnorm(u_a, kv_norm, width=MLA_KV_LORA, col_block=MLA_Q_LORA // MLA_KV_LORA)
    tm_p = min(1024, t)
    tab_specs = [(tab, (tm_p, LANES), lambda i, j: (i, 0)) for tab in (c_tab, s1_tab, s2_tab)]
    n_qk = MLA_HEADS * MLA_QK_PAD
    q_full = _matmul_ws([cqn], [[(w_q, 0)]], n_out=n_qk, tm=tm_p, tn=1024, out_dtype=BF16, epilogue=_epi_mla_q,
                        extras=tab_specs, name="mla_q")
    kr_spec = (u_a, (tm_p, LANES), lambda i, j: (i, (MLA_Q_LORA + MLA_KV_LORA) // LANES))
    k_full = _matmul_ws([ckvn], [[(w_k, 0)]], n_out=n_qk, tm=tm_p, tn=1024, out_dtype=BF16, epilogue=_epi_mla_k,
                        extras=[kr_spec] + tab_specs, name="mla_k")
    v = _matmul_ws([ckvn], [[(w_v, 0)]], n_out=MLA_HEADS * MLA_V, tm=tm_p, tn=1024, out_dtype=BF16,
                   epilogue=_epi_plain, name="mla_v")
    y_a = _mla_attention(q_full, k_full, v, batch, seq)

    bias_pad = jnp.concatenate([gate_bias.astype(F32), jnp.zeros((LANES - 2 * ML_HEADS,), F32)]).reshape(1, LANES)
    y_b = _mlstm(u_b, u_a, (MLA_IN + LANES - MLA_ROPE) // LANES, bias_pad, head_norm, batch, seq)
    return y_a, y_b


def _even_layer(x, tables, w_in, attn_norm, q_norm, w_uq, kv_norm, w_ukv, gate_bias, head_norm, w_out,
                ffn_norm, w_gate, w_up, w_down, batch, seq):
    y_a, y_b = _even_mixers(x, tables, w_in, attn_norm, q_norm, w_uq, kv_norm, w_ukv, gate_bias, head_norm,
                            batch, seq)
    tm = min(1024, x.shape[0])
    res_spec = lambda arr, tn: [(arr, (tm, tn), lambda i, j: (i, j))]
    x1 = _matmul_ws([y_a, y_b], [[(w_out, 0), (w_out, 1)]], n_out=D_MODEL, tm=tm, tn=512, out_dtype=F32,
                    epilogue=_epi_residual, extras=res_spec(x, 512), name="even_out")
    hn = _rmsnorm(x1, ffn_norm, width=D_MODEL)
    a = _matmul_ws([hn], [[(w_gate, 0)], [(w_up, 0)]], n_out=D_FF, tm=tm, tn=256, out_dtype=BF16,
                   epilogue=_epi_swiglu, name="ffn_gate_up")
    tm_d = min(2048, x.shape[0])
    x2 = _matmul([a], [[(w_down, 0)]], n_out=D_MODEL, tm=tm_d, tn=1024, tk=512, out_dtype=F32, epilogue=_epi_residual,
                 extras=[(x1, (tm_d, 1024), lambda i, j: (i, j))], vmem_mb=58, name="ffn_down")
    return x2


def _odd_layer(x, w_in, attn_norm, conv_w, w_out, ffn_norm, w_router, b_router, w_gate_e, w_up_e, w_down_e,
               final_norm, batch, seq):
    t = x.shape[0]
    xn = _rmsnorm(x, attn_norm, width=D_MODEL)
    n_in = w_in.shape[1]
    tm = min(1024, t)
    u = _matmul_ws([xn], [[(w_in, 0)]], n_out=n_in, tm=tm, tn=512, out_dtype=BF16, epilogue=_epi_plain,
                   name="odd_in")
    y_c = _short_conv(u, conv_w, batch, seq)
    y_d = _sb_attention(u, 3 * SC_WIDTH // LANES, batch, seq)
    x1 = _matmul_ws([y_c, y_d], [[(w_out, 0), (w_out, 1)]], n_out=D_MODEL, tm=tm, tn=512, out_dtype=F32,
                    epilogue=_epi_residual, extras=[(x, (tm, 512), lambda i, j: (i, j))], name="odd_out")
    return _moe(x1, ffn_norm, w_router, b_router, w_gate_e, w_up_e, w_down_e, final_norm)


def _moe(x1, ffn_norm, w_router, b_router, w_gate_e, w_up_e, w_down_e, final_norm):
    t = x1.shape[0]
    h, info, cnt = _route(x1, ffn_norm, w_router, b_router)
    tm = MOE_TM
    n_tiles = (t * TOP_K) // tm + N_EXPERTS
    counts = cnt[0, :N_EXPERTS].astype(jnp.int32)
    tiles_e = (counts + tm - 1) // tm
    offs = (jnp.cumsum(tiles_e) - tiles_e) * tm
    idx = info[:, 0:TOP_K].astype(jnp.int32)
    rank = info[:, TOP_K:2 * TOP_K].astype(jnp.int32)
    pos_flat = (offs[idx] + rank).reshape(-1)
    n_active = jnp.sum(tiles_e).astype(jnp.int32).reshape(1)

    inv = _invperm(pos_flat, n_tiles * tm)
    xs = _dispatch(h, inv, n_active, n_tiles, tm=tm)
    tn_gu, tn_d = 512, 1024
    tab_gu = _work_tables(counts, D_FF_EXPERT // tn_gu, tm, n_tiles)
    a_s = _gmm(xs, [w_gate_e, w_up_e], tab_gu, tm=tm, tn=tn_gu, out_dtype=BF16, epilogue=_epi_swiglu,
               vmem_mb=58, name="moe_gate_up")
    tab_d = _work_tables(counts, D_MODEL // tn_d, tm, n_tiles)
    y_s = _gmm(a_s, [w_down_e], tab_d, tm=tm, tn=tn_d, out_dtype=F32, epilogue=_epi_plain,
               vmem_mb=58, name="moe_down")
    return _combine(x1, info, pos_flat, y_s, final_norm)


def kernel(x, positions, even_attn_norm, even_w_in, even_q_norm, even_w_uq, even_kv_norm, even_w_ukv, even_ml_gate_bias, even_ml_head_norm, even_w_out, even_ffn_norm, even_w_gate, even_w_up, even_w_down, odd_attn_norm, odd_w_in, odd_conv_w, odd_w_out, odd_ffn_norm, odd_w_router, odd_b_router, odd_w_gate_e, odd_w_up_e, odd_w_down_e, final_norm):
    batch, seq, d = x.shape
    assert even_w_in.shape[0] == 1 and odd_w_in.shape[0] == 1, "kernel is written for one even and one odd layer"
    xf = x.reshape(batch * seq, d)
    tables = _rope_tables(positions)
    xf = _even_layer(xf, tables, even_w_in[0], even_attn_norm[0], even_q_norm[0], even_w_uq[0], even_kv_norm[0],
                     even_w_ukv[0], even_ml_gate_bias[0], even_ml_head_norm[0], even_w_out[0], even_ffn_norm[0],
                     even_w_gate[0], even_w_up[0], even_w_down[0], batch, seq)
    out = _odd_layer(xf, odd_w_in[0], odd_attn_norm[0], odd_conv_w[0], odd_w_out[0], odd_ffn_norm[0],
                     odd_w_router[0], odd_b_router[0], odd_w_gate_e[0], odd_w_up_e[0], odd_w_down_e[0],
                     final_norm, batch, seq)
    return out.reshape(batch, seq, d)
```

```python
import functools

import jax
import jax.numpy as jnp
from jax import lax
from jax.experimental import pallas as pl
from jax.experimental.pallas import tpu as pltpu

F32 = jnp.float32
BF16 = jnp.bfloat16

D_MODEL = 4096
NORM_EPS = 1e-6

MLA_HEADS = 16
MLA_Q_LORA = 1024
MLA_KV_LORA = 512
MLA_NOPE = 128
MLA_ROPE = 64
MLA_V = 128
ROPE_THETA = 10000.0
MLA_IN = MLA_Q_LORA + MLA_KV_LORA + MLA_ROPE
MLA_QK_PAD = 256

ML_HEADS = 4
ML_QK = 256
ML_V = 512
ML_CHUNK = 128
ML_VA = ML_V + 128

SC_WIDTH = 2048
SC_KERNEL = 3

SB_HEADS = 16
SB_HEAD_DIM = 128

D_FF = 11008
N_EXPERTS = 8
TOP_K = 2
D_FF_EXPERT = 4096

LANES = 128
MOE_TM = 512
MOE_SUB = 256
DMA_QUEUES = 2
LOG2_E = 1.4426950408889634
SCALAR_UNROLL = 8


def _cp(sem, vmem_mb):
    return pltpu.CompilerParams(dimension_semantics=sem, vmem_limit_bytes=vmem_mb * 1024 * 1024)


def _dot(a, b):
    return jnp.dot(a, b, preferred_element_type=F32)


def _dot_nt(a, b):
    return lax.dot_general(a, b, (((1,), (1,)), ((), ())), preferred_element_type=F32)


def _log_sigmoid(x):
    return jnp.minimum(x, 0.0) - jnp.log1p(jnp.exp(-jnp.abs(x)))


def _split_dot(tri, x, pieces):
    acc = None
    r = x
    for p in range(pieces):
        hi = r.astype(BF16)
        part = _dot(tri, hi)
        acc = part if acc is None else acc + part
        if p + 1 < pieces:
            r = r - hi.astype(F32)
    return acc


def _rmsnorm_body(x_ref, g_ref, o_ref):
    x = x_ref[...].astype(F32)
    y = x * lax.rsqrt(jnp.mean(x * x, axis=-1, keepdims=True) + NORM_EPS)
    o_ref[...] = (y * g_ref[...]).astype(o_ref.dtype)


def _rmsnorm(x, g, *, width, col_block=0, out_dtype=BF16, tm=256):
    m = x.shape[0]
    return pl.pallas_call(
        _rmsnorm_body,
        grid=(m // tm,),
        in_specs=[pl.BlockSpec((tm, width), lambda i: (i, col_block)),
                  pl.BlockSpec((1, width), lambda i: (0, 0))],
        out_specs=pl.BlockSpec((tm, width), lambda i: (i, 0)),
        out_shape=jax.ShapeDtypeStruct((m, width), out_dtype),
        compiler_params=_cp(("parallel",), 40),
        name="rmsnorm",
    )(x, g.reshape(1, width).astype(F32))


def _mm_body(*refs, n_lhs, n_acc, n_extra, nk, k_valid_last, epilogue):
    lhs = refs[:n_lhs]
    ws = refs[n_lhs:n_lhs + n_acc * n_lhs]
    extras = refs[n_lhs + n_acc * n_lhs:n_lhs + n_acc * n_lhs + n_extra]
    out = refs[n_lhs + n_acc * n_lhs + n_extra]
    accs = refs[n_lhs + n_acc * n_lhs + n_extra + 1:]
    k = pl.program_id(2)

    def partials(mask_tail):
        parts = []
        for a in range(n_acc):
            s = None
            for l in range(n_lhs):
                x = lhs[l][...]
                w = ws[a * n_lhs + l][...]
                if mask_tail:
                    xc = lax.broadcasted_iota(jnp.int32, x.shape, 1)
                    x = jnp.where(xc < k_valid_last, x, jnp.zeros_like(x))
                    wr = lax.broadcasted_iota(jnp.int32, w.shape, 0)
                    w = jnp.where(wr < k_valid_last, w, jnp.zeros_like(w))
                p = _dot(x.astype(BF16), w.astype(BF16))
                s = p if s is None else s + p
            parts.append(s)
        return parts

    if nk == 1:
        out[...] = epilogue(partials(False), extras).astype(out.dtype)
        return

    ragged = k_valid_last is not None

    @pl.when(k == 0)
    def _():
        for a, p in enumerate(partials(False)):
            accs[a][...] = p

    @pl.when((k > 0) & (k < nk - 1) if ragged else (k > 0))
    def _():
        for a, p in enumerate(partials(False)):
            accs[a][...] += p

    if ragged:
        @pl.when(k == nk - 1)
        def _():
            for a, p in enumerate(partials(True)):
                accs[a][...] += p

    @pl.when(k == nk - 1)
    def _():
        out[...] = epilogue([acc[...] for acc in accs], extras).astype(out.dtype)


def _matmul(lhs, weights, *, n_out, tm, tn, tk, out_dtype, epilogue, extras=(), vmem_mb=48, name="matmul"):
    m = lhs[0].shape[0]
    kdim = lhs[0].shape[1]
    tm = min(tm, m)
    nk = pl.cdiv(kdim, tk)
    k_valid_last = None if kdim % tk == 0 else kdim - (nk - 1) * tk
    n_lhs, n_acc = len(lhs), len(weights)
    grid = (m // tm, n_out // tn, nk)
    ij = lambda g0, g1: (g0, g1)

    in_specs, args = [], []
    for x in lhs:
        in_specs.append(pl.BlockSpec((tm, tk), lambda g0, g1, k: (ij(g0, g1)[0], k)))
        args.append(x)
    for wl in weights:
        for (w, rb) in wl:
            in_specs.append(pl.BlockSpec((tk, tn), lambda g0, g1, k, rb=rb: (rb * nk + k, ij(g0, g1)[1])))
            args.append(w)
    for (arr, bshape, imap) in extras:
        in_specs.append(pl.BlockSpec(bshape, lambda g0, g1, k, imap=imap: imap(*ij(g0, g1))))
        args.append(arr)
    scratch = [pltpu.VMEM((tm, tn), F32) for _ in range(n_acc)] if nk > 1 else []
    body = functools.partial(_mm_body, n_lhs=n_lhs, n_acc=n_acc, n_extra=len(extras), nk=nk,
                             k_valid_last=k_valid_last, epilogue=epilogue)
    return pl.pallas_call(
        body,
        grid=grid,
        in_specs=in_specs,
        out_specs=pl.BlockSpec((tm, tn), lambda g0, g1, k: ij(g0, g1)),
        out_shape=jax.ShapeDtypeStruct((m, n_out), out_dtype),
        scratch_shapes=scratch,
        compiler_params=_cp(("parallel", "parallel", "arbitrary"), vmem_mb),
        name=name,
    )(*args)


def _mm_ws_body(*refs, n_lhs, n_acc, n_extra, tn, kdims, row_blocks, col_offset, cast, epilogue):
    n_w = n_acc * n_lhs
    lhs = refs[:n_lhs]
    w_hbm = refs[n_lhs:n_lhs + n_w]
    extras = refs[n_lhs + n_w:n_lhs + n_w + n_extra]
    out = refs[n_lhs + n_w + n_extra]
    scratch = refs[n_lhs + n_w + n_extra + 1:]
    stage, sem, wbf = scratch[:n_w], scratch[n_w], scratch[n_w + 1:]
    j = pl.program_id(0)
    i = pl.program_id(1)
    nj = pl.num_programs(0)
    slot = j % 2

    def fetch(widx, jj, s):
        kd = kdims[widx % n_lhs]
        col = pl.multiple_of(col_offset + jj * tn, LANES)
        src = w_hbm[widx].at[pl.ds(row_blocks[widx] * kd, kd), pl.ds(col, tn)]
        return pltpu.make_async_copy(src, stage[widx].at[s], sem.at[widx, s])

    @pl.when(i == 0)
    def _():
        @pl.when(j == 0)
        def _():
            for widx in range(n_w):
                fetch(widx, 0, 0).start()

        @pl.when(j + 1 < nj)
        def _():
            for widx in range(n_w):
                fetch(widx, j + 1, 1 - slot).start()

        for widx in range(n_w):
            fetch(widx, j, slot).wait()
            if cast:
                wbf[widx][...] = stage[widx][slot].astype(BF16)

    parts = []
    for a in range(n_acc):
        s = None
        for l in range(n_lhs):
            widx = a * n_lhs + l
            w = wbf[widx][...] if cast else stage[widx][slot]
            p = _dot(lhs[l][...].astype(BF16), w)
            s = p if s is None else s + p
        parts.append(s)
    out[...] = epilogue(parts, extras).astype(out.dtype)


def _matmul_ws(lhs, weights, *, n_out, tm, tn, out_dtype, epilogue, extras=(), col_offset=0, vmem_mb=48,
               name="matmul_ws"):
    assert col_offset % LANES == 0 and tn % LANES == 0
    m = lhs[0].shape[0]
    tm = min(tm, m)
    n_lhs, n_acc = len(lhs), len(weights)
    kdims = tuple(x.shape[1] for x in lhs)
    flat_w = [w for wl in weights for (w, _) in wl]
    row_blocks = tuple(rb for wl in weights for (_, rb) in wl)
    wdtype = flat_w[0].dtype
    cast = wdtype != BF16
    in_specs = [pl.BlockSpec((tm, kd), lambda j, i: (i, 0)) for kd in kdims]
    in_specs += [pl.BlockSpec(memory_space=pl.ANY) for _ in flat_w]
    in_specs += [pl.BlockSpec(bshape, lambda j, i, imap=imap: imap(i, j)) for (_, bshape, imap) in extras]
    w_kdims = [kdims[widx % n_lhs] for widx in range(len(flat_w))]
    scratch = [pltpu.VMEM((2, kd, tn), wdtype) for kd in w_kdims]
    scratch.append(pltpu.SemaphoreType.DMA((len(flat_w), 2)))
    if cast:
        scratch += [pltpu.VMEM((kd, tn), BF16) for kd in w_kdims]
    body = functools.partial(_mm_ws_body, n_lhs=n_lhs, n_acc=n_acc, n_extra=len(extras), tn=tn, kdims=kdims,
                             row_blocks=row_blocks, col_offset=col_offset, cast=cast, epilogue=epilogue)
    return pl.pallas_call(
        body,
        grid=(n_out // tn, m // tm),
        in_specs=in_specs,
        out_specs=pl.BlockSpec((tm, tn), lambda j, i: (i, j)),
        out_shape=jax.ShapeDtypeStruct((m, n_out), out_dtype),
        scratch_shapes=scratch,
        compiler_params=_cp(("arbitrary", "arbitrary"), vmem_mb),
        name=name,
    )(*lhs, *flat_w, *[arr for (arr, _, _) in extras])


def _epi_plain(parts, extras):
    return parts[0]


def _epi_residual(parts, extras):
    return parts[0] + extras[0][...]


def _epi_swiglu(parts, extras):
    g, u = parts
    return g * jax.nn.sigmoid(g) * u


def _rope_tables_body(pos_ref, invf_ref, c_ref, s1_ref, s2_ref):
    ang = pos_ref[...] * invf_ref[...]
    lane = lax.broadcasted_iota(jnp.int32, ang.shape, 1)
    cos = jnp.cos(ang)
    sin = jnp.sin(ang)
    half = MLA_ROPE // 2
    c_ref[...] = jnp.where(lane < MLA_ROPE, cos, 0.0)
    s1_ref[...] = jnp.where(lane < half, -sin, 0.0)
    s2_ref[...] = jnp.where((lane >= half) & (lane < MLA_ROPE), sin, 0.0)


def _rope_tables(positions):
    t = positions.size
    half = MLA_ROPE // 2
    inv_freq = 1.0 / (ROPE_THETA ** (jnp.arange(0, MLA_ROPE, 2, dtype=F32) / MLA_ROPE))
    invf = jnp.concatenate([inv_freq, inv_freq, jnp.zeros((LANES - 2 * half,), F32)]).reshape(1, LANES)
    pos = jnp.broadcast_to(positions.reshape(t, 1).astype(F32), (t, LANES))
    tm = min(512, t)
    spec = pl.BlockSpec((tm, LANES), lambda i: (i, 0))
    sds = jax.ShapeDtypeStruct((t, LANES), F32)
    return pl.pallas_call(
        _rope_tables_body,
        grid=(t // tm,),
        in_specs=[spec, pl.BlockSpec((1, LANES), lambda i: (0, 0))],
        out_specs=[spec, spec, spec],
        out_shape=[sds, sds, sds],
        compiler_params=_cp(("parallel",), 32),
        name="rope_tables",
    )(pos, invf)


def _rope_lanes(t, c, s1, s2):
    return t * c + pltpu.roll(t, LANES - MLA_ROPE // 2, axis=1) * s1 + pltpu.roll(t, MLA_ROPE // 2, axis=1) * s2


def _epi_mla_q(parts, extras):
    acc = parts[0]
    c, s1, s2 = extras[0][...], extras[1][...], extras[2][...]
    scale = (MLA_NOPE + MLA_ROPE) ** -0.5
    pieces = []
    for h in range(acc.shape[1] // MLA_QK_PAD):
        base = h * MLA_QK_PAD
        pieces.append(acc[:, base:base + MLA_NOPE] * scale)
        pieces.append(_rope_lanes(acc[:, base + MLA_NOPE:base + MLA_QK_PAD], c, s1, s2) * scale)
    return jnp.concatenate(pieces, axis=1)


def _epi_mla_k(parts, extras):
    acc = parts[0]
    kr = _rope_lanes(extras[0][...], extras[1][...], extras[2][...], extras[3][...])
    pieces = []
    for h in range(acc.shape[1] // MLA_QK_PAD):
        base = h * MLA_QK_PAD
        pieces.append(acc[:, base:base + MLA_NOPE])
        pieces.append(acc[:, base + MLA_NOPE:base + MLA_QK_PAD] + kr)
    return jnp.concatenate(pieces, axis=1)


def _mla_attn_body(q_ref, k_ref, v_ref, o_ref, *, tq, seq):
    row = lax.broadcasted_iota(jnp.int32, (tq, tq), 0)
    col = lax.broadcasted_iota(jnp.int32, (tq, tq), 1)
    causal = col <= row
    for qi in range(seq // tq):
        n = (qi + 1) * tq
        q = q_ref[qi * tq:(qi + 1) * tq, :]
        s = _dot_nt(q, k_ref[0:n, :])
        diag = jnp.where(causal, s[:, n - tq:], -jnp.inf)
        s = diag if qi == 0 else jnp.concatenate([s[:, :n - tq], diag], axis=1)
        m = jnp.max(s, axis=1, keepdims=True)
        p = jnp.exp(s - m)
        l = jnp.sum(p, axis=1, keepdims=True)
        o = _dot(p.astype(BF16), v_ref[0:n, :]) / l
        o_ref[qi * tq:(qi + 1) * tq, :] = o.astype(o_ref.dtype)


def _mla_attention(q, k, v, batch, seq, *, tq=256):
    t = q.shape[0]
    tq = min(tq, seq)
    body = functools.partial(_mla_attn_body, tq=tq, seq=seq)
    return pl.pallas_call(
        body,
        grid=(batch, MLA_HEADS),
        in_specs=[pl.BlockSpec((seq, MLA_QK_PAD), lambda b, h: (b, h)),
                  pl.BlockSpec((seq, MLA_QK_PAD), lambda b, h: (b, h)),
                  pl.BlockSpec((seq, MLA_V), lambda b, h: (b, h))],
        out_specs=pl.BlockSpec((seq, MLA_V), lambda b, h: (b, h)),
        out_shape=jax.ShapeDtypeStruct((t, MLA_HEADS * MLA_V), BF16),
        compiler_params=_cp(("parallel", "parallel"), 48),
        name="mla_attention",
    )(q, k, v)


def _mlstm_body(q_ref, k_ref, v_ref, o_ref, g_ref, bias_ref, hn_ref, y_ref, ct_ref, m_ref):
    L = ML_CHUNK
    c = pl.program_id(1)

    @pl.when(c == 0)
    def _():
        ct_ref[...] = jnp.zeros(ct_ref.shape, F32)
        m_ref[...] = jnp.zeros(m_ref.shape, F32)

    g = g_ref[...] + bias_ref[...]
    logf = _log_sigmoid(g)
    row = lax.broadcasted_iota(jnp.int32, (L, L), 0)
    col = lax.broadcasted_iota(jnp.int32, (L, L), 1)
    causal = col <= row
    tri = jnp.where(causal, 1.0, 0.0).astype(BF16)
    cum = _split_dot(tri, logf, 3)
    lane = lax.broadcasted_iota(jnp.int32, (L, LANES), 1)
    colq = jnp.where(lane < ML_HEADS, g, cum)
    rowq = colq.T
    ones_col = jnp.where(lane == 0, 1.0, 0.0).astype(BF16)
    scale = ML_QK ** -0.5

    for h in range(ML_HEADS):
        i_c = colq[:, h:h + 1]
        cf_c = colq[:, ML_HEADS + h:ML_HEADS + h + 1]
        i_r = rowq[h:h + 1, :]
        cf_r = rowq[ML_HEADS + h:ML_HEADS + h + 1, :]
        m_prev = m_ref[h][:, 0:1]
        d = jnp.where(causal, cf_c - cf_r + i_r, -jnp.inf)
        m_inter = cf_c + m_prev
        m_t = jnp.maximum(m_inter, jnp.max(d, axis=1, keepdims=True))
        inter = jnp.exp(m_inter - m_t)
        p = jnp.exp(d - m_t)
        qh = q_ref[:, h * ML_QK:(h + 1) * ML_QK]
        kf = k_ref[:, h * ML_QK:(h + 1) * ML_QK].astype(F32) * scale
        kh = kf.astype(BF16)
        s = _dot_nt(qh, kh) * p
        va = jnp.concatenate([v_ref[:, h * ML_V:(h + 1) * ML_V], ones_col], axis=1)
        ct = ct_ref[h]
        na = _dot(s.astype(BF16), va) + inter * _dot(qh, ct.astype(BF16))
        num = na[:, :ML_V]
        den = na[:, ML_V:ML_V + 1]
        hh = num / jnp.maximum(jnp.abs(den), jnp.exp(-m_t))

        f_tot = cf_c[L - 1:L, :]
        w_log = f_tot - cf_c + i_c
        m_new = jnp.maximum(f_tot + m_prev, jnp.max(w_log, axis=0, keepdims=True))
        decay = jnp.exp(f_tot + m_prev - m_new)
        w = jnp.exp(w_log - m_new)
        wv = (va.astype(F32) * w).astype(BF16)
        ct_ref[h] = decay * ct + _dot(kf.T.astype(BF16), wv)
        m_ref[h] = jnp.broadcast_to(m_new, (1, LANES))

        hn = hh * lax.rsqrt(jnp.mean(hh * hh, axis=1, keepdims=True) + NORM_EPS) * hn_ref[:, h * ML_V:(h + 1) * ML_V]
        og = o_ref[:, h * ML_V:(h + 1) * ML_V].astype(F32)
        y_ref[:, h * ML_V:(h + 1) * ML_V] = (jax.nn.sigmoid(og) * hn).astype(y_ref.dtype)


def _mlstm(u_b, gates_src, gate_col_block, bias_pad, head_norm, batch, seq):
    t = u_b.shape[0]
    nc = seq // ML_CHUNK
    nqk = ML_HEADS * ML_QK
    nv = ML_HEADS * ML_V
    rows = lambda b, c: b * nc + c
    return pl.pallas_call(
        _mlstm_body,
        grid=(batch, nc),
        in_specs=[pl.BlockSpec((ML_CHUNK, nqk), lambda b, c: (rows(b, c), 0)),
                  pl.BlockSpec((ML_CHUNK, nqk), lambda b, c: (rows(b, c), 1)),
                  pl.BlockSpec((ML_CHUNK, nv), lambda b, c: (rows(b, c), 1)),
                  pl.BlockSpec((ML_CHUNK, nv), lambda b, c: (rows(b, c), 2)),
                  pl.BlockSpec((ML_CHUNK, LANES), lambda b, c: (rows(b, c), gate_col_block)),
                  pl.BlockSpec((1, LANES), lambda b, c: (0, 0)),
                  pl.BlockSpec((1, nv), lambda b, c: (0, 0))],
        out_specs=pl.BlockSpec((ML_CHUNK, nv), lambda b, c: (rows(b, c), 0)),
        out_shape=jax.ShapeDtypeStruct((t, nv), BF16),
        scratch_shapes=[pltpu.VMEM((ML_HEADS, ML_QK, ML_VA), F32), pltpu.VMEM((ML_HEADS, 1, LANES), F32)],
        compiler_params=_cp(("parallel", "arbitrary"), 32),
        name="mlstm",
    )(u_b, u_b, u_b, u_b, gates_src, bias_pad, head_norm.reshape(1, nv).astype(F32))


def _conv_body(b_ref, c_ref, h_ref, cp_ref, hp_ref, w_ref, y_ref, z_sc, *, ts, halo):
    i = pl.program_id(1)
    z = c_ref[...].astype(F32) * h_ref[...].astype(F32)
    zp = cp_ref[...].astype(F32) * hp_ref[...].astype(F32)
    z_sc[0:halo, :] = jnp.where(i > 0, zp, 0.0)
    z_sc[halo:halo + ts, :] = z
    w = w_ref[...]
    y = w[2:3, :] * z + w[1:2, :] * z_sc[halo - 1:halo - 1 + ts, :] + w[0:1, :] * z_sc[halo - 2:halo - 2 + ts, :]
    y_ref[...] = (b_ref[...].astype(F32) * y).astype(y_ref.dtype)


def _short_conv(u, conv_w, batch, seq, *, ts=256, tw=1024):
    t = u.shape[0]
    halo = 16
    ns = seq // ts
    nw = SC_WIDTH // tw
    wpad = jnp.concatenate([conv_w.astype(F32), jnp.zeros((8 - SC_KERNEL, SC_WIDTH), F32)], axis=0)
    rows = lambda b, i: b * ns + i
    prev = lambda b, i: jnp.maximum((b * seq + i * ts) // halo - 1, 0)
    body = functools.partial(_conv_body, ts=ts, halo=halo)
    return pl.pallas_call(
        body,
        grid=(batch, ns, nw),
        in_specs=[pl.BlockSpec((ts, tw), lambda b, i, j: (rows(b, i), j)),
                  pl.BlockSpec((ts, tw), lambda b, i, j: (rows(b, i), nw + j)),
                  pl.BlockSpec((ts, tw), lambda b, i, j: (rows(b, i), 2 * nw + j)),
                  pl.BlockSpec((halo, tw), lambda b, i, j: (prev(b, i), nw + j)),
                  pl.BlockSpec((halo, tw), lambda b, i, j: (prev(b, i), 2 * nw + j)),
                  pl.BlockSpec((8, tw), lambda b, i, j: (0, j))],
        out_specs=pl.BlockSpec((ts, tw), lambda b, i, j: (rows(b, i), j)),
        out_shape=jax.ShapeDtypeStruct((t, SC_WIDTH), BF16),
        scratch_shapes=[pltpu.VMEM((halo + ts, tw), F32)],
        compiler_params=_cp(("parallel", "parallel", "parallel"), 32),
        name="short_conv",
    )(u, u, u, u, u, wpad)


def _sb_body(q_ref, k_ref, v_ref, o_ref, *, tq, seq):
    scale = SB_HEAD_DIM ** -0.5 * LOG2_E
    row = lax.broadcasted_iota(jnp.int32, (tq, tq), 0)
    col = lax.broadcasted_iota(jnp.int32, (tq, tq), 1)
    strict = col < row
    neg_upper = jnp.where(row > col, -1.0, 0.0).astype(BF16)
    for qi in range(seq // tq):
        nb = qi + 1
        q = q_ref[qi * tq:(qi + 1) * tq, :]
        z = _dot_nt(q, k_ref[0:nb * tq, :]) * scale
        sp = jnp.maximum(z, 0.0) + jnp.log2(1.0 + jnp.exp2(-jnp.abs(z)))
        later = jnp.zeros((tq, 1), F32)
        blocks = [None] * nb
        for j in range(nb - 1, -1, -1):
            zj = z[:, j * tq:(j + 1) * tq]
            spj = sp[:, j * tq:(j + 1) * tq]
            spm = jnp.where(strict, spj, 0.0) if j == nb - 1 else spj
            between = _split_dot_rhs(spm, neg_upper, 2)
            e = jnp.exp2(zj - spj + between + later)
            if j == nb - 1:
                e = jnp.where(strict, e, 0.0)
            blocks[j] = e.astype(BF16)
            later = later + (between[:, 0:1] - spm[:, 0:1])
        a = blocks[0] if nb == 1 else jnp.concatenate(blocks, axis=1)
        o_ref[qi * tq:(qi + 1) * tq, :] = _dot(a, v_ref[0:nb * tq, :]).astype(o_ref.dtype)


def _split_dot_rhs(x, tri, pieces):
    acc = None
    r = x
    for p in range(pieces):
        hi = r.astype(BF16)
        part = _dot(hi, tri)
        acc = part if acc is None else acc + part
        if p + 1 < pieces:
            r = r - hi.astype(F32)
    return acc


def _sb_attention(u, col_base, batch, seq, *, tq=256):
    t = u.shape[0]
    tq = min(tq, seq)
    body = functools.partial(_sb_body, tq=tq, seq=seq)
    spec = lambda off: pl.BlockSpec((seq, SB_HEAD_DIM), lambda b, h: (b, col_base + off + h))
    return pl.pallas_call(
        body,
        grid=(batch, SB_HEADS),
        in_specs=[spec(0), spec(SB_HEADS), spec(2 * SB_HEADS)],
        out_specs=pl.BlockSpec((seq, SB_HEAD_DIM), lambda b, h: (b, h)),
        out_shape=jax.ShapeDtypeStruct((t, SB_HEADS * SB_HEAD_DIM), BF16),
        compiler_params=_cp(("parallel", "parallel"), 48),
        name="sb_attention",
    )(u, u, u)


def _route_body(x_ref, g_ref, wr_ref, br_ref, h_ref, info_ref, cnt_ref, carry):
    i = pl.program_id(0)

    @pl.when(i == 0)
    def _():
        carry[...] = jnp.zeros(carry.shape, F32)

    x = x_ref[...]
    h = x * lax.rsqrt(jnp.mean(x * x, axis=-1, keepdims=True) + NORM_EPS) * g_ref[...]
    h_ref[...] = h
    wr = wr_ref[...]
    h_hi = h.astype(BF16)
    h_lo = (h - h_hi.astype(F32)).astype(BF16)
    w_hi = wr.astype(BF16)
    w_lo = (wr - w_hi.astype(F32)).astype(BF16)
    logits = (_dot(h_hi, w_hi) + _dot(h_hi, w_lo)) + (_dot(h_lo, w_hi) + _dot(h_lo, w_lo)) + br_ref[...]
    tm = logits.shape[0]
    lane = lax.broadcasted_iota(jnp.int32, logits.shape, 1).astype(F32)
    lg = jnp.where(lane < N_EXPERTS, logits, -jnp.inf)
    m1 = jnp.max(lg, axis=1, keepdims=True)
    i1 = jnp.min(jnp.where(lg == m1, lane, float(LANES)), axis=1, keepdims=True)
    lg2 = jnp.where(lane == i1, -jnp.inf, lg)
    m2 = jnp.max(lg2, axis=1, keepdims=True)
    i2 = jnp.min(jnp.where(lg2 == m2, lane, float(LANES)), axis=1, keepdims=True)
    e = jnp.exp(m2 - m1)
    w1 = 1.0 / (1.0 + e)
    w2 = e / (1.0 + e)
    oh1 = lane == i1
    oh2 = lane == i2
    mask = jnp.where(oh1 | oh2, 1.0, 0.0)
    row = lax.broadcasted_iota(jnp.int32, (tm, tm), 0)
    col = lax.broadcasted_iota(jnp.int32, (tm, tm), 1)
    before = jnp.where(col < row, 1.0, 0.0).astype(BF16)
    rank_mat = _dot(before, mask.astype(BF16)) + carry[...]
    r1 = jnp.sum(jnp.where(oh1, rank_mat, 0.0), axis=1, keepdims=True)
    r2 = jnp.sum(jnp.where(oh2, rank_mat, 0.0), axis=1, keepdims=True)
    carry[...] += jnp.sum(mask, axis=0, keepdims=True)
    info = jnp.where(lane == 0, i1,
           jnp.where(lane == 1, i2,
           jnp.where(lane == 2, r1,
           jnp.where(lane == 3, r2,
           jnp.where(lane == 4, w1,
           jnp.where(lane == 5, w2, 0.0))))))
    info_ref[...] = info
    cnt_ref[...] = jnp.broadcast_to(carry[...], cnt_ref.shape)


def _route(x, g, w_router, b_router, *, tm=256):
    t, d = x.shape
    wr = jnp.concatenate([w_router.astype(F32), jnp.zeros((d, LANES - N_EXPERTS), F32)], axis=1)
    br = jnp.concatenate([b_router.astype(F32), jnp.zeros((LANES - N_EXPERTS,), F32)]).reshape(1, LANES)
    return pl.pallas_call(
        _route_body,
        grid=(t // tm,),
        in_specs=[pl.BlockSpec((tm, d), lambda i: (i, 0)),
                  pl.BlockSpec((1, d), lambda i: (0, 0)),
                  pl.BlockSpec((d, LANES), lambda i: (0, 0)),
                  pl.BlockSpec((1, LANES), lambda i: (0, 0))],
        out_specs=[pl.BlockSpec((tm, d), lambda i: (i, 0)),
                   pl.BlockSpec((tm, LANES), lambda i: (i, 0)),
                   pl.BlockSpec((8, LANES), lambda i: (0, 0))],
        out_shape=[jax.ShapeDtypeStruct((t, d), F32),
                   jax.ShapeDtypeStruct((t, LANES), F32),
                   jax.ShapeDtypeStruct((8, LANES), F32)],
        scratch_shapes=[pltpu.VMEM((1, LANES), F32)],
        compiler_params=_cp(("arbitrary",), 40),
        name="moe_route",
    )(x, g.reshape(1, d).astype(F32), wr, br)


def _invperm_body(pos_ref, inv_ref, *, n_assign, n_slots):
    def clear(p, c):
        inv_ref[p] = 0
        return c

    lax.fori_loop(0, n_slots, clear, 0, unroll=SCALAR_UNROLL)

    def put(a, c):
        inv_ref[pos_ref[a]] = lax.shift_right_logical(a, TOP_K.bit_length() - 1)
        return c

    lax.fori_loop(0, n_assign, put, 0, unroll=SCALAR_UNROLL)


def _invperm(pos_flat, n_slots):
    n_assign = pos_flat.shape[0]
    return pl.pallas_call(
        functools.partial(_invperm_body, n_assign=n_assign, n_slots=n_slots),
        in_specs=[pl.BlockSpec(memory_space=pltpu.SMEM)],
        out_specs=pl.BlockSpec(memory_space=pltpu.SMEM),
        out_shape=jax.ShapeDtypeStruct((n_slots,), jnp.int32),
        name="moe_invperm",
    )(pos_flat)


def _row_copy(src_hbm, row, dst_vmem, slot, sem):
    return pltpu.make_async_copy(src_hbm.at[pl.ds(row, 1), :], dst_vmem.at[pl.ds(slot, 1), :], sem)


def _wait_rows(src_hbm, dst_vmem, sem, n):
    def wait(r, c):
        _row_copy(src_hbm, 0, dst_vmem, r, sem).wait()
        return c

    lax.fori_loop(0, n, wait, 0, unroll=SCALAR_UNROLL)


def _dispatch_body(inv_ref, nact_ref, h_hbm, o_ref, buf, sem, *, tm):
    i = pl.program_id(0)
    nact = nact_ref[0]

    def gather(tile, slot):
        def issue(rr, c):
            for u in range(DMA_QUEUES):
                r = rr * DMA_QUEUES + u
                _row_copy(h_hbm, inv_ref[tile * tm + r], buf.at[slot], r, sem.at[slot]).start(priority=u)
            return c

        lax.fori_loop(0, tm // DMA_QUEUES, issue, 0, unroll=SCALAR_UNROLL // DMA_QUEUES)

    @pl.when((i == 0) & (nact > 0))
    def _():
        gather(0, 0)

    for slot in range(2):
        @pl.when((i + 1 < nact) & ((i + 1) % 2 == slot))
        def _():
            gather(i + 1, slot)

    for slot in range(2):
        @pl.when((i < nact) & (i % 2 == slot))
        def _():
            _wait_rows(h_hbm, buf.at[slot], sem.at[slot], tm)
            o_ref[...] = buf[slot].astype(o_ref.dtype)

    @pl.when(i >= nact)
    def _():
        o_ref[...] = jnp.zeros(o_ref.shape, o_ref.dtype)


def _dispatch(h, inv, n_active_tiles, n_tiles, *, tm):
    d = h.shape[1]
    grid_spec = pltpu.PrefetchScalarGridSpec(
        num_scalar_prefetch=2,
        grid=(n_tiles,),
        in_specs=[pl.BlockSpec(memory_space=pl.ANY)],
        out_specs=pl.BlockSpec((tm, d), lambda i, inv, na: (i, 0)),
        scratch_shapes=[pltpu.VMEM((2, tm, d), F32), pltpu.SemaphoreType.DMA((2,))],
    )
    return pl.pallas_call(
        functools.partial(_dispatch_body, tm=tm),
        grid_spec=grid_spec,
        out_shape=jax.ShapeDtypeStruct((n_tiles * tm, d), BF16),
        compiler_params=_cp(("arbitrary",), 40),
        name="moe_dispatch",
    )(inv, n_active_tiles, h)


def _gmm_body(rt_ref, e_ref, jo_ref, jw_ref, first_ref, act_ref, slot_ref, ne_ref, nj_ref, hn_ref, nv_ref, x_ref,
              *refs, n_acc, tn, epilogue):
    w_hbm = refs[:n_acc]
    out = refs[n_acc]
    scratch = refs[n_acc + 1:]
    stage, sem, wbf = scratch[:n_acc], scratch[n_acc], scratch[n_acc + 1:]
    w = pl.program_id(0)

    def fetch(a, expert, col_tile, s):
        col = pl.multiple_of(col_tile * tn, tn)
        return pltpu.make_async_copy(w_hbm[a].at[expert, :, pl.ds(col, tn)], stage[a].at[s], sem.at[a, s])

    @pl.when(first_ref[w] == 1)
    def _():
        s = slot_ref[w]

        @pl.when(w == 0)
        def _():
            for a in range(n_acc):
                fetch(a, e_ref[0], jw_ref[0], 0).start()

        @pl.when(hn_ref[w] == 1)
        def _():
            for a in range(n_acc):
                fetch(a, ne_ref[w], nj_ref[w], 1 - s).start()

        for a in range(n_acc):
            fetch(a, e_ref[w], jw_ref[w], s).wait()
            wbf[a][...] = stage[a][s].astype(BF16)

    tm = x_ref.shape[0]
    for nsub in range(1, tm // MOE_SUB + 1):
        @pl.when((act_ref[w] == 1) & (nv_ref[w] == nsub))
        def _():
            rows = nsub * MOE_SUB
            x = x_ref[0:rows, :].astype(BF16)
            out[0:rows, :] = epilogue([_dot(x, wbf[a][...]) for a in range(n_acc)], ()).astype(out.dtype)
            if rows < tm:
                out[rows:, :] = jnp.zeros((tm - rows, out.shape[1]), out.dtype)

    @pl.when(act_ref[w] == 0)
    def _():
        out[...] = jnp.zeros(out.shape, out.dtype)


def _gmm(xs, weights, tables, *, tm, tn, out_dtype, epilogue, vmem_mb, name):
    p, kdim = xs.shape
    n_out = weights[0].shape[2]
    n_items = tables[0].shape[0]
    n_acc = len(weights)
    grid_spec = pltpu.PrefetchScalarGridSpec(
        num_scalar_prefetch=len(tables),
        grid=(n_items,),
        in_specs=[pl.BlockSpec((tm, kdim), lambda w, rt, *_: (rt[w], 0))]
        + [pl.BlockSpec(memory_space=pl.ANY) for _ in weights],
        out_specs=pl.BlockSpec((tm, tn), lambda w, rt, e, jo, *_: (rt[w], jo[w])),
        scratch_shapes=[pltpu.VMEM((2, kdim, tn), weights[0].dtype) for _ in weights]
        + [pltpu.SemaphoreType.DMA((n_acc, 2))]
        + [pltpu.VMEM((kdim, tn), BF16) for _ in weights],
    )
    return pl.pallas_call(
        functools.partial(_gmm_body, n_acc=n_acc, tn=tn, epilogue=epilogue),
        grid_spec=grid_spec,
        out_shape=jax.ShapeDtypeStruct((p, n_out), out_dtype),
        compiler_params=_cp(("arbitrary",), vmem_mb),
        name=name,
    )(*tables, xs, *weights)


def _work_tables(counts, n_col_tiles, tm, n_tiles):
    tiles_e = (counts + tm - 1) // tm
    tile_end = jnp.cumsum(tiles_e)
    tile_start = tile_end - tiles_e
    total_tiles = tile_end[-1]
    n_items = n_col_tiles * n_tiles
    item_end = n_col_tiles * tile_end
    w = jnp.arange(n_items, dtype=jnp.int32)
    n_active = n_col_tiles * total_tiles
    active = w < n_active
    wc = jnp.minimum(w, n_active - 1)
    e = jnp.sum((wc[:, None] >= item_end[None, :]).astype(jnp.int32), axis=1)
    e = jnp.minimum(e, N_EXPERTS - 1)
    te = jnp.maximum(tiles_e[e], 1)
    local = wc - n_col_tiles * tile_start[e]
    j = local // te
    q = local % te
    rt = tile_start[e] + q
    first = (q == 0) & active
    slot = (jnp.cumsum(first.astype(jnp.int32)) - 1) % 2
    nxt = lax.cummin(jnp.where(first, w, n_items), axis=0, reverse=True)
    nxt = jnp.concatenate([nxt[1:], jnp.full((1,), n_items, nxt.dtype)])
    has_next = nxt < n_items
    nxt = jnp.minimum(nxt, n_items - 1)
    spare = jnp.maximum(w - n_active, 0)
    rt = jnp.where(active, rt, total_tiles + spare // n_col_tiles)
    j_out = jnp.where(active, j, spare % n_col_tiles)
    rows_valid = jnp.clip(counts[e] - q * tm, 1, tm)
    n_sub = (rows_valid + MOE_SUB - 1) // MOE_SUB
    i32 = lambda a: a.astype(jnp.int32)
    return (i32(rt), i32(e), i32(j_out), i32(j), i32(first), i32(active), i32(slot), i32(e[nxt]), i32(j[nxt]),
            i32(has_next), i32(n_sub))


def _combine_body(pos_ref, x_ref, info_ref, g_ref, y_hbm, o_ref, buf, sem, *, tm):
    i = pl.program_id(0)
    n = pl.num_programs(0)

    def gather(tile, slot):
        def issue(r, c):
            for k in range(TOP_K):
                _row_copy(y_hbm, pos_ref[(tile * tm + r) * TOP_K + k], buf.at[slot, k], r,
                          sem.at[slot]).start(priority=k % DMA_QUEUES)
            return c

        lax.fori_loop(0, tm, issue, 0, unroll=SCALAR_UNROLL)

    @pl.when(i == 0)
    def _():
        gather(0, 0)

    for slot in range(2):
        @pl.when((i + 1 < n) & ((i + 1) % 2 == slot))
        def _():
            gather(i + 1, slot)

    for slot in range(2):
        @pl.when(i % 2 == slot)
        def _():
            for k in range(TOP_K):
                _wait_rows(y_hbm, buf.at[slot, k], sem.at[slot], tm)
            info = info_ref[...]
            acc = x_ref[...] + info[:, 4:5] * buf[slot, 0] + info[:, 5:6] * buf[slot, 1]
            y = acc * lax.rsqrt(jnp.mean(acc * acc, axis=-1, keepdims=True) + NORM_EPS)
            o_ref[...] = y * g_ref[...]


def _combine(x, info, pos_flat, y_sorted, final_norm, *, tm=128):
    t, d = x.shape
    grid_spec = pltpu.PrefetchScalarGridSpec(
        num_scalar_prefetch=1,
        grid=(t // tm,),
        in_specs=[pl.BlockSpec((tm, d), lambda i, pos: (i, 0)),
                  pl.BlockSpec((tm, LANES), lambda i, pos: (i, 0)),
                  pl.BlockSpec((1, d), lambda i, pos: (0, 0)),
                  pl.BlockSpec(memory_space=pl.ANY)],
        out_specs=pl.BlockSpec((tm, d), lambda i, pos: (i, 0)),
        scratch_shapes=[pltpu.VMEM((2, TOP_K, tm, d), F32), pltpu.SemaphoreType.DMA((2,))],
    )
    return pl.pallas_call(
        functools.partial(_combine_body, tm=tm),
        grid_spec=grid_spec,
        out_shape=jax.ShapeDtypeStruct((t, d), F32),
        compiler_params=_cp(("arbitrary",), 40),
        name="moe_combine",
    )(pos_flat, x, info, final_norm.reshape(1, d).astype(F32), y_sorted)


def _even_mixers(x, tables, w_in, attn_norm, q_norm, w_uq, kv_norm, w_ukv, gate_bias, head_norm, batch, seq):
    t = x.shape[0]
    c_tab, s1_tab, s2_tab = tables
    n_ml_main = 2 * ML_HEADS * ML_QK + 2 * ML_HEADS * ML_V
    gate_lo = MLA_IN + n_ml_main
    w_cat = jnp.concatenate(
        [w_in[:, :MLA_IN], jnp.zeros((D_MODEL, LANES - MLA_ROPE), w_in.dtype),
         w_in[:, gate_lo:], jnp.zeros((D_MODEL, LANES - 2 * ML_HEADS), w_in.dtype),
         w_in[:, MLA_IN:gate_lo]], axis=1).astype(BF16)
    n_a = MLA_IN + LANES - MLA_ROPE + LANES
    w_q = jnp.pad(w_uq.reshape(MLA_Q_LORA, MLA_HEADS, MLA_NOPE + MLA_ROPE),
                  ((0, 0), (0, 0), (0, MLA_QK_PAD - MLA_NOPE - MLA_ROPE))).reshape(MLA_Q_LORA, -1).astype(BF16)
    w_kv3 = w_ukv.reshape(MLA_KV_LORA, MLA_HEADS, MLA_NOPE + MLA_V)
    w_k = jnp.pad(w_kv3[:, :, :MLA_NOPE], ((0, 0), (0, 0), (0, MLA_QK_PAD - MLA_NOPE))).reshape(MLA_KV_LORA, -1).astype(BF16)
    w_v = w_kv3[:, :, MLA_NOPE:].reshape(MLA_KV_LORA, -1).astype(BF16)

    xn = _rmsnorm(x, attn_norm, width=D_MODEL)
    u_a = _matmul_ws([xn], [[(w_cat, 0)]], n_out=n_a, tm=1024, tn=n_a // 2, out_dtype=F32, epilogue=_epi_plain,
                     name="even_in_a")
    u_b = _matmul_ws([xn], [[(w_cat, 0)]], n_out=n_ml_main, tm=1024, tn=512, out_dtype=BF16, epilogue=_epi_plain,
                     col_offset=n_a, name="even_in_b")

    cqn = _rmsnorm(u_a, q_norm, width=MLA_Q_LORA, col_block=0)
    ckvn = _rmsnorm(u_a, kv_norm, width=MLA_KV_LORA, col_block=MLA_Q_LORA // MLA_KV_LORA)
    tm_p = min(1024, t)
    tab_specs = [(tab, (tm_p, LANES), lambda i, j: (i, 0)) for tab in (c_tab, s1_tab, s2_tab)]
    n_qk = MLA_HEADS * MLA_QK_PAD
    q_full = _matmul_ws([cqn], [[(w_q, 0)]], n_out=n_qk, tm=tm_p, tn=1024, out_dtype=BF16, epilogue=_epi_mla_q,
                        extras=tab_specs, name="mla_q")
    kr_spec = (u_a, (tm_p, LANES), lambda i, j: (i, (MLA_Q_LORA + MLA_KV_LORA) // LANES))
    k_full = _matmul_ws([ckvn], [[(w_k, 0)]], n_out=n_qk, tm=tm_p, tn=1024, out_dtype=BF16, epilogue=_epi_mla_k,
                        extras=[kr_spec] + tab_specs, name="mla_k")
    v = _matmul_ws([ckvn], [[(w_v, 0)]], n_out=MLA_HEADS * MLA_V, tm=tm_p, tn=1024, out_dtype=BF16,
                   epilogue=_epi_plain, name="mla_v")
    y_a = _mla_attention(q_full, k_full, v, batch, seq)

    bias_pad = jnp.concatenate([gate_bias.astype(F32), jnp.zeros((LANES - 2 * ML_HEADS,), F32)]).reshape(1, LANES)
    y_b = _mlstm(u_b, u_a, (MLA_IN + LANES - MLA_ROPE) // LANES, bias_pad, head_norm, batch, seq)
    return y_a, y_b


def _even_layer(x, tables, w_in, attn_norm, q_norm, w_uq, kv_norm, w_ukv, gate_bias, head_norm, w_out,
                ffn_norm, w_gate, w_up, w_down, batch, seq):
    y_a, y_b = _even_mixers(x, tables, w_in, attn_norm, q_norm, w_uq, kv_norm, w_ukv, gate_bias, head_norm,
                            batch, seq)
    tm = min(1024, x.shape[0])
    res_spec = lambda arr, tn: [(arr, (tm, tn), lambda i, j: (i, j))]
    x1 = _matmul_ws([y_a, y_b], [[(w_out, 0), (w_out, 1)]], n_out=D_MODEL, tm=tm, tn=512, out_dtype=F32,
                    epilogue=_epi_residual, extras=res_spec(x, 512), name="even_out")
    hn = _rmsnorm(x1, ffn_norm, width=D_MODEL)
    a = _matmul_ws([hn], [[(w_gate, 0)], [(w_up, 0)]], n_out=D_FF, tm=min(2048, x.shape[0]), tn=256, out_dtype=BF16,
                   epilogue=_epi_swiglu, vmem_mb=58, name="ffn_gate_up")
    tm_d = min(2048, x.shape[0])
    x2 = _matmul([a], [[(w_down, 0)]], n_out=D_MODEL, tm=tm_d, tn=1024, tk=512, out_dtype=F32, epilogue=_epi_residual,
                 extras=[(x1, (tm_d, 1024), lambda i, j: (i, j))], vmem_mb=58, name="ffn_down")
    return x2


def _odd_layer(x, w_in, attn_norm, conv_w, w_out, ffn_norm, w_router, b_router, w_gate_e, w_up_e, w_down_e,
               final_norm, batch, seq):
    t = x.shape[0]
    xn = _rmsnorm(x, attn_norm, width=D_MODEL)
    n_in = w_in.shape[1]
    tm = min(1024, t)
    u = _matmul_ws([xn], [[(w_in, 0)]], n_out=n_in, tm=tm, tn=512, out_dtype=BF16, epilogue=_epi_plain,
                   name="odd_in")
    y_c = _short_conv(u, conv_w, batch, seq)
    y_d = _sb_attention(u, 3 * SC_WIDTH // LANES, batch, seq)
    x1 = _matmul_ws([y_c, y_d], [[(w_out, 0), (w_out, 1)]], n_out=D_MODEL, tm=tm, tn=512, out_dtype=F32,
                    epilogue=_epi_residual, extras=[(x, (tm, 512), lambda i, j: (i, j))], name="odd_out")
    return _moe(x1, ffn_norm, w_router, b_router, w_gate_e, w_up_e, w_down_e, final_norm)


def _moe(x1, ffn_norm, w_router, b_router, w_gate_e, w_up_e, w_down_e, final_norm):
    t = x1.shape[0]
    h, info, cnt = _route(x1, ffn_norm, w_router, b_router)
    tm = MOE_TM
    n_tiles = (t * TOP_K) // tm + N_EXPERTS
    counts = cnt[0, :N_EXPERTS].astype(jnp.int32)
    tiles_e = (counts + tm - 1) // tm
    offs = (jnp.cumsum(tiles_e) - tiles_e) * tm
    idx = info[:, 0:TOP_K].astype(jnp.int32)
    rank = info[:, TOP_K:2 * TOP_K].astype(jnp.int32)
    pos_flat = (offs[idx] + rank).reshape(-1)
    n_active = jnp.sum(tiles_e).astype(jnp.int32).reshape(1)

    inv = _invperm(pos_flat, n_tiles * tm)
    xs = _dispatch(h, inv, n_active, n_tiles, tm=tm)
    tn_gu, tn_d = 512, 1024
    tab_gu = _work_tables(counts, D_FF_EXPERT // tn_gu, tm, n_tiles)
    a_s = _gmm(xs, [w_gate_e, w_up_e], tab_gu, tm=tm, tn=tn_gu, out_dtype=BF16, epilogue=_epi_swiglu,
               vmem_mb=58, name="moe_gate_up")
    tab_d = _work_tables(counts, D_MODEL // tn_d, tm, n_tiles)
    y_s = _gmm(a_s, [w_down_e], tab_d, tm=tm, tn=tn_d, out_dtype=F32, epilogue=_epi_plain,
               vmem_mb=58, name="moe_down")
    return _combine(x1, info, pos_flat, y_s, final_norm)


def kernel(x, positions, even_attn_norm, even_w_in, even_q_norm, even_w_uq, even_kv_norm, even_w_ukv, even_ml_gate_bias, even_ml_head_norm, even_w_out, even_ffn_norm, even_w_gate, even_w_up, even_w_down, odd_attn_norm, odd_w_in, odd_conv_w, odd_w_out, odd_ffn_norm, odd_w_router, odd_b_router, odd_w_gate_e, odd_w_up_e, odd_w_down_e, final_norm):
    batch, seq, d = x.shape
    assert even_w_in.shape[0] == 1 and odd_w_in.shape[0] == 1, "kernel is written for one even and one odd layer"
    xf = x.reshape(batch * seq, d)
    tables = _rope_tables(positions)
    xf = _even_layer(xf, tables, even_w_in[0], even_attn_norm[0], even_q_norm[0], even_w_uq[0], even_kv_norm[0],
                     even_w_ukv[0], even_ml_gate_bias[0], even_ml_head_norm[0], even_w_out[0], even_ffn_norm[0],
                     even_w_gate[0], even_w_up[0], even_w_down[0], batch, seq)
    out = _odd_layer(xf, odd_w_in[0], odd_attn_norm[0], odd_conv_w[0], odd_w_out[0], odd_ffn_norm[0],
                     odd_w_router[0], odd_b_router[0], odd_w_gate_e[0], odd_w_up_e[0], odd_w_down_e[0],
                     final_norm, batch, seq)
    return out.reshape(batch, seq, d)
```

```python
import functools

import jax
import jax.numpy as jnp
from jax import lax
from jax.experimental import pallas as pl
from jax.experimental.pallas import tpu as pltpu

F32 = jnp.float32
BF16 = jnp.bfloat16

D_MODEL = 4096
NORM_EPS = 1e-6

MLA_HEADS = 16
MLA_Q_LORA = 1024
MLA_KV_LORA = 512
MLA_NOPE = 128
MLA_ROPE = 64
MLA_V = 128
ROPE_THETA = 10000.0
MLA_IN = MLA_Q_LORA + MLA_KV_LORA + MLA_ROPE
MLA_QK_PAD = 256

ML_HEADS = 4
ML_QK = 256
ML_V = 512
ML_CHUNK = 128
ML_VA = ML_V + 128

SC_WIDTH = 2048
SC_KERNEL = 3

SB_HEADS = 16
SB_HEAD_DIM = 128

D_FF = 11008
N_EXPERTS = 8
TOP_K = 2
D_FF_EXPERT = 4096

LANES = 128
MOE_TM = 512
MOE_SUB = 256
DMA_QUEUES = 2
LOG2_E = 1.4426950408889634
SCALAR_UNROLL = 8


def _cp(sem, vmem_mb):
    return pltpu.CompilerParams(dimension_semantics=sem, vmem_limit_bytes=vmem_mb * 1024 * 1024)


def _dot(a, b):
    return jnp.dot(a, b, preferred_element_type=F32)


def _dot_nt(a, b):
    return lax.dot_general(a, b, (((1,), (1,)), ((), ())), preferred_element_type=F32)


def _log_sigmoid(x):
    return jnp.minimum(x, 0.0) - jnp.log1p(jnp.exp(-jnp.abs(x)))


def _split_dot(tri, x, pieces):
    acc = None
    r = x
    for p in range(pieces):
        hi = r.astype(BF16)
        part = _dot(tri, hi)
        acc = part if acc is None else acc + part
        if p + 1 < pieces:
            r = r - hi.astype(F32)
    return acc


def _rmsnorm_body(x_ref, g_ref, o_ref):
    x = x_ref[...].astype(F32)
    y = x * lax.rsqrt(jnp.mean(x * x, axis=-1, keepdims=True) + NORM_EPS)
    o_ref[...] = (y * g_ref[...]).astype(o_ref.dtype)


def _rmsnorm(x, g, *, width, col_block=0, out_dtype=BF16, tm=256):
    m = x.shape[0]
    return pl.pallas_call(
        _rmsnorm_body,
        grid=(m // tm,),
        in_specs=[pl.BlockSpec((tm, width), lambda i: (i, col_block)),
                  pl.BlockSpec((1, width), lambda i: (0, 0))],
        out_specs=pl.BlockSpec((tm, width), lambda i: (i, 0)),
        out_shape=jax.ShapeDtypeStruct((m, width), out_dtype),
        compiler_params=_cp(("parallel",), 40),
        name="rmsnorm",
    )(x, g.reshape(1, width).astype(F32))


def _mm_body(*refs, n_lhs, n_acc, n_extra, nk, k_valid_last, epilogue):
    lhs = refs[:n_lhs]
    ws = refs[n_lhs:n_lhs + n_acc * n_lhs]
    extras = refs[n_lhs + n_acc * n_lhs:n_lhs + n_acc * n_lhs + n_extra]
    out = refs[n_lhs + n_acc * n_lhs + n_extra]
    accs = refs[n_lhs + n_acc * n_lhs + n_extra + 1:]
    k = pl.program_id(2)

    def partials(mask_tail):
        parts = []
        for a in range(n_acc):
            s = None
            for l in range(n_lhs):
                x = lhs[l][...]
                w = ws[a * n_lhs + l][...]
                if mask_tail:
                    xc = lax.broadcasted_iota(jnp.int32, x.shape, 1)
                    x = jnp.where(xc < k_valid_last, x, jnp.zeros_like(x))
                    wr = lax.broadcasted_iota(jnp.int32, w.shape, 0)
                    w = jnp.where(wr < k_valid_last, w, jnp.zeros_like(w))
                p = _dot(x.astype(BF16), w.astype(BF16))
                s = p if s is None else s + p
            parts.append(s)
        return parts

    if nk == 1:
        out[...] = epilogue(partials(False), extras).astype(out.dtype)
        return

    ragged = k_valid_last is not None

    @pl.when(k == 0)
    def _():
        for a, p in enumerate(partials(False)):
            accs[a][...] = p

    @pl.when((k > 0) & (k < nk - 1) if ragged else (k > 0))
    def _():
        for a, p in enumerate(partials(False)):
            accs[a][...] += p

    if ragged:
        @pl.when(k == nk - 1)
        def _():
            for a, p in enumerate(partials(True)):
                accs[a][...] += p

    @pl.when(k == nk - 1)
    def _():
        out[...] = epilogue([acc[...] for acc in accs], extras).astype(out.dtype)


def _matmul(lhs, weights, *, n_out, tm, tn, tk, out_dtype, epilogue, extras=(), vmem_mb=48, name="matmul"):
    m = lhs[0].shape[0]
    kdim = lhs[0].shape[1]
    tm = min(tm, m)
    nk = pl.cdiv(kdim, tk)
    k_valid_last = None if kdim % tk == 0 else kdim - (nk - 1) * tk
    n_lhs, n_acc = len(lhs), len(weights)
    grid = (m // tm, n_out // tn, nk)
    ij = lambda g0, g1: (g0, g1)

    in_specs, args = [], []
    for x in lhs:
        in_specs.append(pl.BlockSpec((tm, tk), lambda g0, g1, k: (ij(g0, g1)[0], k)))
        args.append(x)
    for wl in weights:
        for (w, rb) in wl:
            in_specs.append(pl.BlockSpec((tk, tn), lambda g0, g1, k, rb=rb: (rb * nk + k, ij(g0, g1)[1])))
            args.append(w)
    for (arr, bshape, imap) in extras:
        in_specs.append(pl.BlockSpec(bshape, lambda g0, g1, k, imap=imap: imap(*ij(g0, g1))))
        args.append(arr)
    scratch = [pltpu.VMEM((tm, tn), F32) for _ in range(n_acc)] if nk > 1 else []
    body = functools.partial(_mm_body, n_lhs=n_lhs, n_acc=n_acc, n_extra=len(extras), nk=nk,
                             k_valid_last=k_valid_last, epilogue=epilogue)
    return pl.pallas_call(
        body,
        grid=grid,
        in_specs=in_specs,
        out_specs=pl.BlockSpec((tm, tn), lambda g0, g1, k: ij(g0, g1)),
        out_shape=jax.ShapeDtypeStruct((m, n_out), out_dtype),
        scratch_shapes=scratch,
        compiler_params=_cp(("parallel", "parallel", "arbitrary"), vmem_mb),
        name=name,
    )(*args)


def _mm_ws_body(*refs, n_lhs, n_acc, n_extra, tn, kdims, row_blocks, col_offset, cast, epilogue):
    n_w = n_acc * n_lhs
    lhs = refs[:n_lhs]
    w_hbm = refs[n_lhs:n_lhs + n_w]
    extras = refs[n_lhs + n_w:n_lhs + n_w + n_extra]
    out = refs[n_lhs + n_w + n_extra]
    scratch = refs[n_lhs + n_w + n_extra + 1:]
    stage, sem, wbf = scratch[:n_w], scratch[n_w], scratch[n_w + 1:]
    j = pl.program_id(0)
    i = pl.program_id(1)
    nj = pl.num_programs(0)
    slot = j % 2

    def fetch(widx, jj, s):
        kd = kdims[widx % n_lhs]
        col = pl.multiple_of(col_offset + jj * tn, LANES)
        src = w_hbm[widx].at[pl.ds(row_blocks[widx] * kd, kd), pl.ds(col, tn)]
        return pltpu.make_async_copy(src, stage[widx].at[s], sem.at[widx, s])

    @pl.when(i == 0)
    def _():
        @pl.when(j == 0)
        def _():
            for widx in range(n_w):
                fetch(widx, 0, 0).start()

        @pl.when(j + 1 < nj)
        def _():
            for widx in range(n_w):
                fetch(widx, j + 1, 1 - slot).start()

        for widx in range(n_w):
            fetch(widx, j, slot).wait()
            if cast:
                wbf[widx][...] = stage[widx][slot].astype(BF16)

    parts = []
    for a in range(n_acc):
        s = None
        for l in range(n_lhs):
            widx = a * n_lhs + l
            w = wbf[widx][...] if cast else stage[widx][slot]
            p = _dot(lhs[l][...].astype(BF16), w)
            s = p if s is None else s + p
        parts.append(s)
    out[...] = epilogue(parts, extras).astype(out.dtype)


def _matmul_ws(lhs, weights, *, n_out, tm, tn, out_dtype, epilogue, extras=(), col_offset=0, vmem_mb=48,
               name="matmul_ws"):
    assert col_offset % LANES == 0 and tn % LANES == 0
    m = lhs[0].shape[0]
    tm = min(tm, m)
    n_lhs, n_acc = len(lhs), len(weights)
    kdims = tuple(x.shape[1] for x in lhs)
    flat_w = [w for wl in weights for (w, _) in wl]
    row_blocks = tuple(rb for wl in weights for (_, rb) in wl)
    wdtype = flat_w[0].dtype
    cast = wdtype != BF16
    in_specs = [pl.BlockSpec((tm, kd), lambda j, i: (i, 0)) for kd in kdims]
    in_specs += [pl.BlockSpec(memory_space=pl.ANY) for _ in flat_w]
    in_specs += [pl.BlockSpec(bshape, lambda j, i, imap=imap: imap(i, j)) for (_, bshape, imap) in extras]
    w_kdims = [kdims[widx % n_lhs] for widx in range(len(flat_w))]
    scratch = [pltpu.VMEM((2, kd, tn), wdtype) for kd in w_kdims]
    scratch.append(pltpu.SemaphoreType.DMA((len(flat_w), 2)))
    if cast:
        scratch += [pltpu.VMEM((kd, tn), BF16) for kd in w_kdims]
    body = functools.partial(_mm_ws_body, n_lhs=n_lhs, n_acc=n_acc, n_extra=len(extras), tn=tn, kdims=kdims,
                             row_blocks=row_blocks, col_offset=col_offset, cast=cast, epilogue=epilogue)
    return pl.pallas_call(
        body,
        grid=(n_out // tn, m // tm),
        in_specs=in_specs,
        out_specs=pl.BlockSpec((tm, tn), lambda j, i: (i, j)),
        out_shape=jax.ShapeDtypeStruct((m, n_out), out_dtype),
        scratch_shapes=scratch,
        compiler_params=_cp(("arbitrary", "arbitrary"), vmem_mb),
        name=name,
    )(*lhs, *flat_w, *[arr for (arr, _, _) in extras])


def _epi_plain(parts, extras):
    return parts[0]


def _epi_residual(parts, extras):
    return parts[0] + extras[0][...]


def _epi_swiglu(parts, extras):
    g, u = parts
    return g * jax.nn.sigmoid(g) * u


def _rope_tables_body(pos_ref, invf_ref, c_ref, s1_ref, s2_ref):
    ang = pos_ref[...] * invf_ref[...]
    lane = lax.broadcasted_iota(jnp.int32, ang.shape, 1)
    cos = jnp.cos(ang)
    sin = jnp.sin(ang)
    half = MLA_ROPE // 2
    c_ref[...] = jnp.where(lane < MLA_ROPE, cos, 0.0)
    s1_ref[...] = jnp.where(lane < half, -sin, 0.0)
    s2_ref[...] = jnp.where((lane >= half) & (lane < MLA_ROPE), sin, 0.0)


def _rope_tables(positions):
    t = positions.size
    half = MLA_ROPE // 2
    inv_freq = 1.0 / (ROPE_THETA ** (jnp.arange(0, MLA_ROPE, 2, dtype=F32) / MLA_ROPE))
    invf = jnp.concatenate([inv_freq, inv_freq, jnp.zeros((LANES - 2 * half,), F32)]).reshape(1, LANES)
    pos = jnp.broadcast_to(positions.reshape(t, 1).astype(F32), (t, LANES))
    tm = min(512, t)
    spec = pl.BlockSpec((tm, LANES), lambda i: (i, 0))
    sds = jax.ShapeDtypeStruct((t, LANES), F32)
    return pl.pallas_call(
        _rope_tables_body,
        grid=(t // tm,),
        in_specs=[spec, pl.BlockSpec((1, LANES), lambda i: (0, 0))],
        out_specs=[spec, spec, spec],
        out_shape=[sds, sds, sds],
        compiler_params=_cp(("parallel",), 32),
        name="rope_tables",
    )(pos, invf)


def _rope_lanes(t, c, s1, s2):
    return t * c + pltpu.roll(t, LANES - MLA_ROPE // 2, axis=1) * s1 + pltpu.roll(t, MLA_ROPE // 2, axis=1) * s2


def _epi_mla_q(parts, extras):
    acc = parts[0]
    c, s1, s2 = extras[0][...], extras[1][...], extras[2][...]
    scale = (MLA_NOPE + MLA_ROPE) ** -0.5
    pieces = []
    for h in range(acc.shape[1] // MLA_QK_PAD):
        base = h * MLA_QK_PAD
        pieces.append(acc[:, base:base + MLA_NOPE] * scale)
        pieces.append(_rope_lanes(acc[:, base + MLA_NOPE:base + MLA_QK_PAD], c, s1, s2) * scale)
    return jnp.concatenate(pieces, axis=1)


def _epi_mla_k(parts, extras):
    acc = parts[0]
    kr = _rope_lanes(extras[0][...], extras[1][...], extras[2][...], extras[3][...])
    pieces = []
    for h in range(acc.shape[1] // MLA_QK_PAD):
        base = h * MLA_QK_PAD
        pieces.append(acc[:, base:base + MLA_NOPE])
        pieces.append(acc[:, base + MLA_NOPE:base + MLA_QK_PAD] + kr)
    return jnp.concatenate(pieces, axis=1)


def _mla_attn_body(q_ref, k_ref, v_ref, o_ref, *, tq, seq):
    row = lax.broadcasted_iota(jnp.int32, (tq, tq), 0)
    col = lax.broadcasted_iota(jnp.int32, (tq, tq), 1)
    causal = col <= row
    for qi in range(seq // tq):
        n = (qi + 1) * tq
        q = q_ref[qi * tq:(qi + 1) * tq, :]
        s = _dot_nt(q, k_ref[0:n, :])
        diag = jnp.where(causal, s[:, n - tq:], -jnp.inf)
        s = diag if qi == 0 else jnp.concatenate([s[:, :n - tq], diag], axis=1)
        m = jnp.max(s, axis=1, keepdims=True)
        p = jnp.exp(s - m)
        l = jnp.sum(p, axis=1, keepdims=True)
        o = _dot(p.astype(BF16), v_ref[0:n, :]) / l
        o_ref[qi * tq:(qi + 1) * tq, :] = o.astype(o_ref.dtype)


def _mla_attention(q, k, v, batch, seq, *, tq=256):
    t = q.shape[0]
    tq = min(tq, seq)
    body = functools.partial(_mla_attn_body, tq=tq, seq=seq)
    return pl.pallas_call(
        body,
        grid=(batch, MLA_HEADS),
        in_specs=[pl.BlockSpec((seq, MLA_QK_PAD), lambda b, h: (b, h)),
                  pl.BlockSpec((seq, MLA_QK_PAD), lambda b, h: (b, h)),
                  pl.BlockSpec((seq, MLA_V), lambda b, h: (b, h))],
        out_specs=pl.BlockSpec((seq, MLA_V), lambda b, h: (b, h)),
        out_shape=jax.ShapeDtypeStruct((t, MLA_HEADS * MLA_V), BF16),
        compiler_params=_cp(("parallel", "parallel"), 48),
        name="mla_attention",
    )(q, k, v)


def _mlstm_body(q_ref, k_ref, v_ref, o_ref, g_ref, bias_ref, hn_ref, y_ref, ct_ref, m_ref):
    L = ML_CHUNK
    c = pl.program_id(1)

    @pl.when(c == 0)
    def _():
        ct_ref[...] = jnp.zeros(ct_ref.shape, F32)
        m_ref[...] = jnp.zeros(m_ref.shape, F32)

    g = g_ref[...] + bias_ref[...]
    logf = _log_sigmoid(g)
    row = lax.broadcasted_iota(jnp.int32, (L, L), 0)
    col = lax.broadcasted_iota(jnp.int32, (L, L), 1)
    causal = col <= row
    tri = jnp.where(causal, 1.0, 0.0).astype(BF16)
    cum = _split_dot(tri, logf, 3)
    lane = lax.broadcasted_iota(jnp.int32, (L, LANES), 1)
    colq = jnp.where(lane < ML_HEADS, g, cum)
    rowq = colq.T
    ones_col = jnp.where(lane == 0, 1.0, 0.0).astype(BF16)
    scale = ML_QK ** -0.5

    for h in range(ML_HEADS):
        i_c = colq[:, h:h + 1]
        cf_c = colq[:, ML_HEADS + h:ML_HEADS + h + 1]
        i_r = rowq[h:h + 1, :]
        cf_r = rowq[ML_HEADS + h:ML_HEADS + h + 1, :]
        m_prev = m_ref[h][:, 0:1]
        d = jnp.where(causal, cf_c - cf_r + i_r, -jnp.inf)
        m_inter = cf_c + m_prev
        m_t = jnp.maximum(m_inter, jnp.max(d, axis=1, keepdims=True))
        inter = jnp.exp(m_inter - m_t)
        p = jnp.exp(d - m_t)
        qh = q_ref[:, h * ML_QK:(h + 1) * ML_QK]
        kf = k_ref[:, h * ML_QK:(h + 1) * ML_QK].astype(F32) * scale
        kh = kf.astype(BF16)
        s = _dot_nt(qh, kh) * p
        va = jnp.concatenate([v_ref[:, h * ML_V:(h + 1) * ML_V], ones_col], axis=1)
        ct = ct_ref[h]
        na = _dot(s.astype(BF16), va) + inter * _dot(qh, ct.astype(BF16))
        num = na[:, :ML_V]
        den = na[:, ML_V:ML_V + 1]
        hh = num / jnp.maximum(jnp.abs(den), jnp.exp(-m_t))

        f_tot = cf_c[L - 1:L, :]
        w_log = f_tot - cf_c + i_c
        m_new = jnp.maximum(f_tot + m_prev, jnp.max(w_log, axis=0, keepdims=True))
        decay = jnp.exp(f_tot + m_prev - m_new)
        w = jnp.exp(w_log - m_new)
        wv = (va.astype(F32) * w).astype(BF16)
        ct_ref[h] = decay * ct + _dot(kf.T.astype(BF16), wv)
        m_ref[h] = jnp.broadcast_to(m_new, (1, LANES))

        hn = hh * lax.rsqrt(jnp.mean(hh * hh, axis=1, keepdims=True) + NORM_EPS) * hn_ref[:, h * ML_V:(h + 1) * ML_V]
        og = o_ref[:, h * ML_V:(h + 1) * ML_V].astype(F32)
        y_ref[:, h * ML_V:(h + 1) * ML_V] = (jax.nn.sigmoid(og) * hn).astype(y_ref.dtype)


def _mlstm(u_b, gates_src, gate_col_block, bias_pad, head_norm, batch, seq):
    t = u_b.shape[0]
    nc = seq // ML_CHUNK
    nqk = ML_HEADS * ML_QK
    nv = ML_HEADS * ML_V
    rows = lambda b, c: b * nc + c
    return pl.pallas_call(
        _mlstm_body,
        grid=(batch, nc),
        in_specs=[pl.BlockSpec((ML_CHUNK, nqk), lambda b, c: (rows(b, c), 0)),
                  pl.BlockSpec((ML_CHUNK, nqk), lambda b, c: (rows(b, c), 1)),
                  pl.BlockSpec((ML_CHUNK, nv), lambda b, c: (rows(b, c), 1)),
                  pl.BlockSpec((ML_CHUNK, nv), lambda b, c: (rows(b, c), 2)),
                  pl.BlockSpec((ML_CHUNK, LANES), lambda b, c: (rows(b, c), gate_col_block)),
                  pl.BlockSpec((1, LANES), lambda b, c: (0, 0)),
                  pl.BlockSpec((1, nv), lambda b, c: (0, 0))],
        out_specs=pl.BlockSpec((ML_CHUNK, nv), lambda b, c: (rows(b, c), 0)),
        out_shape=jax.ShapeDtypeStruct((t, nv), BF16),
        scratch_shapes=[pltpu.VMEM((ML_HEADS, ML_QK, ML_VA), F32), pltpu.VMEM((ML_HEADS, 1, LANES), F32)],
        compiler_params=_cp(("parallel", "arbitrary"), 32),
        name="mlstm",
    )(u_b, u_b, u_b, u_b, gates_src, bias_pad, head_norm.reshape(1, nv).astype(F32))


def _conv_body(b_ref, c_ref, h_ref, cp_ref, hp_ref, w_ref, y_ref, z_sc, *, ts, halo):
    i = pl.program_id(1)
    z = c_ref[...].astype(F32) * h_ref[...].astype(F32)
    zp = cp_ref[...].astype(F32) * hp_ref[...].astype(F32)
    z_sc[0:halo, :] = jnp.where(i > 0, zp, 0.0)
    z_sc[halo:halo + ts, :] = z
    w = w_ref[...]
    y = w[2:3, :] * z + w[1:2, :] * z_sc[halo - 1:halo - 1 + ts, :] + w[0:1, :] * z_sc[halo - 2:halo - 2 + ts, :]
    y_ref[...] = (b_ref[...].astype(F32) * y).astype(y_ref.dtype)


def _short_conv(u, conv_w, batch, seq, *, ts=256, tw=1024):
    t = u.shape[0]
    halo = 16
    ns = seq // ts
    nw = SC_WIDTH // tw
    wpad = jnp.concatenate([conv_w.astype(F32), jnp.zeros((8 - SC_KERNEL, SC_WIDTH), F32)], axis=0)
    rows = lambda b, i: b * ns + i
    prev = lambda b, i: jnp.maximum((b * seq + i * ts) // halo - 1, 0)
    body = functools.partial(_conv_body, ts=ts, halo=halo)
    return pl.pallas_call(
        body,
        grid=(batch, ns, nw),
        in_specs=[pl.BlockSpec((ts, tw), lambda b, i, j: (rows(b, i), j)),
                  pl.BlockSpec((ts, tw), lambda b, i, j: (rows(b, i), nw + j)),
                  pl.BlockSpec((ts, tw), lambda b, i, j: (rows(b, i), 2 * nw + j)),
                  pl.BlockSpec((halo, tw), lambda b, i, j: (prev(b, i), nw + j)),
                  pl.BlockSpec((halo, tw), lambda b, i, j: (prev(b, i), 2 * nw + j)),
                  pl.BlockSpec((8, tw), lambda b, i, j: (0, j))],
        out_specs=pl.BlockSpec((ts, tw), lambda b, i, j: (rows(b, i), j)),
        out_shape=jax.ShapeDtypeStruct((t, SC_WIDTH), BF16),
        scratch_shapes=[pltpu.VMEM((halo + ts, tw), F32)],
        compiler_params=_cp(("parallel", "parallel", "parallel"), 32),
        name="short_conv",
    )(u, u, u, u, u, wpad)


def _sb_body(q_ref, k_ref, v_ref, o_ref, *, tq, seq):
    scale = SB_HEAD_DIM ** -0.5 * LOG2_E
    row = lax.broadcasted_iota(jnp.int32, (tq, tq), 0)
    col = lax.broadcasted_iota(jnp.int32, (tq, tq), 1)
    strict = col < row
    neg_upper_incl = jnp.where(row >= col, -1.0, 0.0).astype(BF16)
    for qi in range(seq // tq):
        nb = qi + 1
        q = q_ref[qi * tq:(qi + 1) * tq, :]
        z = _dot_nt(q, k_ref[0:nb * tq, :]) * scale
        sp = jnp.maximum(z, 0.0) + jnp.log2(1.0 + jnp.exp2(-jnp.abs(z)))
        later = jnp.zeros((tq, 1), F32)
        blocks = [None] * nb
        for j in range(nb - 1, -1, -1):
            zj = z[:, j * tq:(j + 1) * tq]
            spj = sp[:, j * tq:(j + 1) * tq]
            spm = jnp.where(strict, spj, 0.0) if j == nb - 1 else spj
            upto = _split_dot_rhs(spm, neg_upper_incl, 2)
            e = jnp.exp2(zj + upto + later)
            if j == nb - 1:
                e = jnp.where(strict, e, 0.0)
            blocks[j] = e.astype(BF16)
            later = later + upto[:, 0:1]
        a = blocks[0] if nb == 1 else jnp.concatenate(blocks, axis=1)
        o_ref[qi * tq:(qi + 1) * tq, :] = _dot(a, v_ref[0:nb * tq, :]).astype(o_ref.dtype)


def _split_dot_rhs(x, tri, pieces):
    acc = None
    r = x
    for p in range(pieces):
        hi = r.astype(BF16)
        part = _dot(hi, tri)
        acc = part if acc is None else acc + part
        if p + 1 < pieces:
            r = r - hi.astype(F32)
    return acc


def _sb_attention(u, col_base, batch, seq, *, tq=256):
    t = u.shape[0]
    tq = min(tq, seq)
    body = functools.partial(_sb_body, tq=tq, seq=seq)
    spec = lambda off: pl.BlockSpec((seq, SB_HEAD_DIM), lambda b, h: (b, col_base + off + h))
    return pl.pallas_call(
        body,
        grid=(batch, SB_HEADS),
        in_specs=[spec(0), spec(SB_HEADS), spec(2 * SB_HEADS)],
        out_specs=pl.BlockSpec((seq, SB_HEAD_DIM), lambda b, h: (b, h)),
        out_shape=jax.ShapeDtypeStruct((t, SB_HEADS * SB_HEAD_DIM), BF16),
        compiler_params=_cp(("parallel", "parallel"), 48),
        name="sb_attention",
    )(u, u, u)


def _route_body(x_ref, g_ref, wr_ref, br_ref, h_ref, info_ref, cnt_ref, carry):
    i = pl.program_id(0)

    @pl.when(i == 0)
    def _():
        carry[...] = jnp.zeros(carry.shape, F32)

    x = x_ref[...]
    h = x * lax.rsqrt(jnp.mean(x * x, axis=-1, keepdims=True) + NORM_EPS) * g_ref[...]
    h_ref[...] = h
    wr = wr_ref[...]
    h_hi = h.astype(BF16)
    h_lo = (h - h_hi.astype(F32)).astype(BF16)
    w_hi = wr.astype(BF16)
    w_lo = (wr - w_hi.astype(F32)).astype(BF16)
    logits = (_dot(h_hi, w_hi) + _dot(h_hi, w_lo)) + (_dot(h_lo, w_hi) + _dot(h_lo, w_lo)) + br_ref[...]
    tm = logits.shape[0]
    lane = lax.broadcasted_iota(jnp.int32, logits.shape, 1).astype(F32)
    lg = jnp.where(lane < N_EXPERTS, logits, -jnp.inf)
    m1 = jnp.max(lg, axis=1, keepdims=True)
    i1 = jnp.min(jnp.where(lg == m1, lane, float(LANES)), axis=1, keepdims=True)
    lg2 = jnp.where(lane == i1, -jnp.inf, lg)
    m2 = jnp.max(lg2, axis=1, keepdims=True)
    i2 = jnp.min(jnp.where(lg2 == m2, lane, float(LANES)), axis=1, keepdims=True)
    e = jnp.exp(m2 - m1)
    w1 = 1.0 / (1.0 + e)
    w2 = e / (1.0 + e)
    oh1 = lane == i1
    oh2 = lane == i2
    mask = jnp.where(oh1 | oh2, 1.0, 0.0)
    row = lax.broadcasted_iota(jnp.int32, (tm, tm), 0)
    col = lax.broadcasted_iota(jnp.int32, (tm, tm), 1)
    before = jnp.where(col < row, 1.0, 0.0).astype(BF16)
    rank_mat = _dot(before, mask.astype(BF16)) + carry[...]
    r1 = jnp.sum(jnp.where(oh1, rank_mat, 0.0), axis=1, keepdims=True)
    r2 = jnp.sum(jnp.where(oh2, rank_mat, 0.0), axis=1, keepdims=True)
    carry[...] += jnp.sum(mask, axis=0, keepdims=True)
    info = jnp.where(lane == 0, i1,
           jnp.where(lane == 1, i2,
           jnp.where(lane == 2, r1,
           jnp.where(lane == 3, r2,
           jnp.where(lane == 4, w1,
           jnp.where(lane == 5, w2, 0.0))))))
    info_ref[...] = info
    cnt_ref[...] = jnp.broadcast_to(carry[...], cnt_ref.shape)


def _route(x, g, w_router, b_router, *, tm=256):
    t, d = x.shape
    wr = jnp.concatenate([w_router.astype(F32), jnp.zeros((d, LANES - N_EXPERTS), F32)], axis=1)
    br = jnp.concatenate([b_router.astype(F32), jnp.zeros((LANES - N_EXPERTS,), F32)]).reshape(1, LANES)
    return pl.pallas_call(
        _route_body,
        grid=(t // tm,),
        in_specs=[pl.BlockSpec((tm, d), lambda i: (i, 0)),
                  pl.BlockSpec((1, d), lambda i: (0, 0)),
                  pl.BlockSpec((d, LANES), lambda i: (0, 0)),
                  pl.BlockSpec((1, LANES), lambda i: (0, 0))],
        out_specs=[pl.BlockSpec((tm, d), lambda i: (i, 0)),
                   pl.BlockSpec((tm, LANES), lambda i: (i, 0)),
                   pl.BlockSpec((8, LANES), lambda i: (0, 0))],
        out_shape=[jax.ShapeDtypeStruct((t, d), F32),
                   jax.ShapeDtypeStruct((t, LANES), F32),
                   jax.ShapeDtypeStruct((8, LANES), F32)],
        scratch_shapes=[pltpu.VMEM((1, LANES), F32)],
        compiler_params=_cp(("arbitrary",), 40),
        name="moe_route",
    )(x, g.reshape(1, d).astype(F32), wr, br)


def _invperm_body(pos_ref, inv_ref, *, n_assign, n_slots):
    def clear(p, c):
        inv_ref[p] = 0
        return c

    lax.fori_loop(0, n_slots, clear, 0, unroll=SCALAR_UNROLL)

    def put(a, c):
        inv_ref[pos_ref[a]] = lax.shift_right_logical(a, TOP_K.bit_length() - 1)
        return c

    lax.fori_loop(0, n_assign, put, 0, unroll=SCALAR_UNROLL)


def _invperm(pos_flat, n_slots):
    n_assign = pos_flat.shape[0]
    return pl.pallas_call(
        functools.partial(_invperm_body, n_assign=n_assign, n_slots=n_slots),
        in_specs=[pl.BlockSpec(memory_space=pltpu.SMEM)],
        out_specs=pl.BlockSpec(memory_space=pltpu.SMEM),
        out_shape=jax.ShapeDtypeStruct((n_slots,), jnp.int32),
        name="moe_invperm",
    )(pos_flat)


def _row_copy(src_hbm, row, dst_vmem, slot, sem):
    return pltpu.make_async_copy(src_hbm.at[pl.ds(row, 1), :], dst_vmem.at[pl.ds(slot, 1), :], sem)


def _wait_rows(src_hbm, dst_vmem, sem, n):
    def wait(r, c):
        _row_copy(src_hbm, 0, dst_vmem, r, sem).wait()
        return c

    lax.fori_loop(0, n, wait, 0, unroll=SCALAR_UNROLL)


def _dispatch_body(inv_ref, nact_ref, h_hbm, o_ref, buf, sem, *, tm):
    i = pl.program_id(0)
    nact = nact_ref[0]

    def gather(tile, slot):
        def issue(rr, c):
            for u in range(DMA_QUEUES):
                r = rr * DMA_QUEUES + u
                _row_copy(h_hbm, inv_ref[tile * tm + r], buf.at[slot], r, sem.at[slot]).start(priority=u)
            return c

        lax.fori_loop(0, tm // DMA_QUEUES, issue, 0, unroll=SCALAR_UNROLL // DMA_QUEUES)

    @pl.when((i == 0) & (nact > 0))
    def _():
        gather(0, 0)

    for slot in range(2):
        @pl.when((i + 1 < nact) & ((i + 1) % 2 == slot))
        def _():
            gather(i + 1, slot)

    for slot in range(2):
        @pl.when((i < nact) & (i % 2 == slot))
        def _():
            _wait_rows(h_hbm, buf.at[slot], sem.at[slot], tm)
            o_ref[...] = buf[slot].astype(o_ref.dtype)

    @pl.when(i >= nact)
    def _():
        o_ref[...] = jnp.zeros(o_ref.shape, o_ref.dtype)


def _dispatch(h, inv, n_active_tiles, n_tiles, *, tm):
    d = h.shape[1]
    grid_spec = pltpu.PrefetchScalarGridSpec(
        num_scalar_prefetch=2,
        grid=(n_tiles,),
        in_specs=[pl.BlockSpec(memory_space=pl.ANY)],
        out_specs=pl.BlockSpec((tm, d), lambda i, inv, na: (i, 0)),
        scratch_shapes=[pltpu.VMEM((2, tm, d), F32), pltpu.SemaphoreType.DMA((2,))],
    )
    return pl.pallas_call(
        functools.partial(_dispatch_body, tm=tm),
        grid_spec=grid_spec,
        out_shape=jax.ShapeDtypeStruct((n_tiles * tm, d), BF16),
        compiler_params=_cp(("arbitrary",), 40),
        name="moe_dispatch",
    )(inv, n_active_tiles, h)


def _gmm_body(rt_ref, e_ref, jo_ref, jw_ref, first_ref, act_ref, slot_ref, ne_ref, nj_ref, hn_ref, nv_ref, x_ref,
              *refs, n_acc, tn, epilogue):
    w_hbm = refs[:n_acc]
    out = refs[n_acc]
    scratch = refs[n_acc + 1:]
    stage, sem, wbf = scratch[:n_acc], scratch[n_acc], scratch[n_acc + 1:]
    w = pl.program_id(0)

    def fetch(a, expert, col_tile, s):
        col = pl.multiple_of(col_tile * tn, tn)
        return pltpu.make_async_copy(w_hbm[a].at[expert, :, pl.ds(col, tn)], stage[a].at[s], sem.at[a, s])

    @pl.when(first_ref[w] == 1)
    def _():
        s = slot_ref[w]

        @pl.when(w == 0)
        def _():
            for a in range(n_acc):
                fetch(a, e_ref[0], jw_ref[0], 0).start()

        @pl.when(hn_ref[w] == 1)
        def _():
            for a in range(n_acc):
                fetch(a, ne_ref[w], nj_ref[w], 1 - s).start()

        for a in range(n_acc):
            fetch(a, e_ref[w], jw_ref[w], s).wait()
            wbf[a][...] = stage[a][s].astype(BF16)

    tm = x_ref.shape[0]
    for nsub in range(1, tm // MOE_SUB + 1):
        @pl.when((act_ref[w] == 1) & (nv_ref[w] == nsub))
        def _():
            rows = nsub * MOE_SUB
            x = x_ref[0:rows, :].astype(BF16)
            out[0:rows, :] = epilogue([_dot(x, wbf[a][...]) for a in range(n_acc)], ()).astype(out.dtype)
            if rows < tm:
                out[rows:, :] = jnp.zeros((tm - rows, out.shape[1]), out.dtype)

    @pl.when(act_ref[w] == 0)
    def _():
        out[...] = jnp.zeros(out.shape, out.dtype)


def _gmm(xs, weights, tables, *, tm, tn, out_dtype, epilogue, vmem_mb, name):
    p, kdim = xs.shape
    n_out = weights[0].shape[2]
    n_items = tables[0].shape[0]
    n_acc = len(weights)
    grid_spec = pltpu.PrefetchScalarGridSpec(
        num_scalar_prefetch=len(tables),
        grid=(n_items,),
        in_specs=[pl.BlockSpec((tm, kdim), lambda w, rt, *_: (rt[w], 0))]
        + [pl.BlockSpec(memory_space=pl.ANY) for _ in weights],
        out_specs=pl.BlockSpec((tm, tn), lambda w, rt, e, jo, *_: (rt[w], jo[w])),
        scratch_shapes=[pltpu.VMEM((2, kdim, tn), weights[0].dtype) for _ in weights]
        + [pltpu.SemaphoreType.DMA((n_acc, 2))]
        + [pltpu.VMEM((kdim, tn), BF16) for _ in weights],
    )
    return pl.pallas_call(
        functools.partial(_gmm_body, n_acc=n_acc, tn=tn, epilogue=epilogue),
        grid_spec=grid_spec,
        out_shape=jax.ShapeDtypeStruct((p, n_out), out_dtype),
        compiler_params=_cp(("arbitrary",), vmem_mb),
        name=name,
    )(*tables, xs, *weights)


def _work_tables(counts, n_col_tiles, tm, n_tiles):
    tiles_e = (counts + tm - 1) // tm
    tile_end = jnp.cumsum(tiles_e)
    tile_start = tile_end - tiles_e
    total_tiles = tile_end[-1]
    n_items = n_col_tiles * n_tiles
    item_end = n_col_tiles * tile_end
    w = jnp.arange(n_items, dtype=jnp.int32)
    n_active = n_col_tiles * total_tiles
    active = w < n_active
    wc = jnp.minimum(w, n_active - 1)
    e = jnp.sum((wc[:, None] >= item_end[None, :]).astype(jnp.int32), axis=1)
    e = jnp.minimum(e, N_EXPERTS - 1)
    te = jnp.maximum(tiles_e[e], 1)
    local = wc - n_col_tiles * tile_start[e]
    j = local // te
    q = local % te
    rt = tile_start[e] + q
    first = (q == 0) & active
    slot = (jnp.cumsum(first.astype(jnp.int32)) - 1) % 2
    nxt = lax.cummin(jnp.where(first, w, n_items), axis=0, reverse=True)
    nxt = jnp.concatenate([nxt[1:], jnp.full((1,), n_items, nxt.dtype)])
    has_next = nxt < n_items
    nxt = jnp.minimum(nxt, n_items - 1)
    spare = jnp.maximum(w - n_active, 0)
    rt = jnp.where(active, rt, total_tiles + spare // n_col_tiles)
    j_out = jnp.where(active, j, spare % n_col_tiles)
    rows_valid = jnp.clip(counts[e] - q * tm, 1, tm)
    n_sub = (rows_valid + MOE_SUB - 1) // MOE_SUB
    i32 = lambda a: a.astype(jnp.int32)
    return (i32(rt), i32(e), i32(j_out), i32(j), i32(first), i32(active), i32(slot), i32(e[nxt]), i32(j[nxt]),
            i32(has_next), i32(n_sub))


def _combine_body(pos_ref, x_ref, info_ref, g_ref, y_hbm, o_ref, buf, sem, *, tm):
    i = pl.program_id(0)
    n = pl.num_programs(0)

    def gather(tile, slot):
        def issue(r, c):
            for k in range(TOP_K):
                _row_copy(y_hbm, pos_ref[(tile * tm + r) * TOP_K + k], buf.at[slot, k], r,
                          sem.at[slot]).start(priority=k % DMA_QUEUES)
            return c

        lax.fori_loop(0, tm, issue, 0, unroll=SCALAR_UNROLL)

    @pl.when(i == 0)
    def _():
        gather(0, 0)

    for slot in range(2):
        @pl.when((i + 1 < n) & ((i + 1) % 2 == slot))
        def _():
            gather(i + 1, slot)

    for slot in range(2):
        @pl.when(i % 2 == slot)
        def _():
            for k in range(TOP_K):
                _wait_rows(y_hbm, buf.at[slot, k], sem.at[slot], tm)
            info = info_ref[...]
            acc = x_ref[...] + info[:, 4:5] * buf[slot, 0] + info[:, 5:6] * buf[slot, 1]
            y = acc * lax.rsqrt(jnp.mean(acc * acc, axis=-1, keepdims=True) + NORM_EPS)
            o_ref[...] = y * g_ref[...]


def _combine(x, info, pos_flat, y_sorted, final_norm, *, tm=128):
    t, d = x.shape
    grid_spec = pltpu.PrefetchScalarGridSpec(
        num_scalar_prefetch=1,
        grid=(t // tm,),
        in_specs=[pl.BlockSpec((tm, d), lambda i, pos: (i, 0)),
                  pl.BlockSpec((tm, LANES), lambda i, pos: (i, 0)),
                  pl.BlockSpec((1, d), lambda i, pos: (0, 0)),
                  pl.BlockSpec(memory_space=pl.ANY)],
        out_specs=pl.BlockSpec((tm, d), lambda i, pos: (i, 0)),
        scratch_shapes=[pltpu.VMEM((2, TOP_K, tm, d), F32), pltpu.SemaphoreType.DMA((2,))],
    )
    return pl.pallas_call(
        functools.partial(_combine_body, tm=tm),
        grid_spec=grid_spec,
        out_shape=jax.ShapeDtypeStruct((t, d), F32),
        compiler_params=_cp(("arbitrary",), 40),
        name="moe_combine",
    )(pos_flat, x, info, final_norm.reshape(1, d).astype(F32), y_sorted)


def _pack_w_body(w_ref, o_ref, *, gate_lo):
    w = w_ref[...]
    zeros = lambda n: jnp.zeros((w.shape[0], n), w.dtype)
    cat = jnp.concatenate([w[:, :MLA_IN], zeros(LANES - MLA_ROPE), w[:, gate_lo:], zeros(LANES - 2 * ML_HEADS),
                           w[:, MLA_IN:gate_lo]], axis=1)
    o_ref[...] = cat.astype(o_ref.dtype)


def _pack_even_w_in(w_in, gate_lo, *, tr=256):
    k, n = w_in.shape
    n_cat = n + (LANES - MLA_ROPE) + (LANES - 2 * ML_HEADS)
    return pl.pallas_call(
        functools.partial(_pack_w_body, gate_lo=gate_lo),
        grid=(k // tr,),
        in_specs=[pl.BlockSpec((tr, n), lambda i: (i, 0))],
        out_specs=pl.BlockSpec((tr, n_cat), lambda i: (i, 0)),
        out_shape=jax.ShapeDtypeStruct((k, n_cat), BF16),
        compiler_params=_cp(("parallel",), 40),
        name="pack_w_in",
    )(w_in)


def _even_mixers(x, tables, w_in, attn_norm, q_norm, w_uq, kv_norm, w_ukv, gate_bias, head_norm, batch, seq):
    t = x.shape[0]
    c_tab, s1_tab, s2_tab = tables
    n_ml_main = 2 * ML_HEADS * ML_QK + 2 * ML_HEADS * ML_V
    gate_lo = MLA_IN + n_ml_main
    w_cat = _pack_even_w_in(w_in, gate_lo)
    n_a = MLA_IN + LANES - MLA_ROPE + LANES
    w_q = jnp.pad(w_uq.reshape(MLA_Q_LORA, MLA_HEADS, MLA_NOPE + MLA_ROPE),
                  ((0, 0), (0, 0), (0, MLA_QK_PAD - MLA_NOPE - MLA_ROPE))).reshape(MLA_Q_LORA, -1).astype(BF16)
    w_kv3 = w_ukv.reshape(MLA_KV_LORA, MLA_HEADS, MLA_NOPE + MLA_V)
    w_k = jnp.pad(w_kv3[:, :, :MLA_NOPE], ((0, 0), (0, 0), (0, MLA_QK_PAD - MLA_NOPE))).reshape(MLA_KV_LORA, -1).astype(BF16)
    w_v = w_kv3[:, :, MLA_NOPE:].reshape(MLA_KV_LORA, -1).astype(BF16)

    xn = _rmsnorm(x, attn_norm, width=D_MODEL)
    u_a = _matmul_ws([xn], [[(w_cat, 0)]], n_out=n_a, tm=1024, tn=n_a // 2, out_dtype=F32, epilogue=_epi_plain,
                     name="even_in_a")
    u_b = _matmul_ws([xn], [[(w_cat, 0)]], n_out=n_ml_main, tm=1024, tn=512, out_dtype=BF16, epilogue=_epi_plain,
                     col_offset=n_a, name="even_in_b")

    cqn = _rmsnorm(u_a, q_norm, width=MLA_Q_LORA, col_block=0)
    ckvn = _rmsnorm(u_a, kv_norm, width=MLA_KV_LORA, col_block=MLA_Q_LORA // MLA_KV_LORA)
    tm_p = min(1024, t)
    tab_specs = [(tab, (tm_p, LANES), lambda i, j: (i, 0)) for tab in (c_tab, s1_tab, s2_tab)]
    n_qk = MLA_HEADS * MLA_QK_PAD
    q_full = _matmul_ws([cqn], [[(w_q, 0)]], n_out=n_qk, tm=tm_p, tn=1024, out_dtype=BF16, epilogue=_epi_mla_q,
                        extras=tab_specs, name="mla_q")
    kr_spec = (u_a, (tm_p, LANES), lambda i, j: (i, (MLA_Q_LORA + MLA_KV_LORA) // LANES))
    k_full = _matmul_ws([ckvn], [[(w_k, 0)]], n_out=n_qk, tm=tm_p, tn=1024, out_dtype=BF16, epilogue=_epi_mla_k,
                        extras=[kr_spec] + tab_specs, name="mla_k")
    v = _matmul_ws([ckvn], [[(w_v, 0)]], n_out=MLA_HEADS * MLA_V, tm=tm_p, tn=1024, out_dtype=BF16,
                   epilogue=_epi_plain, name="mla_v")
    y_a = _mla_attention(q_full, k_full, v, batch, seq)

    bias_pad = jnp.concatenate([gate_bias.astype(F32), jnp.zeros((LANES - 2 * ML_HEADS,), F32)]).reshape(1, LANES)
    y_b = _mlstm(u_b, u_a, (MLA_IN + LANES - MLA_ROPE) // LANES, bias_pad, head_norm, batch, seq)
    return y_a, y_b


def _even_layer(x, tables, w_in, attn_norm, q_norm, w_uq, kv_norm, w_ukv, gate_bias, head_norm, w_out,
                ffn_norm, w_gate, w_up, w_down, batch, seq):
    y_a, y_b = _even_mixers(x, tables, w_in, attn_norm, q_norm, w_uq, kv_norm, w_ukv, gate_bias, head_norm,
                            batch, seq)
    tm = min(1024, x.shape[0])
    res_spec = lambda arr, tn: [(arr, (tm, tn), lambda i, j: (i, j))]
    x1 = _matmul_ws([y_a, y_b], [[(w_out, 0), (w_out, 1)]], n_out=D_MODEL, tm=tm, tn=512, out_dtype=F32,
                    epilogue=_epi_residual, extras=res_spec(x, 512), name="even_out")
    hn = _rmsnorm(x1, ffn_norm, width=D_MODEL)
    a = _matmul_ws([hn], [[(w_gate, 0)], [(w_up, 0)]], n_out=D_FF, tm=min(2048, x.shape[0]), tn=256, out_dtype=BF16,
                   epilogue=_epi_swiglu, vmem_mb=58, name="ffn_gate_up")
    tm_d = min(2048, x.shape[0])
    x2 = _matmul([a], [[(w_down, 0)]], n_out=D_MODEL, tm=tm_d, tn=1024, tk=512, out_dtype=F32, epilogue=_epi_residual,
                 extras=[(x1, (tm_d, 1024), lambda i, j: (i, j))], vmem_mb=58, name="ffn_down")
    return x2


def _odd_layer(x, w_in, attn_norm, conv_w, w_out, ffn_norm, w_router, b_router, w_gate_e, w_up_e, w_down_e,
               final_norm, batch, seq):
    t = x.shape[0]
    xn = _rmsnorm(x, attn_norm, width=D_MODEL)
    n_in = w_in.shape[1]
    tm = min(1024, t)
    u = _matmul_ws([xn], [[(w_in, 0)]], n_out=n_in, tm=tm, tn=512, out_dtype=BF16, epilogue=_epi_plain,
                   name="odd_in")
    y_c = _short_conv(u, conv_w, batch, seq)
    y_d = _sb_attention(u, 3 * SC_WIDTH // LANES, batch, seq)
    x1 = _matmul_ws([y_c, y_d], [[(w_out, 0), (w_out, 1)]], n_out=D_MODEL, tm=tm, tn=512, out_dtype=F32,
                    epilogue=_epi_residual, extras=[(x, (tm, 512), lambda i, j: (i, j))], name="odd_out")
    return _moe(x1, ffn_norm, w_router, b_router, w_gate_e, w_up_e, w_down_e, final_norm)


def _moe(x1, ffn_norm, w_router, b_router, w_gate_e, w_up_e, w_down_e, final_norm):
    t = x1.shape[0]
    h, info, cnt = _route(x1, ffn_norm, w_router, b_router)
    tm = MOE_TM
    n_tiles = (t * TOP_K) // tm + N_EXPERTS
    counts = cnt[0, :N_EXPERTS].astype(jnp.int32)
    tiles_e = (counts + tm - 1) // tm
    offs = (jnp.cumsum(tiles_e) - tiles_e) * tm
    idx = info[:, 0:TOP_K].astype(jnp.int32)
    rank = info[:, TOP_K:2 * TOP_K].astype(jnp.int32)
    pos_flat = (offs[idx] + rank).reshape(-1)
    n_active = jnp.sum(tiles_e).astype(jnp.int32).reshape(1)

    inv = _invperm(pos_flat, n_tiles * tm)
    xs = _dispatch(h, inv, n_active, n_tiles, tm=tm)
    tn_gu, tn_d = 512, 1024
    tab_gu = _work_tables(counts, D_FF_EXPERT // tn_gu, tm, n_tiles)
    a_s = _gmm(xs, [w_gate_e, w_up_e], tab_gu, tm=tm, tn=tn_gu, out_dtype=BF16, epilogue=_epi_swiglu,
               vmem_mb=58, name="moe_gate_up")
    tab_d = _work_tables(counts, D_MODEL // tn_d, tm, n_tiles)
    y_s = _gmm(a_s, [w_down_e], tab_d, tm=tm, tn=tn_d, out_dtype=F32, epilogue=_epi_plain,
               vmem_mb=58, name="moe_down")
    return _combine(x1, info, pos_flat, y_s, final_norm)


def kernel(x, positions, even_attn_norm, even_w_in, even_q_norm, even_w_uq, even_kv_norm, even_w_ukv, even_ml_gate_bias, even_ml_head_norm, even_w_out, even_ffn_norm, even_w_gate, even_w_up, even_w_down, odd_attn_norm, odd_w_in, odd_conv_w, odd_w_out, odd_ffn_norm, odd_w_router, odd_b_router, odd_w_gate_e, odd_w_up_e, odd_w_down_e, final_norm):
    batch, seq, d = x.shape
    assert even_w_in.shape[0] == 1 and odd_w_in.shape[0] == 1, "kernel is written for one even and one odd layer"
    xf = x.reshape(batch * seq, d)
    tables = _rope_tables(positions)
    xf = _even_layer(xf, tables, even_w_in[0], even_attn_norm[0], even_q_norm[0], even_w_uq[0], even_kv_norm[0],
                     even_w_ukv[0], even_ml_gate_bias[0], even_ml_head_norm[0], even_w_out[0], even_ffn_norm[0],
                     even_w_gate[0], even_w_up[0], even_w_down[0], batch, seq)
    out = _odd_layer(xf, odd_w_in[0], odd_attn_norm[0], odd_conv_w[0], odd_w_out[0], odd_ffn_norm[0],
                     odd_w_router[0], odd_b_router[0], odd_w_gate_e[0], odd_w_up_e[0], odd_w_down_e[0],
                     final_norm, batch, seq)
    return out.reshape(batch, seq, d)
```

```python
import functools

import jax
import jax.numpy as jnp
from jax import lax
from jax.experimental import pallas as pl
from jax.experimental.pallas import tpu as pltpu

F32 = jnp.float32
BF16 = jnp.bfloat16

D_MODEL = 4096
NORM_EPS = 1e-6

MLA_HEADS = 16
MLA_Q_LORA = 1024
MLA_KV_LORA = 512
MLA_NOPE = 128
MLA_ROPE = 64
MLA_V = 128
ROPE_THETA = 10000.0
MLA_IN = MLA_Q_LORA + MLA_KV_LORA + MLA_ROPE
MLA_QK_PAD = 256

ML_HEADS = 4
ML_QK = 256
ML_V = 512
ML_CHUNK = 128
ML_VA = ML_V + 128

SC_WIDTH = 2048
SC_KERNEL = 3

SB_HEADS = 16
SB_HEAD_DIM = 128

D_FF = 11008
N_EXPERTS = 8
TOP_K = 2
D_FF_EXPERT = 4096

LANES = 128
SUBLANES_BF16 = 16
MOE_TM = 512
MOE_SUB = 256
DMA_QUEUES = 2
LOG2_E = 1.4426950408889634
SCALAR_UNROLL = 8


def _cp(sem, vmem_mb):
    return pltpu.CompilerParams(dimension_semantics=sem, vmem_limit_bytes=vmem_mb * 1024 * 1024)


def _dot(a, b):
    return jnp.dot(a, b, preferred_element_type=F32)


def _dot_nt(a, b):
    return lax.dot_general(a, b, (((1,), (1,)), ((), ())), preferred_element_type=F32)


def _log_sigmoid(x):
    return jnp.minimum(x, 0.0) - jnp.log1p(jnp.exp(-jnp.abs(x)))


def _split_dot(tri, x, pieces):
    acc = None
    r = x
    for p in range(pieces):
        hi = r.astype(BF16)
        part = _dot(tri, hi)
        acc = part if acc is None else acc + part
        if p + 1 < pieces:
            r = r - hi.astype(F32)
    return acc


def _rmsnorm_body(x_ref, g_ref, o_ref):
    x = x_ref[...].astype(F32)
    y = x * lax.rsqrt(jnp.mean(x * x, axis=-1, keepdims=True) + NORM_EPS)
    o_ref[...] = (y * g_ref[...]).astype(o_ref.dtype)


def _rmsnorm(x, g, *, width, col_block=0, out_dtype=BF16, tm=256):
    m = x.shape[0]
    return pl.pallas_call(
        _rmsnorm_body,
        grid=(m // tm,),
        in_specs=[pl.BlockSpec((tm, width), lambda i: (i, col_block)),
                  pl.BlockSpec((1, width), lambda i: (0, 0))],
        out_specs=pl.BlockSpec((tm, width), lambda i: (i, 0)),
        out_shape=jax.ShapeDtypeStruct((m, width), out_dtype),
        compiler_params=_cp(("parallel",), 40),
        name="rmsnorm",
    )(x, g.reshape(1, width).astype(F32))


def _mm_body(*refs, n_lhs, n_acc, n_extra, nk, k_valid_last, epilogue):
    lhs = refs[:n_lhs]
    ws = refs[n_lhs:n_lhs + n_acc * n_lhs]
    extras = refs[n_lhs + n_acc * n_lhs:n_lhs + n_acc * n_lhs + n_extra]
    out = refs[n_lhs + n_acc * n_lhs + n_extra]
    accs = refs[n_lhs + n_acc * n_lhs + n_extra + 1:]
    k = pl.program_id(2)

    def partials(mask_tail):
        parts = []
        for a in range(n_acc):
            s = None
            for l in range(n_lhs):
                x = lhs[l][...]
                w = ws[a * n_lhs + l][...]
                if mask_tail:
                    xc = lax.broadcasted_iota(jnp.int32, x.shape, 1)
                    x = jnp.where(xc < k_valid_last, x, jnp.zeros_like(x))
                    wr = lax.broadcasted_iota(jnp.int32, w.shape, 0)
                    w = jnp.where(wr < k_valid_last, w, jnp.zeros_like(w))
                p = _dot(x.astype(BF16), w.astype(BF16))
                s = p if s is None else s + p
            parts.append(s)
        return parts

    if nk == 1:
        out[...] = epilogue(partials(False), extras).astype(out.dtype)
        return

    ragged = k_valid_last is not None

    @pl.when(k == 0)
    def _():
        for a, p in enumerate(partials(False)):
            accs[a][...] = p

    @pl.when((k > 0) & (k < nk - 1) if ragged else (k > 0))
    def _():
        for a, p in enumerate(partials(False)):
            accs[a][...] += p

    if ragged:
        @pl.when(k == nk - 1)
        def _():
            for a, p in enumerate(partials(True)):
                accs[a][...] += p

    @pl.when(k == nk - 1)
    def _():
        out[...] = epilogue([acc[...] for acc in accs], extras).astype(out.dtype)


def _matmul(lhs, weights, *, n_out, tm, tn, tk, out_dtype, epilogue, extras=(), vmem_mb=48, name="matmul"):
    m = lhs[0].shape[0]
    kdim = lhs[0].shape[1]
    tm = min(tm, m)
    nk = pl.cdiv(kdim, tk)
    k_valid_last = None if kdim % tk == 0 else kdim - (nk - 1) * tk
    n_lhs, n_acc = len(lhs), len(weights)
    grid = (m // tm, n_out // tn, nk)
    ij = lambda g0, g1: (g0, g1)

    in_specs, args = [], []
    for x in lhs:
        in_specs.append(pl.BlockSpec((tm, tk), lambda g0, g1, k: (ij(g0, g1)[0], k)))
        args.append(x)
    for wl in weights:
        for (w, rb) in wl:
            in_specs.append(pl.BlockSpec((tk, tn), lambda g0, g1, k, rb=rb: (rb * nk + k, ij(g0, g1)[1])))
            args.append(w)
    for (arr, bshape, imap) in extras:
        in_specs.append(pl.BlockSpec(bshape, lambda g0, g1, k, imap=imap: imap(*ij(g0, g1))))
        args.append(arr)
    scratch = [pltpu.VMEM((tm, tn), F32) for _ in range(n_acc)] if nk > 1 else []
    body = functools.partial(_mm_body, n_lhs=n_lhs, n_acc=n_acc, n_extra=len(extras), nk=nk,
                             k_valid_last=k_valid_last, epilogue=epilogue)
    return pl.pallas_call(
        body,
        grid=grid,
        in_specs=in_specs,
        out_specs=pl.BlockSpec((tm, tn), lambda g0, g1, k: ij(g0, g1)),
        out_shape=jax.ShapeDtypeStruct((m, n_out), out_dtype),
        scratch_shapes=scratch,
        compiler_params=_cp(("parallel", "parallel", "arbitrary"), vmem_mb),
        name=name,
    )(*args)


def _mm_ws_body(*refs, n_lhs, n_acc, n_extra, tn, kdims, row_blocks, col_offset, transposed, cast, epilogue):
    n_w = n_acc * n_lhs
    lhs = refs[:n_lhs]
    w_hbm = refs[n_lhs:n_lhs + n_w]
    extras = refs[n_lhs + n_w:n_lhs + n_w + n_extra]
    out = refs[n_lhs + n_w + n_extra]
    scratch = refs[n_lhs + n_w + n_extra + 1:]
    stage, sem, wbf = scratch[:n_w], scratch[n_w], scratch[n_w + 1:]
    j = pl.program_id(0)
    i = pl.program_id(1)
    nj = pl.num_programs(0)
    slot = j % 2

    def fetch(widx, jj, s):
        kd = kdims[widx % n_lhs]
        if transposed:
            col = pl.multiple_of(col_offset + jj * tn, SUBLANES_BF16)
            src = w_hbm[widx].at[pl.ds(col, tn), pl.ds(row_blocks[widx] * kd, kd)]
        else:
            col = pl.multiple_of(col_offset + jj * tn, LANES)
            src = w_hbm[widx].at[pl.ds(row_blocks[widx] * kd, kd), pl.ds(col, tn)]
        return pltpu.make_async_copy(src, stage[widx].at[s], sem.at[widx, s])

    @pl.when(i == 0)
    def _():
        @pl.when(j == 0)
        def _():
            for widx in range(n_w):
                fetch(widx, 0, 0).start()

        @pl.when(j + 1 < nj)
        def _():
            for widx in range(n_w):
                fetch(widx, j + 1, 1 - slot).start()

        for widx in range(n_w):
            fetch(widx, j, slot).wait()
            if cast:
                wbf[widx][...] = stage[widx][slot].astype(BF16)

    parts = []
    for a in range(n_acc):
        s = None
        for l in range(n_lhs):
            widx = a * n_lhs + l
            w = wbf[widx][...] if cast else stage[widx][slot]
            x = lhs[l][...].astype(BF16)
            p = _dot_nt(x, w) if transposed else _dot(x, w)
            s = p if s is None else s + p
        parts.append(s)
    out[...] = epilogue(parts, extras).astype(out.dtype)


def _matmul_ws(lhs, weights, *, n_out, tm, tn, out_dtype, epilogue, extras=(), col_offset=0, transposed=False,
               vmem_mb=48, name="matmul_ws"):
    assert tn % LANES == 0 and col_offset % (SUBLANES_BF16 if transposed else LANES) == 0
    m = lhs[0].shape[0]
    tm = min(tm, m)
    n_lhs, n_acc = len(lhs), len(weights)
    kdims = tuple(x.shape[1] for x in lhs)
    flat_w = [w for wl in weights for (w, _) in wl]
    row_blocks = tuple(rb for wl in weights for (_, rb) in wl)
    wdtype = flat_w[0].dtype
    cast = wdtype != BF16
    in_specs = [pl.BlockSpec((tm, kd), lambda j, i: (i, 0)) for kd in kdims]
    in_specs += [pl.BlockSpec(memory_space=pl.ANY) for _ in flat_w]
    in_specs += [pl.BlockSpec(bshape, lambda j, i, imap=imap: imap(i, j)) for (_, bshape, imap) in extras]
    w_kdims = [kdims[widx % n_lhs] for widx in range(len(flat_w))]
    tile = (lambda kd: (tn, kd)) if transposed else (lambda kd: (kd, tn))
    scratch = [pltpu.VMEM((2,) + tile(kd), wdtype) for kd in w_kdims]
    scratch.append(pltpu.SemaphoreType.DMA((len(flat_w), 2)))
    if cast:
        scratch += [pltpu.VMEM(tile(kd), BF16) for kd in w_kdims]
    body = functools.partial(_mm_ws_body, n_lhs=n_lhs, n_acc=n_acc, n_extra=len(extras), tn=tn, kdims=kdims,
                             row_blocks=row_blocks, col_offset=col_offset, transposed=transposed, cast=cast,
                             epilogue=epilogue)
    return pl.pallas_call(
        body,
        grid=(n_out // tn, m // tm),
        in_specs=in_specs,
        out_specs=pl.BlockSpec((tm, tn), lambda j, i: (i, j)),
        out_shape=jax.ShapeDtypeStruct((m, n_out), out_dtype),
        scratch_shapes=scratch,
        compiler_params=_cp(("arbitrary", "arbitrary"), vmem_mb),
        name=name,
    )(*lhs, *flat_w, *[arr for (arr, _, _) in extras])


def _epi_plain(parts, extras):
    return parts[0]


def _epi_residual(parts, extras):
    return parts[0] + extras[0][...]


def _epi_swiglu(parts, extras):
    g, u = parts
    return g * jax.nn.sigmoid(g) * u


def _rope_tables_body(pos_ref, invf_ref, c_ref, s1_ref, s2_ref):
    ang = pos_ref[...] * invf_ref[...]
    lane = lax.broadcasted_iota(jnp.int32, ang.shape, 1)
    cos = jnp.cos(ang)
    sin = jnp.sin(ang)
    half = MLA_ROPE // 2
    c_ref[...] = jnp.where(lane < MLA_ROPE, cos, 0.0)
    s1_ref[...] = jnp.where(lane < half, -sin, 0.0)
    s2_ref[...] = jnp.where((lane >= half) & (lane < MLA_ROPE), sin, 0.0)


def _rope_tables(positions):
    t = positions.size
    half = MLA_ROPE // 2
    inv_freq = 1.0 / (ROPE_THETA ** (jnp.arange(0, MLA_ROPE, 2, dtype=F32) / MLA_ROPE))
    invf = jnp.concatenate([inv_freq, inv_freq, jnp.zeros((LANES - 2 * half,), F32)]).reshape(1, LANES)
    pos = jnp.broadcast_to(positions.reshape(t, 1).astype(F32), (t, LANES))
    tm = min(512, t)
    spec = pl.BlockSpec((tm, LANES), lambda i: (i, 0))
    sds = jax.ShapeDtypeStruct((t, LANES), F32)
    return pl.pallas_call(
        _rope_tables_body,
        grid=(t // tm,),
        in_specs=[spec, pl.BlockSpec((1, LANES), lambda i: (0, 0))],
        out_specs=[spec, spec, spec],
        out_shape=[sds, sds, sds],
        compiler_params=_cp(("parallel",), 32),
        name="rope_tables",
    )(pos, invf)


def _rope_lanes(t, c, s1, s2):
    return t * c + pltpu.roll(t, LANES - MLA_ROPE // 2, axis=1) * s1 + pltpu.roll(t, MLA_ROPE // 2, axis=1) * s2


def _epi_mla_q(parts, extras):
    acc = parts[0]
    c, s1, s2 = extras[0][...], extras[1][...], extras[2][...]
    scale = (MLA_NOPE + MLA_ROPE) ** -0.5
    pieces = []
    for h in range(acc.shape[1] // MLA_QK_PAD):
        base = h * MLA_QK_PAD
        pieces.append(acc[:, base:base + MLA_NOPE] * scale)
        pieces.append(_rope_lanes(acc[:, base + MLA_NOPE:base + MLA_QK_PAD], c, s1, s2) * scale)
    return jnp.concatenate(pieces, axis=1)


def _epi_mla_k(parts, extras):
    acc = parts[0]
    kr = _rope_lanes(extras[0][...], extras[1][...], extras[2][...], extras[3][...])
    pieces = []
    for h in range(acc.shape[1] // MLA_QK_PAD):
        base = h * MLA_QK_PAD
        pieces.append(acc[:, base:base + MLA_NOPE])
        pieces.append(acc[:, base + MLA_NOPE:base + MLA_QK_PAD] + kr)
    return jnp.concatenate(pieces, axis=1)


def _mla_attn_body(q_ref, k_ref, v_ref, o_ref, *, tq, seq):
    row = lax.broadcasted_iota(jnp.int32, (tq, tq), 0)
    col = lax.broadcasted_iota(jnp.int32, (tq, tq), 1)
    causal = col <= row
    for qi in range(seq // tq):
        n = (qi + 1) * tq
        q = q_ref[qi * tq:(qi + 1) * tq, :]
        s = _dot_nt(q, k_ref[0:n, :])
        diag = jnp.where(causal, s[:, n - tq:], -jnp.inf)
        s = diag if qi == 0 else jnp.concatenate([s[:, :n - tq], diag], axis=1)
        m = jnp.max(s, axis=1, keepdims=True)
        p = jnp.exp(s - m)
        l = jnp.sum(p, axis=1, keepdims=True)
        o = _dot(p.astype(BF16), v_ref[0:n, :]) / l
        o_ref[qi * tq:(qi + 1) * tq, :] = o.astype(o_ref.dtype)


def _mla_attention(q, k, v, batch, seq, *, tq=256):
    t = q.shape[0]
    tq = min(tq, seq)
    body = functools.partial(_mla_attn_body, tq=tq, seq=seq)
    return pl.pallas_call(
        body,
        grid=(batch, MLA_HEADS),
        in_specs=[pl.BlockSpec((seq, MLA_QK_PAD), lambda b, h: (b, h)),
                  pl.BlockSpec((seq, MLA_QK_PAD), lambda b, h: (b, h)),
                  pl.BlockSpec((seq, MLA_V), lambda b, h: (b, h))],
        out_specs=pl.BlockSpec((seq, MLA_V), lambda b, h: (b, h)),
        out_shape=jax.ShapeDtypeStruct((t, MLA_HEADS * MLA_V), BF16),
        compiler_params=_cp(("parallel", "parallel"), 48),
        name="mla_attention",
    )(q, k, v)


def _mlstm_body(q_ref, k_ref, v_ref, o_ref, g_ref, bias_ref, hn_ref, y_ref, ct_ref, m_ref):
    L = ML_CHUNK
    c = pl.program_id(1)

    @pl.when(c == 0)
    def _():
        ct_ref[...] = jnp.zeros(ct_ref.shape, F32)
        m_ref[...] = jnp.zeros(m_ref.shape, F32)

    g = g_ref[...] + bias_ref[...]
    logf = _log_sigmoid(g)
    row = lax.broadcasted_iota(jnp.int32, (L, L), 0)
    col = lax.broadcasted_iota(jnp.int32, (L, L), 1)
    causal = col <= row
    tri = jnp.where(causal, 1.0, 0.0).astype(BF16)
    cum = _split_dot(tri, logf, 3)
    lane = lax.broadcasted_iota(jnp.int32, (L, LANES), 1)
    colq = jnp.where(lane < ML_HEADS, g, cum)
    rowq = colq.T
    ones_col = jnp.where(lane == 0, 1.0, 0.0).astype(BF16)
    scale = ML_QK ** -0.5

    for h in range(ML_HEADS):
        i_c = colq[:, h:h + 1]
        cf_c = colq[:, ML_HEADS + h:ML_HEADS + h + 1]
        i_r = rowq[h:h + 1, :]
        cf_r = rowq[ML_HEADS + h:ML_HEADS + h + 1, :]
        m_prev = m_ref[h][:, 0:1]
        d = jnp.where(causal, cf_c - cf_r + i_r, -jnp.inf)
        m_inter = cf_c + m_prev
        m_t = jnp.maximum(m_inter, jnp.max(d, axis=1, keepdims=True))
        inter = jnp.exp(m_inter - m_t)
        p = jnp.exp(d - m_t)
        qh = q_ref[:, h * ML_QK:(h + 1) * ML_QK]
        kf = k_ref[:, h * ML_QK:(h + 1) * ML_QK].astype(F32) * scale
        kh = kf.astype(BF16)
        s = _dot_nt(qh, kh) * p
        va = jnp.concatenate([v_ref[:, h * ML_V:(h + 1) * ML_V], ones_col], axis=1)
        ct = ct_ref[h]
        na = _dot(s.astype(BF16), va) + inter * _dot(qh, ct.astype(BF16))
        num = na[:, :ML_V]
        den = na[:, ML_V:ML_V + 1]
        hh = num / jnp.maximum(jnp.abs(den), jnp.exp(-m_t))

        f_tot = cf_c[L - 1:L, :]
        w_log = f_tot - cf_c + i_c
        m_new = jnp.maximum(f_tot + m_prev, jnp.max(w_log, axis=0, keepdims=True))
        decay = jnp.exp(f_tot + m_prev - m_new)
        w = jnp.exp(w_log - m_new)
        wv = (va.astype(F32) * w).astype(BF16)
        ct_ref[h] = decay * ct + _dot(kf.T.astype(BF16), wv)
        m_ref[h] = jnp.broadcast_to(m_new, (1, LANES))

        hn = hh * lax.rsqrt(jnp.mean(hh * hh, axis=1, keepdims=True) + NORM_EPS) * hn_ref[:, h * ML_V:(h + 1) * ML_V]
        og = o_ref[:, h * ML_V:(h + 1) * ML_V].astype(F32)
        y_ref[:, h * ML_V:(h + 1) * ML_V] = (jax.nn.sigmoid(og) * hn).astype(y_ref.dtype)


def _mlstm(u_b, gates_src, gate_col_block, bias_pad, head_norm, batch, seq):
    t = u_b.shape[0]
    nc = seq // ML_CHUNK
    nqk = ML_HEADS * ML_QK
    nv = ML_HEADS * ML_V
    rows = lambda b, c: b * nc + c
    return pl.pallas_call(
        _mlstm_body,
        grid=(batch, nc),
        in_specs=[pl.BlockSpec((ML_CHUNK, nqk), lambda b, c: (rows(b, c), 0)),
                  pl.BlockSpec((ML_CHUNK, nqk), lambda b, c: (rows(b, c), 1)),
                  pl.BlockSpec((ML_CHUNK, nv), lambda b, c: (rows(b, c), 1)),
                  pl.BlockSpec((ML_CHUNK, nv), lambda b, c: (rows(b, c), 2)),
                  pl.BlockSpec((ML_CHUNK, LANES), lambda b, c: (rows(b, c), gate_col_block)),
                  pl.BlockSpec((1, LANES), lambda b, c: (0, 0)),
                  pl.BlockSpec((1, nv), lambda b, c: (0, 0))],
        out_specs=pl.BlockSpec((ML_CHUNK, nv), lambda b, c: (rows(b, c), 0)),
        out_shape=jax.ShapeDtypeStruct((t, nv), BF16),
        scratch_shapes=[pltpu.VMEM((ML_HEADS, ML_QK, ML_VA), F32), pltpu.VMEM((ML_HEADS, 1, LANES), F32)],
        compiler_params=_cp(("parallel", "arbitrary"), 32),
        name="mlstm",
    )(u_b, u_b, u_b, u_b, gates_src, bias_pad, head_norm.reshape(1, nv).astype(F32))


def _conv_body(b_ref, c_ref, h_ref, cp_ref, hp_ref, w_ref, y_ref, z_sc, *, ts, halo):
    i = pl.program_id(1)
    z = c_ref[...].astype(F32) * h_ref[...].astype(F32)
    zp = cp_ref[...].astype(F32) * hp_ref[...].astype(F32)
    z_sc[0:halo, :] = jnp.where(i > 0, zp, 0.0)
    z_sc[halo:halo + ts, :] = z
    w = w_ref[...]
    y = w[2:3, :] * z + w[1:2, :] * z_sc[halo - 1:halo - 1 + ts, :] + w[0:1, :] * z_sc[halo - 2:halo - 2 + ts, :]
    y_ref[...] = (b_ref[...].astype(F32) * y).astype(y_ref.dtype)


def _short_conv(u, conv_w, batch, seq, *, ts=256, tw=1024):
    t = u.shape[0]
    halo = 16
    ns = seq // ts
    nw = SC_WIDTH // tw
    wpad = jnp.concatenate([conv_w.astype(F32), jnp.zeros((8 - SC_KERNEL, SC_WIDTH), F32)], axis=0)
    rows = lambda b, i: b * ns + i
    prev = lambda b, i: jnp.maximum((b * seq + i * ts) // halo - 1, 0)
    body = functools.partial(_conv_body, ts=ts, halo=halo)
    return pl.pallas_call(
        body,
        grid=(batch, ns, nw),
        in_specs=[pl.BlockSpec((ts, tw), lambda b, i, j: (rows(b, i), j)),
                  pl.BlockSpec((ts, tw), lambda b, i, j: (rows(b, i), nw + j)),
                  pl.BlockSpec((ts, tw), lambda b, i, j: (rows(b, i), 2 * nw + j)),
                  pl.BlockSpec((halo, tw), lambda b, i, j: (prev(b, i), nw + j)),
                  pl.BlockSpec((halo, tw), lambda b, i, j: (prev(b, i), 2 * nw + j)),
                  pl.BlockSpec((8, tw), lambda b, i, j: (0, j))],
        out_specs=pl.BlockSpec((ts, tw), lambda b, i, j: (rows(b, i), j)),
        out_shape=jax.ShapeDtypeStruct((t, SC_WIDTH), BF16),
        scratch_shapes=[pltpu.VMEM((halo + ts, tw), F32)],
        compiler_params=_cp(("parallel", "parallel", "parallel"), 32),
        name="short_conv",
    )(u, u, u, u, u, wpad)


def _sb_body(q_ref, k_ref, v_ref, o_ref, *, tq, seq):
    scale = SB_HEAD_DIM ** -0.5 * LOG2_E
    row = lax.broadcasted_iota(jnp.int32, (tq, tq), 0)
    col = lax.broadcasted_iota(jnp.int32, (tq, tq), 1)
    strict = col < row
    neg_upper_incl = jnp.where(row >= col, -1.0, 0.0).astype(BF16)
    for qi in range(seq // tq):
        nb = qi + 1
        q = q_ref[qi * tq:(qi + 1) * tq, :]
        z = _dot_nt(q, k_ref[0:nb * tq, :]) * scale
        sp = jnp.maximum(z, 0.0) + jnp.log2(1.0 + jnp.exp2(-jnp.abs(z)))
        later = jnp.zeros((tq, 1), F32)
        blocks = [None] * nb
        for j in range(nb - 1, -1, -1):
            zj = z[:, j * tq:(j + 1) * tq]
            spj = sp[:, j * tq:(j + 1) * tq]
            spm = jnp.where(strict, spj, 0.0) if j == nb - 1 else spj
            upto = _split_dot_rhs(spm, neg_upper_incl, 2)
            e = jnp.exp2(zj + upto + later)
            if j == nb - 1:
                e = jnp.where(strict, e, 0.0)
            blocks[j] = e.astype(BF16)
            later = later + upto[:, 0:1]
        a = blocks[0] if nb == 1 else jnp.concatenate(blocks, axis=1)
        o_ref[qi * tq:(qi + 1) * tq, :] = _dot(a, v_ref[0:nb * tq, :]).astype(o_ref.dtype)


def _split_dot_rhs(x, tri, pieces):
    acc = None
    r = x
    for p in range(pieces):
        hi = r.astype(BF16)
        part = _dot(hi, tri)
        acc = part if acc is None else acc + part
        if p + 1 < pieces:
            r = r - hi.astype(F32)
    return acc


def _sb_attention(u, col_base, batch, seq, *, tq=256):
    t = u.shape[0]
    tq = min(tq, seq)
    body = functools.partial(_sb_body, tq=tq, seq=seq)
    spec = lambda off: pl.BlockSpec((seq, SB_HEAD_DIM), lambda b, h: (b, col_base + off + h))
    return pl.pallas_call(
        body,
        grid=(batch, SB_HEADS),
        in_specs=[spec(0), spec(SB_HEADS), spec(2 * SB_HEADS)],
        out_specs=pl.BlockSpec((seq, SB_HEAD_DIM), lambda b, h: (b, h)),
        out_shape=jax.ShapeDtypeStruct((t, SB_HEADS * SB_HEAD_DIM), BF16),
        compiler_params=_cp(("parallel", "parallel"), 48),
        name="sb_attention",
    )(u, u, u)


def _route_body(x_ref, g_ref, wr_ref, br_ref, h_ref, info_ref, cnt_ref, carry):
    i = pl.program_id(0)

    @pl.when(i == 0)
    def _():
        carry[...] = jnp.zeros(carry.shape, F32)

    x = x_ref[...]
    h = x * lax.rsqrt(jnp.mean(x * x, axis=-1, keepdims=True) + NORM_EPS) * g_ref[...]
    h_ref[...] = h
    wr = wr_ref[...]
    h_hi = h.astype(BF16)
    h_lo = (h - h_hi.astype(F32)).astype(BF16)
    w_hi = wr.astype(BF16)
    w_lo = (wr - w_hi.astype(F32)).astype(BF16)
    logits = (_dot(h_hi, w_hi) + _dot(h_hi, w_lo)) + (_dot(h_lo, w_hi) + _dot(h_lo, w_lo)) + br_ref[...]
    tm = logits.shape[0]
    lane = lax.broadcasted_iota(jnp.int32, logits.shape, 1).astype(F32)
    lg = jnp.where(lane < N_EXPERTS, logits, -jnp.inf)
    m1 = jnp.max(lg, axis=1, keepdims=True)
    i1 = jnp.min(jnp.where(lg == m1, lane, float(LANES)), axis=1, keepdims=True)
    lg2 = jnp.where(lane == i1, -jnp.inf, lg)
    m2 = jnp.max(lg2, axis=1, keepdims=True)
    i2 = jnp.min(jnp.where(lg2 == m2, lane, float(LANES)), axis=1, keepdims=True)
    e = jnp.exp(m2 - m1)
    w1 = 1.0 / (1.0 + e)
    w2 = e / (1.0 + e)
    oh1 = lane == i1
    oh2 = lane == i2
    mask = jnp.where(oh1 | oh2, 1.0, 0.0)
    row = lax.broadcasted_iota(jnp.int32, (tm, tm), 0)
    col = lax.broadcasted_iota(jnp.int32, (tm, tm), 1)
    before = jnp.where(col < row, 1.0, 0.0).astype(BF16)
    rank_mat = _dot(before, mask.astype(BF16)) + carry[...]
    r1 = jnp.sum(jnp.where(oh1, rank_mat, 0.0), axis=1, keepdims=True)
    r2 = jnp.sum(jnp.where(oh2, rank_mat, 0.0), axis=1, keepdims=True)
    carry[...] += jnp.sum(mask, axis=0, keepdims=True)
    info = jnp.where(lane == 0, i1,
           jnp.where(lane == 1, i2,
           jnp.where(lane == 2, r1,
           jnp.where(lane == 3, r2,
           jnp.where(lane == 4, w1,
           jnp.where(lane == 5, w2, 0.0))))))
    info_ref[...] = info
    cnt_ref[...] = jnp.broadcast_to(carry[...], cnt_ref.shape)


def _route(x, g, w_router, b_router, *, tm=256):
    t, d = x.shape
    wr = jnp.concatenate([w_router.astype(F32), jnp.zeros((d, LANES - N_EXPERTS), F32)], axis=1)
    br = jnp.concatenate([b_router.astype(F32), jnp.zeros((LANES - N_EXPERTS,), F32)]).reshape(1, LANES)
    return pl.pallas_call(
        _route_body,
        grid=(t // tm,),
        in_specs=[pl.BlockSpec((tm, d), lambda i: (i, 0)),
                  pl.BlockSpec((1, d), lambda i: (0, 0)),
                  pl.BlockSpec((d, LANES), lambda i: (0, 0)),
                  pl.BlockSpec((1, LANES), lambda i: (0, 0))],
        out_specs=[pl.BlockSpec((tm, d), lambda i: (i, 0)),
                   pl.BlockSpec((tm, LANES), lambda i: (i, 0)),
                   pl.BlockSpec((8, LANES), lambda i: (0, 0))],
        out_shape=[jax.ShapeDtypeStruct((t, d), F32),
                   jax.ShapeDtypeStruct((t, LANES), F32),
                   jax.ShapeDtypeStruct((8, LANES), F32)],
        scratch_shapes=[pltpu.VMEM((1, LANES), F32)],
        compiler_params=_cp(("arbitrary",), 40),
        name="moe_route",
    )(x, g.reshape(1, d).astype(F32), wr, br)


def _invperm_body(pos_ref, inv_ref, *, n_assign, n_slots):
    def clear(p, c):
        inv_ref[p] = 0
        return c

    lax.fori_loop(0, n_slots, clear, 0, unroll=SCALAR_UNROLL)

    def put(a, c):
        inv_ref[pos_ref[a]] = lax.shift_right_logical(a, TOP_K.bit_length() - 1)
        return c

    lax.fori_loop(0, n_assign, put, 0, unroll=SCALAR_UNROLL)


def _invperm(pos_flat, n_slots):
    n_assign = pos_flat.shape[0]
    return pl.pallas_call(
        functools.partial(_invperm_body, n_assign=n_assign, n_slots=n_slots),
        in_specs=[pl.BlockSpec(memory_space=pltpu.SMEM)],
        out_specs=pl.BlockSpec(memory_space=pltpu.SMEM),
        out_shape=jax.ShapeDtypeStruct((n_slots,), jnp.int32),
        name="moe_invperm",
    )(pos_flat)


def _row_copy(src_hbm, row, dst_vmem, slot, sem):
    return pltpu.make_async_copy(src_hbm.at[pl.ds(row, 1), :], dst_vmem.at[pl.ds(slot, 1), :], sem)


def _wait_rows(src_hbm, dst_vmem, sem, n):
    def wait(r, c):
        _row_copy(src_hbm, 0, dst_vmem, r, sem).wait()
        return c

    lax.fori_loop(0, n, wait, 0, unroll=SCALAR_UNROLL)


def _dispatch_body(inv_ref, nact_ref, h_hbm, o_ref, buf, sem, *, tm):
    i = pl.program_id(0)
    nact = nact_ref[0]

    def gather(tile, slot):
        def issue(rr, c):
            for u in range(DMA_QUEUES):
                r = rr * DMA_QUEUES + u
                _row_copy(h_hbm, inv_ref[tile * tm + r], buf.at[slot], r, sem.at[slot]).start(priority=u)
            return c

        lax.fori_loop(0, tm // DMA_QUEUES, issue, 0, unroll=SCALAR_UNROLL // DMA_QUEUES)

    @pl.when((i == 0) & (nact > 0))
    def _():
        gather(0, 0)

    for slot in range(2):
        @pl.when((i + 1 < nact) & ((i + 1) % 2 == slot))
        def _():
            gather(i + 1, slot)

    for slot in range(2):
        @pl.when((i < nact) & (i % 2 == slot))
        def _():
            _wait_rows(h_hbm, buf.at[slot], sem.at[slot], tm)
            o_ref[...] = buf[slot].astype(o_ref.dtype)

    @pl.when(i >= nact)
    def _():
        o_ref[...] = jnp.zeros(o_ref.shape, o_ref.dtype)


def _dispatch(h, inv, n_active_tiles, n_tiles, *, tm):
    d = h.shape[1]
    grid_spec = pltpu.PrefetchScalarGridSpec(
        num_scalar_prefetch=2,
        grid=(n_tiles,),
        in_specs=[pl.BlockSpec(memory_space=pl.ANY)],
        out_specs=pl.BlockSpec((tm, d), lambda i, inv, na: (i, 0)),
        scratch_shapes=[pltpu.VMEM((2, tm, d), F32), pltpu.SemaphoreType.DMA((2,))],
    )
    return pl.pallas_call(
        functools.partial(_dispatch_body, tm=tm),
        grid_spec=grid_spec,
        out_shape=jax.ShapeDtypeStruct((n_tiles * tm, d), BF16),
        compiler_params=_cp(("arbitrary",), 40),
        name="moe_dispatch",
    )(inv, n_active_tiles, h)


def _gmm_body(rt_ref, e_ref, jo_ref, jw_ref, first_ref, act_ref, slot_ref, ne_ref, nj_ref, hn_ref, nv_ref, x_ref,
              *refs, n_acc, tn, epilogue):
    w_hbm = refs[:n_acc]
    out = refs[n_acc]
    scratch = refs[n_acc + 1:]
    stage, sem, wbf = scratch[:n_acc], scratch[n_acc], scratch[n_acc + 1:]
    w = pl.program_id(0)

    def fetch(a, expert, col_tile, s):
        col = pl.multiple_of(col_tile * tn, tn)
        return pltpu.make_async_copy(w_hbm[a].at[expert, :, pl.ds(col, tn)], stage[a].at[s], sem.at[a, s])

    @pl.when(first_ref[w] == 1)
    def _():
        s = slot_ref[w]

        @pl.when(w == 0)
        def _():
            for a in range(n_acc):
                fetch(a, e_ref[0], jw_ref[0], 0).start()

        @pl.when(hn_ref[w] == 1)
        def _():
            for a in range(n_acc):
                fetch(a, ne_ref[w], nj_ref[w], 1 - s).start()

        for a in range(n_acc):
            fetch(a, e_ref[w], jw_ref[w], s).wait()
            wbf[a][...] = stage[a][s].astype(BF16)

    tm = x_ref.shape[0]
    for nsub in range(1, tm // MOE_SUB + 1):
        @pl.when((act_ref[w] == 1) & (nv_ref[w] == nsub))
        def _():
            rows = nsub * MOE_SUB
            x = x_ref[0:rows, :].astype(BF16)
            out[0:rows, :] = epilogue([_dot(x, wbf[a][...]) for a in range(n_acc)], ()).astype(out.dtype)
            if rows < tm:
                out[rows:, :] = jnp.zeros((tm - rows, out.shape[1]), out.dtype)

    @pl.when(act_ref[w] == 0)
    def _():
        out[...] = jnp.zeros(out.shape, out.dtype)


def _gmm(xs, weights, tables, *, tm, tn, out_dtype, epilogue, vmem_mb, name):
    p, kdim = xs.shape
    n_out = weights[0].shape[2]
    n_items = tables[0].shape[0]
    n_acc = len(weights)
    grid_spec = pltpu.PrefetchScalarGridSpec(
        num_scalar_prefetch=len(tables),
        grid=(n_items,),
        in_specs=[pl.BlockSpec((tm, kdim), lambda w, rt, *_: (rt[w], 0))]
        + [pl.BlockSpec(memory_space=pl.ANY) for _ in weights],
        out_specs=pl.BlockSpec((tm, tn), lambda w, rt, e, jo, *_: (rt[w], jo[w])),
        scratch_shapes=[pltpu.VMEM((2, kdim, tn), weights[0].dtype) for _ in weights]
        + [pltpu.SemaphoreType.DMA((n_acc, 2))]
        + [pltpu.VMEM((kdim, tn), BF16) for _ in weights],
    )
    return pl.pallas_call(
        functools.partial(_gmm_body, n_acc=n_acc, tn=tn, epilogue=epilogue),
        grid_spec=grid_spec,
        out_shape=jax.ShapeDtypeStruct((p, n_out), out_dtype),
        compiler_params=_cp(("arbitrary",), vmem_mb),
        name=name,
    )(*tables, xs, *weights)


def _work_tables(counts, n_col_tiles, tm, n_tiles):
    tiles_e = (counts + tm - 1) // tm
    tile_end = jnp.cumsum(tiles_e)
    tile_start = tile_end - tiles_e
    total_tiles = tile_end[-1]
    n_items = n_col_tiles * n_tiles
    item_end = n_col_tiles * tile_end
    w = jnp.arange(n_items, dtype=jnp.int32)
    n_active = n_col_tiles * total_tiles
    active = w < n_active
    wc = jnp.minimum(w, n_active - 1)
    e = jnp.sum((wc[:, None] >= item_end[None, :]).astype(jnp.int32), axis=1)
    e = jnp.minimum(e, N_EXPERTS - 1)
    te = jnp.maximum(tiles_e[e], 1)
    local = wc - n_col_tiles * tile_start[e]
    j = local // te
    q = local % te
    rt = tile_start[e] + q
    first = (q == 0) & active
    slot = (jnp.cumsum(first.astype(jnp.int32)) - 1) % 2
    nxt = lax.cummin(jnp.where(first, w, n_items), axis=0, reverse=True)
    nxt = jnp.concatenate([nxt[1:], jnp.full((1,), n_items, nxt.dtype)])
    has_next = nxt < n_items
    nxt = jnp.minimum(nxt, n_items - 1)
    spare = jnp.maximum(w - n_active, 0)
    rt = jnp.where(active, rt, total_tiles + spare // n_col_tiles)
    j_out = jnp.where(active, j, spare % n_col_tiles)
    rows_valid = jnp.clip(counts[e] - q * tm, 1, tm)
    n_sub = (rows_valid + MOE_SUB - 1) // MOE_SUB
    i32 = lambda a: a.astype(jnp.int32)
    return (i32(rt), i32(e), i32(j_out), i32(j), i32(first), i32(active), i32(slot), i32(e[nxt]), i32(j[nxt]),
            i32(has_next), i32(n_sub))


def _combine_body(pos_ref, x_ref, info_ref, g_ref, y_hbm, o_ref, buf, sem, *, tm):
    i = pl.program_id(0)
    n = pl.num_programs(0)

    def gather(tile, slot):
        def issue(r, c):
            for k in range(TOP_K):
                _row_copy(y_hbm, pos_ref[(tile * tm + r) * TOP_K + k], buf.at[slot, k], r,
                          sem.at[slot]).start(priority=k % DMA_QUEUES)
            return c

        lax.fori_loop(0, tm, issue, 0, unroll=SCALAR_UNROLL)

    @pl.when(i == 0)
    def _():
        gather(0, 0)

    for slot in range(2):
        @pl.when((i + 1 < n) & ((i + 1) % 2 == slot))
        def _():
            gather(i + 1, slot)

    for slot in range(2):
        @pl.when(i % 2 == slot)
        def _():
            for k in range(TOP_K):
                _wait_rows(y_hbm, buf.at[slot, k], sem.at[slot], tm)
            info = info_ref[...]
            acc = x_ref[...] + info[:, 4:5] * buf[slot, 0] + info[:, 5:6] * buf[slot, 1]
            y = acc * lax.rsqrt(jnp.mean(acc * acc, axis=-1, keepdims=True) + NORM_EPS)
            o_ref[...] = y * g_ref[...]


def _combine(x, info, pos_flat, y_sorted, final_norm, *, tm=128):
    t, d = x.shape
    grid_spec = pltpu.PrefetchScalarGridSpec(
        num_scalar_prefetch=1,
        grid=(t // tm,),
        in_specs=[pl.BlockSpec((tm, d), lambda i, pos: (i, 0)),
                  pl.BlockSpec((tm, LANES), lambda i, pos: (i, 0)),
                  pl.BlockSpec((1, d), lambda i, pos: (0, 0)),
                  pl.BlockSpec(memory_space=pl.ANY)],
        out_specs=pl.BlockSpec((tm, d), lambda i, pos: (i, 0)),
        scratch_shapes=[pltpu.VMEM((2, TOP_K, tm, d), F32), pltpu.SemaphoreType.DMA((2,))],
    )
    return pl.pallas_call(
        functools.partial(_combine_body, tm=tm),
        grid_spec=grid_spec,
        out_shape=jax.ShapeDtypeStruct((t, d), F32),
        compiler_params=_cp(("arbitrary",), 40),
        name="moe_combine",
    )(pos_flat, x, info, final_norm.reshape(1, d).astype(F32), y_sorted)


def _even_mixers(x, tables, w_in, attn_norm, q_norm, w_uq, kv_norm, w_ukv, gate_bias, head_norm, batch, seq):
    t = x.shape[0]
    c_tab, s1_tab, s2_tab = tables
    n_ml_main = 2 * ML_HEADS * ML_QK + 2 * ML_HEADS * ML_V
    gate_lo = MLA_IN + n_ml_main
    w_t = w_in.T
    w_a_t = jnp.concatenate(
        [w_t[:MLA_IN], jnp.zeros((LANES - MLA_ROPE, D_MODEL), w_t.dtype),
         w_t[gate_lo:], jnp.zeros((LANES - 2 * ML_HEADS, D_MODEL), w_t.dtype)], axis=0)
    n_a = w_a_t.shape[0]
    w_q = jnp.pad(w_uq.reshape(MLA_Q_LORA, MLA_HEADS, MLA_NOPE + MLA_ROPE),
                  ((0, 0), (0, 0), (0, MLA_QK_PAD - MLA_NOPE - MLA_ROPE))).reshape(MLA_Q_LORA, -1).astype(BF16)
    w_kv3 = w_ukv.reshape(MLA_KV_LORA, MLA_HEADS, MLA_NOPE + MLA_V)
    w_k = jnp.pad(w_kv3[:, :, :MLA_NOPE], ((0, 0), (0, 0), (0, MLA_QK_PAD - MLA_NOPE))).reshape(MLA_KV_LORA, -1).astype(BF16)
    w_v = w_kv3[:, :, MLA_NOPE:].reshape(MLA_KV_LORA, -1).astype(BF16)

    xn = _rmsnorm(x, attn_norm, width=D_MODEL)
    u_a = _matmul_ws([xn], [[(w_a_t, 0)]], n_out=n_a, tm=1024, tn=256, out_dtype=F32, epilogue=_epi_plain,
                     transposed=True, name="even_in_a")
    u_b = _matmul_ws([xn], [[(w_t, 0)]], n_out=n_ml_main, tm=1024, tn=512, out_dtype=BF16, epilogue=_epi_plain,
                     col_offset=MLA_IN, transposed=True, name="even_in_b")

    cqn = _rmsnorm(u_a, q_norm, width=MLA_Q_LORA, col_block=0)
    ckvn = _rmsnorm(u_a, kv_norm, width=MLA_KV_LORA, col_block=MLA_Q_LORA // MLA_KV_LORA)
    tm_p = min(1024, t)
    tab_specs = [(tab, (tm_p, LANES), lambda i, j: (i, 0)) for tab in (c_tab, s1_tab, s2_tab)]
    n_qk = MLA_HEADS * MLA_QK_PAD
    q_full = _matmul_ws([cqn], [[(w_q, 0)]], n_out=n_qk, tm=tm_p, tn=1024, out_dtype=BF16, epilogue=_epi_mla_q,
                        extras=tab_specs, name="mla_q")
    kr_spec = (u_a, (tm_p, LANES), lambda i, j: (i, (MLA_Q_LORA + MLA_KV_LORA) // LANES))
    k_full = _matmul_ws([ckvn], [[(w_k, 0)]], n_out=n_qk, tm=tm_p, tn=1024, out_dtype=BF16, epilogue=_epi_mla_k,
                        extras=[kr_spec] + tab_specs, name="mla_k")
    v = _matmul_ws([ckvn], [[(w_v, 0)]], n_out=MLA_HEADS * MLA_V, tm=tm_p, tn=1024, out_dtype=BF16,
                   epilogue=_epi_plain, name="mla_v")
    y_a = _mla_attention(q_full, k_full, v, batch, seq)

    bias_pad = jnp.concatenate([gate_bias.astype(F32), jnp.zeros((LANES - 2 * ML_HEADS,), F32)]).reshape(1, LANES)
    y_b = _mlstm(u_b, u_a, (MLA_IN + LANES - MLA_ROPE) // LANES, bias_pad, head_norm, batch, seq)
    return y_a, y_b


def _even_layer(x, tables, w_in, attn_norm, q_norm, w_uq, kv_norm, w_ukv, gate_bias, head_norm, w_out,
                ffn_norm, w_gate, w_up, w_down, batch, seq):
    y_a, y_b = _even_mixers(x, tables, w_in, attn_norm, q_norm, w_uq, kv_norm, w_ukv, gate_bias, head_norm,
                            batch, seq)
    tm = min(1024, x.shape[0])
    res_spec = lambda arr, tn: [(arr, (tm, tn), lambda i, j: (i, j))]
    x1 = _matmul_ws([y_a, y_b], [[(w_out, 0), (w_out, 1)]], n_out=D_MODEL, tm=tm, tn=512, out_dtype=F32,
                    epilogue=_epi_residual, extras=res_spec(x, 512), name="even_out")
    hn = _rmsnorm(x1, ffn_norm, width=D_MODEL)
    a = _matmul_ws([hn], [[(w_gate, 0)], [(w_up, 0)]], n_out=D_FF, tm=min(2048, x.shape[0]), tn=256, out_dtype=BF16,
                   epilogue=_epi_swiglu, vmem_mb=58, name="ffn_gate_up")
    tm_d = min(2048, x.shape[0])
    x2 = _matmul([a], [[(w_down, 0)]], n_out=D_MODEL, tm=tm_d, tn=1024, tk=512, out_dtype=F32, epilogue=_epi_residual,
                 extras=[(x1, (tm_d, 1024), lambda i, j: (i, j))], vmem_mb=58, name="ffn_down")
    return x2


def _odd_layer(x, w_in, attn_norm, conv_w, w_out, ffn_norm, w_router, b_router, w_gate_e, w_up_e, w_down_e,
               final_norm, batch, seq):
    t = x.shape[0]
    xn = _rmsnorm(x, attn_norm, width=D_MODEL)
    n_in = w_in.shape[1]
    tm = min(1024, t)
    u = _matmul_ws([xn], [[(w_in, 0)]], n_out=n_in, tm=tm, tn=512, out_dtype=BF16, epilogue=_epi_plain,
                   name="odd_in")
    y_c = _short_conv(u, conv_w, batch, seq)
    y_d = _sb_attention(u, 3 * SC_WIDTH // LANES, batch, seq)
    x1 = _matmul_ws([y_c, y_d], [[(w_out, 0), (w_out, 1)]], n_out=D_MODEL, tm=tm, tn=512, out_dtype=F32,
                    epilogue=_epi_residual, extras=[(x, (tm, 512), lambda i, j: (i, j))], name="odd_out")
    return _moe(x1, ffn_norm, w_router, b_router, w_gate_e, w_up_e, w_down_e, final_norm)


def _moe(x1, ffn_norm, w_router, b_router, w_gate_e, w_up_e, w_down_e, final_norm):
    t = x1.shape[0]
    h, info, cnt = _route(x1, ffn_norm, w_router, b_router)
    tm = MOE_TM
    n_tiles = (t * TOP_K) // tm + N_EXPERTS
    counts = cnt[0, :N_EXPERTS].astype(jnp.int32)
    tiles_e = (counts + tm - 1) // tm
    offs = (jnp.cumsum(tiles_e) - tiles_e) * tm
    idx = info[:, 0:TOP_K].astype(jnp.int32)
    rank = info[:, TOP_K:2 * TOP_K].astype(jnp.int32)
    pos_flat = (offs[idx] + rank).reshape(-1)
    n_active = jnp.sum(tiles_e).astype(jnp.int32).reshape(1)

    inv = _invperm(pos_flat, n_tiles * tm)
    xs = _dispatch(h, inv, n_active, n_tiles, tm=tm)
    tn_gu, tn_d = 512, 1024
    tab_gu = _work_tables(counts, D_FF_EXPERT // tn_gu, tm, n_tiles)
    a_s = _gmm(xs, [w_gate_e, w_up_e], tab_gu, tm=tm, tn=tn_gu, out_dtype=BF16, epilogue=_epi_swiglu,
               vmem_mb=58, name="moe_gate_up")
    tab_d = _work_tables(counts, D_MODEL // tn_d, tm, n_tiles)
    y_s = _gmm(a_s, [w_down_e], tab_d, tm=tm, tn=tn_d, out_dtype=F32, epilogue=_epi_plain,
               vmem_mb=58, name="moe_down")
    return _combine(x1, info, pos_flat, y_s, final_norm)


def kernel(x, positions, even_attn_norm, even_w_in, even_q_norm, even_w_uq, even_kv_norm, even_w_ukv, even_ml_gate_bias, even_ml_head_norm, even_w_out, even_ffn_norm, even_w_gate, even_w_up, even_w_down, odd_attn_norm, odd_w_in, odd_conv_w, odd_w_out, odd_ffn_norm, odd_w_router, odd_b_router, odd_w_gate_e, odd_w_up_e, odd_w_down_e, final_norm):
    batch, seq, d = x.shape
    assert even_w_in.shape[0] == 1 and odd_w_in.shape[0] == 1, "kernel is written for one even and one odd layer"
    xf = x.reshape(batch * seq, d)
    tables = _rope_tables(positions)
    xf = _even_layer(xf, tables, even_w_in[0], even_attn_norm[0], even_q_norm[0], even_w_uq[0], even_kv_norm[0],
                     even_w_ukv[0], even_ml_gate_bias[0], even_ml_head_norm[0], even_w_out[0], even_ffn_norm[0],
                     even_w_gate[0], even_w_up[0], even_w_down[0], batch, seq)
    out = _odd_layer(xf, odd_w_in[0], odd_attn_norm[0], odd_conv_w[0], odd_w_out[0], odd_ffn_norm[0],
                     odd_w_router[0], odd_b_router[0], odd_w_gate_e[0], odd_w_up_e[0], odd_w_down_e[0],
                     final_norm, batch, seq)
    return out.reshape(batch, seq, d)
```

```python
import functools

import jax
import jax.numpy as jnp
from jax import lax
from jax.experimental import pallas as pl
from jax.experimental.pallas import tpu as pltpu

F32 = jnp.float32
BF16 = jnp.bfloat16

D_MODEL = 4096
NORM_EPS = 1e-6

MLA_HEADS = 16
MLA_Q_LORA = 1024
MLA_KV_LORA = 512
MLA_NOPE = 128
MLA_ROPE = 64
MLA_V = 128
ROPE_THETA = 10000.0
MLA_IN = MLA_Q_LORA + MLA_KV_LORA + MLA_ROPE
MLA_QK_PAD = 256

ML_HEADS = 4
ML_QK = 256
ML_V = 512
ML_CHUNK = 128
ML_VA = ML_V + 128

SC_WIDTH = 2048
SC_KERNEL = 3

SB_HEADS = 16
SB_HEAD_DIM = 128

D_FF = 11008
N_EXPERTS = 8
TOP_K = 2
D_FF_EXPERT = 4096

LANES = 128
SUBLANES_BF16 = 16
MOE_TM = 512
MOE_SUB = 256
GATHER_SLOTS = 3
DMA_QUEUES = 2
LOG2_E = 1.4426950408889634
SCALAR_UNROLL = 8


def _cp(sem, vmem_mb):
    return pltpu.CompilerParams(dimension_semantics=sem, vmem_limit_bytes=vmem_mb * 1024 * 1024)


def _dot(a, b):
    return jnp.dot(a, b, preferred_element_type=F32)


def _dot_nt(a, b):
    return lax.dot_general(a, b, (((1,), (1,)), ((), ())), preferred_element_type=F32)


def _log_sigmoid(x):
    return jnp.minimum(x, 0.0) - jnp.log1p(jnp.exp(-jnp.abs(x)))


def _split_dot(tri, x, pieces):
    acc = None
    r = x
    for p in range(pieces):
        hi = r.astype(BF16)
        part = _dot(tri, hi)
        acc = part if acc is None else acc + part
        if p + 1 < pieces:
            r = r - hi.astype(F32)
    return acc


def _rmsnorm_body(x_ref, g_ref, o_ref):
    x = x_ref[...].astype(F32)
    y = x * lax.rsqrt(jnp.mean(x * x, axis=-1, keepdims=True) + NORM_EPS)
    o_ref[...] = (y * g_ref[...]).astype(o_ref.dtype)


def _rmsnorm(x, g, *, width, col_block=0, out_dtype=BF16, tm=256):
    m = x.shape[0]
    return pl.pallas_call(
        _rmsnorm_body,
        grid=(m // tm,),
        in_specs=[pl.BlockSpec((tm, width), lambda i: (i, col_block)),
                  pl.BlockSpec((1, width), lambda i: (0, 0))],
        out_specs=pl.BlockSpec((tm, width), lambda i: (i, 0)),
        out_shape=jax.ShapeDtypeStruct((m, width), out_dtype),
        compiler_params=_cp(("parallel",), 40),
        name="rmsnorm",
    )(x, g.reshape(1, width).astype(F32))


def _mm_body(*refs, n_lhs, n_acc, n_extra, nk, k_valid_last, epilogue):
    lhs = refs[:n_lhs]
    ws = refs[n_lhs:n_lhs + n_acc * n_lhs]
    extras = refs[n_lhs + n_acc * n_lhs:n_lhs + n_acc * n_lhs + n_extra]
    out = refs[n_lhs + n_acc * n_lhs + n_extra]
    accs = refs[n_lhs + n_acc * n_lhs + n_extra + 1:]
    k = pl.program_id(2)

    def partials(mask_tail):
        parts = []
        for a in range(n_acc):
            s = None
            for l in range(n_lhs):
                x = lhs[l][...]
                w = ws[a * n_lhs + l][...]
                if mask_tail:
                    xc = lax.broadcasted_iota(jnp.int32, x.shape, 1)
                    x = jnp.where(xc < k_valid_last, x, jnp.zeros_like(x))
                    wr = lax.broadcasted_iota(jnp.int32, w.shape, 0)
                    w = jnp.where(wr < k_valid_last, w, jnp.zeros_like(w))
                p = _dot(x.astype(BF16), w.astype(BF16))
                s = p if s is None else s + p
            parts.append(s)
        return parts

    if nk == 1:
        out[...] = epilogue(partials(False), extras).astype(out.dtype)
        return

    ragged = k_valid_last is not None

    @pl.when(k == 0)
    def _():
        for a, p in enumerate(partials(False)):
            accs[a][...] = p

    @pl.when((k > 0) & (k < nk - 1) if ragged else (k > 0))
    def _():
        for a, p in enumerate(partials(False)):
            accs[a][...] += p

    if ragged:
        @pl.when(k == nk - 1)
        def _():
            for a, p in enumerate(partials(True)):
                accs[a][...] += p

    @pl.when(k == nk - 1)
    def _():
        out[...] = epilogue([acc[...] for acc in accs], extras).astype(out.dtype)


def _matmul(lhs, weights, *, n_out, tm, tn, tk, out_dtype, epilogue, extras=(), vmem_mb=48, name="matmul"):
    m = lhs[0].shape[0]
    kdim = lhs[0].shape[1]
    tm = min(tm, m)
    nk = pl.cdiv(kdim, tk)
    k_valid_last = None if kdim % tk == 0 else kdim - (nk - 1) * tk
    n_lhs, n_acc = len(lhs), len(weights)
    grid = (m // tm, n_out // tn, nk)
    ij = lambda g0, g1: (g0, g1)

    in_specs, args = [], []
    for x in lhs:
        in_specs.append(pl.BlockSpec((tm, tk), lambda g0, g1, k: (ij(g0, g1)[0], k)))
        args.append(x)
    for wl in weights:
        for (w, rb) in wl:
            in_specs.append(pl.BlockSpec((tk, tn), lambda g0, g1, k, rb=rb: (rb * nk + k, ij(g0, g1)[1])))
            args.append(w)
    for (arr, bshape, imap) in extras:
        in_specs.append(pl.BlockSpec(bshape, lambda g0, g1, k, imap=imap: imap(*ij(g0, g1))))
        args.append(arr)
    scratch = [pltpu.VMEM((tm, tn), F32) for _ in range(n_acc)] if nk > 1 else []
    body = functools.partial(_mm_body, n_lhs=n_lhs, n_acc=n_acc, n_extra=len(extras), nk=nk,
                             k_valid_last=k_valid_last, epilogue=epilogue)
    return pl.pallas_call(
        body,
        grid=grid,
        in_specs=in_specs,
        out_specs=pl.BlockSpec((tm, tn), lambda g0, g1, k: ij(g0, g1)),
        out_shape=jax.ShapeDtypeStruct((m, n_out), out_dtype),
        scratch_shapes=scratch,
        compiler_params=_cp(("parallel", "parallel", "arbitrary"), vmem_mb),
        name=name,
    )(*args)


def _mm_ws_body(*refs, n_lhs, n_acc, n_extra, tn, kdims, row_blocks, col_offset, transposed, cast, epilogue):
    n_w = n_acc * n_lhs
    lhs = refs[:n_lhs]
    w_hbm = refs[n_lhs:n_lhs + n_w]
    extras = refs[n_lhs + n_w:n_lhs + n_w + n_extra]
    out = refs[n_lhs + n_w + n_extra]
    scratch = refs[n_lhs + n_w + n_extra + 1:]
    stage, sem, wbf = scratch[:n_w], scratch[n_w], scratch[n_w + 1:]
    j = pl.program_id(0)
    i = pl.program_id(1)
    nj = pl.num_programs(0)
    slot = j % 2

    def fetch(widx, jj, s):
        kd = kdims[widx % n_lhs]
        if transposed:
            col = pl.multiple_of(col_offset + jj * tn, SUBLANES_BF16)
            src = w_hbm[widx].at[pl.ds(col, tn), pl.ds(row_blocks[widx] * kd, kd)]
        else:
            col = pl.multiple_of(col_offset + jj * tn, LANES)
            src = w_hbm[widx].at[pl.ds(row_blocks[widx] * kd, kd), pl.ds(col, tn)]
        return pltpu.make_async_copy(src, stage[widx].at[s], sem.at[widx, s])

    @pl.when(i == 0)
    def _():
        @pl.when(j == 0)
        def _():
            for widx in range(n_w):
                fetch(widx, 0, 0).start()

        @pl.when(j + 1 < nj)
        def _():
            for widx in range(n_w):
                fetch(widx, j + 1, 1 - slot).start()

        for widx in range(n_w):
            fetch(widx, j, slot).wait()
            if cast:
                wbf[widx][...] = stage[widx][slot].astype(BF16)

    parts = []
    for a in range(n_acc):
        s = None
        for l in range(n_lhs):
            widx = a * n_lhs + l
            w = wbf[widx][...] if cast else stage[widx][slot]
            x = lhs[l][...].astype(BF16)
            p = _dot_nt(x, w) if transposed else _dot(x, w)
            s = p if s is None else s + p
        parts.append(s)
    out[...] = epilogue(parts, extras).astype(out.dtype)


def _matmul_ws(lhs, weights, *, n_out, tm, tn, out_dtype, epilogue, extras=(), col_offset=0, transposed=False,
               vmem_mb=48, name="matmul_ws"):
    assert tn % LANES == 0 and col_offset % (SUBLANES_BF16 if transposed else LANES) == 0
    m = lhs[0].shape[0]
    tm = min(tm, m)
    n_lhs, n_acc = len(lhs), len(weights)
    kdims = tuple(x.shape[1] for x in lhs)
    flat_w = [w for wl in weights for (w, _) in wl]
    row_blocks = tuple(rb for wl in weights for (_, rb) in wl)
    wdtype = flat_w[0].dtype
    cast = wdtype != BF16
    in_specs = [pl.BlockSpec((tm, kd), lambda j, i: (i, 0)) for kd in kdims]
    in_specs += [pl.BlockSpec(memory_space=pl.ANY) for _ in flat_w]
    in_specs += [pl.BlockSpec(bshape, lambda j, i, imap=imap: imap(i, j)) for (_, bshape, imap) in extras]
    w_kdims = [kdims[widx % n_lhs] for widx in range(len(flat_w))]
    tile = (lambda kd: (tn, kd)) if transposed else (lambda kd: (kd, tn))
    scratch = [pltpu.VMEM((2,) + tile(kd), wdtype) for kd in w_kdims]
    scratch.append(pltpu.SemaphoreType.DMA((len(flat_w), 2)))
    if cast:
        scratch += [pltpu.VMEM(tile(kd), BF16) for kd in w_kdims]
    body = functools.partial(_mm_ws_body, n_lhs=n_lhs, n_acc=n_acc, n_extra=len(extras), tn=tn, kdims=kdims,
                             row_blocks=row_blocks, col_offset=col_offset, transposed=transposed, cast=cast,
                             epilogue=epilogue)
    return pl.pallas_call(
        body,
        grid=(n_out // tn, m // tm),
        in_specs=in_specs,
        out_specs=pl.BlockSpec((tm, tn), lambda j, i: (i, j)),
        out_shape=jax.ShapeDtypeStruct((m, n_out), out_dtype),
        scratch_shapes=scratch,
        compiler_params=_cp(("arbitrary", "arbitrary"), vmem_mb),
        name=name,
    )(*lhs, *flat_w, *[arr for (arr, _, _) in extras])


def _epi_plain(parts, extras):
    return parts[0]


def _epi_residual(parts, extras):
    return parts[0] + extras[0][...]


def _epi_swiglu(parts, extras):
    g, u = parts
    return g * jax.nn.sigmoid(g) * u


def _rope_tables_body(pos_ref, invf_ref, c_ref, s1_ref, s2_ref):
    ang = pos_ref[...] * invf_ref[...]
    lane = lax.broadcasted_iota(jnp.int32, ang.shape, 1)
    cos = jnp.cos(ang)
    sin = jnp.sin(ang)
    half = MLA_ROPE // 2
    c_ref[...] = jnp.where(lane < MLA_ROPE, cos, 0.0)
    s1_ref[...] = jnp.where(lane < half, -sin, 0.0)
    s2_ref[...] = jnp.where((lane >= half) & (lane < MLA_ROPE), sin, 0.0)


def _rope_tables(positions):
    t = positions.size
    half = MLA_ROPE // 2
    inv_freq = 1.0 / (ROPE_THETA ** (jnp.arange(0, MLA_ROPE, 2, dtype=F32) / MLA_ROPE))
    invf = jnp.concatenate([inv_freq, inv_freq, jnp.zeros((LANES - 2 * half,), F32)]).reshape(1, LANES)
    pos = jnp.broadcast_to(positions.reshape(t, 1).astype(F32), (t, LANES))
    tm = min(512, t)
    spec = pl.BlockSpec((tm, LANES), lambda i: (i, 0))
    sds = jax.ShapeDtypeStruct((t, LANES), F32)
    return pl.pallas_call(
        _rope_tables_body,
        grid=(t // tm,),
        in_specs=[spec, pl.BlockSpec((1, LANES), lambda i: (0, 0))],
        out_specs=[spec, spec, spec],
        out_shape=[sds, sds, sds],
        compiler_params=_cp(("parallel",), 32),
        name="rope_tables",
    )(pos, invf)


def _rope_lanes(t, c, s1, s2):
    return t * c + pltpu.roll(t, LANES - MLA_ROPE // 2, axis=1) * s1 + pltpu.roll(t, MLA_ROPE // 2, axis=1) * s2


def _epi_mla_q(parts, extras):
    acc = parts[0]
    c, s1, s2 = extras[0][...], extras[1][...], extras[2][...]
    scale = (MLA_NOPE + MLA_ROPE) ** -0.5
    pieces = []
    for h in range(acc.shape[1] // MLA_QK_PAD):
        base = h * MLA_QK_PAD
        pieces.append(acc[:, base:base + MLA_NOPE] * scale)
        pieces.append(_rope_lanes(acc[:, base + MLA_NOPE:base + MLA_QK_PAD], c, s1, s2) * scale)
    return jnp.concatenate(pieces, axis=1)


def _epi_mla_k(parts, extras):
    acc = parts[0]
    kr = _rope_lanes(extras[0][...], extras[1][...], extras[2][...], extras[3][...])
    pieces = []
    for h in range(acc.shape[1] // MLA_QK_PAD):
        base = h * MLA_QK_PAD
        pieces.append(acc[:, base:base + MLA_NOPE])
        pieces.append(acc[:, base + MLA_NOPE:base + MLA_QK_PAD] + kr)
    return jnp.concatenate(pieces, axis=1)


def _mla_attn_body(q_ref, k_ref, v_ref, o_ref, *, tq, seq):
    row = lax.broadcasted_iota(jnp.int32, (tq, tq), 0)
    col = lax.broadcasted_iota(jnp.int32, (tq, tq), 1)
    causal = col <= row
    for qi in range(seq // tq):
        n = (qi + 1) * tq
        q = q_ref[qi * tq:(qi + 1) * tq, :]
        s = _dot_nt(q, k_ref[0:n, :])
        diag = jnp.where(causal, s[:, n - tq:], -jnp.inf)
        s = diag if qi == 0 else jnp.concatenate([s[:, :n - tq], diag], axis=1)
        m = jnp.max(s, axis=1, keepdims=True)
        p = jnp.exp(s - m)
        l = jnp.sum(p, axis=1, keepdims=True)
        o = _dot(p.astype(BF16), v_ref[0:n, :]) / l
        o_ref[qi * tq:(qi + 1) * tq, :] = o.astype(o_ref.dtype)


def _mla_attention(q, k, v, batch, seq, *, tq=256):
    t = q.shape[0]
    tq = min(tq, seq)
    body = functools.partial(_mla_attn_body, tq=tq, seq=seq)
    return pl.pallas_call(
        body,
        grid=(batch, MLA_HEADS),
        in_specs=[pl.BlockSpec((seq, MLA_QK_PAD), lambda b, h: (b, h)),
                  pl.BlockSpec((seq, MLA_QK_PAD), lambda b, h: (b, h)),
                  pl.BlockSpec((seq, MLA_V), lambda b, h: (b, h))],
        out_specs=pl.BlockSpec((seq, MLA_V), lambda b, h: (b, h)),
        out_shape=jax.ShapeDtypeStruct((t, MLA_HEADS * MLA_V), BF16),
        compiler_params=_cp(("parallel", "parallel"), 48),
        name="mla_attention",
    )(q, k, v)


def _mlstm_body(q_ref, k_ref, v_ref, o_ref, g_ref, bias_ref, hn_ref, y_ref, ct_ref, m_ref):
    L = ML_CHUNK
    c = pl.program_id(1)

    @pl.when(c == 0)
    def _():
        ct_ref[...] = jnp.zeros(ct_ref.shape, F32)
        m_ref[...] = jnp.zeros(m_ref.shape, F32)

    g = g_ref[...] + bias_ref[...]
    logf = _log_sigmoid(g)
    row = lax.broadcasted_iota(jnp.int32, (L, L), 0)
    col = lax.broadcasted_iota(jnp.int32, (L, L), 1)
    causal = col <= row
    tri = jnp.where(causal, 1.0, 0.0).astype(BF16)
    cum = _split_dot(tri, logf, 3)
    lane = lax.broadcasted_iota(jnp.int32, (L, LANES), 1)
    colq = jnp.where(lane < ML_HEADS, g, cum)
    rowq = colq.T
    ones_col = jnp.where(lane == 0, 1.0, 0.0).astype(BF16)
    scale = ML_QK ** -0.5

    for h in range(ML_HEADS):
        i_c = colq[:, h:h + 1]
        cf_c = colq[:, ML_HEADS + h:ML_HEADS + h + 1]
        i_r = rowq[h:h + 1, :]
        cf_r = rowq[ML_HEADS + h:ML_HEADS + h + 1, :]
        m_prev = m_ref[h][:, 0:1]
        d = jnp.where(causal, cf_c - cf_r + i_r, -jnp.inf)
        m_inter = cf_c + m_prev
        m_t = jnp.maximum(m_inter, jnp.max(d, axis=1, keepdims=True))
        inter = jnp.exp(m_inter - m_t)
        p = jnp.exp(d - m_t)
        qh = q_ref[:, h * ML_QK:(h + 1) * ML_QK]
        kf = k_ref[:, h * ML_QK:(h + 1) * ML_QK].astype(F32) * scale
        kh = kf.astype(BF16)
        s = _dot_nt(qh, kh) * p
        va = jnp.concatenate([v_ref[:, h * ML_V:(h + 1) * ML_V], ones_col], axis=1)
        ct = ct_ref[h]
        na = _dot(s.astype(BF16), va) + inter * _dot(qh, ct.astype(BF16))
        num = na[:, :ML_V]
        den = na[:, ML_V:ML_V + 1]
        hh = num / jnp.maximum(jnp.abs(den), jnp.exp(-m_t))

        f_tot = cf_c[L - 1:L, :]
        w_log = f_tot - cf_c + i_c
        m_new = jnp.maximum(f_tot + m_prev, jnp.max(w_log, axis=0, keepdims=True))
        decay = jnp.exp(f_tot + m_prev - m_new)
        w = jnp.exp(w_log - m_new)
        wv = (va.astype(F32) * w).astype(BF16)
        ct_ref[h] = decay * ct + _dot(kf.T.astype(BF16), wv)
        m_ref[h] = jnp.broadcast_to(m_new, (1, LANES))

        hn = hh * lax.rsqrt(jnp.mean(hh * hh, axis=1, keepdims=True) + NORM_EPS) * hn_ref[:, h * ML_V:(h + 1) * ML_V]
        og = o_ref[:, h * ML_V:(h + 1) * ML_V].astype(F32)
        y_ref[:, h * ML_V:(h + 1) * ML_V] = (jax.nn.sigmoid(og) * hn).astype(y_ref.dtype)


def _mlstm(u_b, gates_src, gate_col_block, bias_pad, head_norm, batch, seq):
    t = u_b.shape[0]
    nc = seq // ML_CHUNK
    nqk = ML_HEADS * ML_QK
    nv = ML_HEADS * ML_V
    rows = lambda b, c: b * nc + c
    return pl.pallas_call(
        _mlstm_body,
        grid=(batch, nc),
        in_specs=[pl.BlockSpec((ML_CHUNK, nqk), lambda b, c: (rows(b, c), 0)),
                  pl.BlockSpec((ML_CHUNK, nqk), lambda b, c: (rows(b, c), 1)),
                  pl.BlockSpec((ML_CHUNK, nv), lambda b, c: (rows(b, c), 1)),
                  pl.BlockSpec((ML_CHUNK, nv), lambda b, c: (rows(b, c), 2)),
                  pl.BlockSpec((ML_CHUNK, LANES), lambda b, c: (rows(b, c), gate_col_block)),
                  pl.BlockSpec((1, LANES), lambda b, c: (0, 0)),
                  pl.BlockSpec((1, nv), lambda b, c: (0, 0))],
        out_specs=pl.BlockSpec((ML_CHUNK, nv), lambda b, c: (rows(b, c), 0)),
        out_shape=jax.ShapeDtypeStruct((t, nv), BF16),
        scratch_shapes=[pltpu.VMEM((ML_HEADS, ML_QK, ML_VA), F32), pltpu.VMEM((ML_HEADS, 1, LANES), F32)],
        compiler_params=_cp(("parallel", "arbitrary"), 32),
        name="mlstm",
    )(u_b, u_b, u_b, u_b, gates_src, bias_pad, head_norm.reshape(1, nv).astype(F32))


def _conv_body(b_ref, c_ref, h_ref, cp_ref, hp_ref, w_ref, y_ref, z_sc, *, ts, halo):
    i = pl.program_id(1)
    z = c_ref[...].astype(F32) * h_ref[...].astype(F32)
    zp = cp_ref[...].astype(F32) * hp_ref[...].astype(F32)
    z_sc[0:halo, :] = jnp.where(i > 0, zp, 0.0)
    z_sc[halo:halo + ts, :] = z
    w = w_ref[...]
    y = w[2:3, :] * z + w[1:2, :] * z_sc[halo - 1:halo - 1 + ts, :] + w[0:1, :] * z_sc[halo - 2:halo - 2 + ts, :]
    y_ref[...] = (b_ref[...].astype(F32) * y).astype(y_ref.dtype)


def _short_conv(u, conv_w, batch, seq, *, ts=256, tw=1024):
    t = u.shape[0]
    halo = 16
    ns = seq // ts
    nw = SC_WIDTH // tw
    wpad = jnp.concatenate([conv_w.astype(F32), jnp.zeros((8 - SC_KERNEL, SC_WIDTH), F32)], axis=0)
    rows = lambda b, i: b * ns + i
    prev = lambda b, i: jnp.maximum((b * seq + i * ts) // halo - 1, 0)
    body = functools.partial(_conv_body, ts=ts, halo=halo)
    return pl.pallas_call(
        body,
        grid=(batch, ns, nw),
        in_specs=[pl.BlockSpec((ts, tw), lambda b, i, j: (rows(b, i), j)),
                  pl.BlockSpec((ts, tw), lambda b, i, j: (rows(b, i), nw + j)),
                  pl.BlockSpec((ts, tw), lambda b, i, j: (rows(b, i), 2 * nw + j)),
                  pl.BlockSpec((halo, tw), lambda b, i, j: (prev(b, i), nw + j)),
                  pl.BlockSpec((halo, tw), lambda b, i, j: (prev(b, i), 2 * nw + j)),
                  pl.BlockSpec((8, tw), lambda b, i, j: (0, j))],
        out_specs=pl.BlockSpec((ts, tw), lambda b, i, j: (rows(b, i), j)),
        out_shape=jax.ShapeDtypeStruct((t, SC_WIDTH), BF16),
        scratch_shapes=[pltpu.VMEM((halo + ts, tw), F32)],
        compiler_params=_cp(("parallel", "parallel", "parallel"), 32),
        name="short_conv",
    )(u, u, u, u, u, wpad)


def _sb_body(q_ref, k_ref, v_ref, o_ref, *, tq, seq):
    scale = SB_HEAD_DIM ** -0.5 * LOG2_E
    row = lax.broadcasted_iota(jnp.int32, (tq, tq), 0)
    col = lax.broadcasted_iota(jnp.int32, (tq, tq), 1)
    strict = col < row
    neg_upper_incl = jnp.where(row >= col, -1.0, 0.0).astype(BF16)
    for qi in range(seq // tq):
        nb = qi + 1
        q = q_ref[qi * tq:(qi + 1) * tq, :]
        z = _dot_nt(q, k_ref[0:nb * tq, :]) * scale
        sp = jnp.maximum(z, 0.0) + jnp.log2(1.0 + jnp.exp2(-jnp.abs(z)))
        later = jnp.zeros((tq, 1), F32)
        blocks = [None] * nb
        for j in range(nb - 1, -1, -1):
            zj = z[:, j * tq:(j + 1) * tq]
            spj = sp[:, j * tq:(j + 1) * tq]
            spm = jnp.where(strict, spj, 0.0) if j == nb - 1 else spj
            upto = _split_dot_rhs(spm, neg_upper_incl, 2)
            e = jnp.exp2(zj + upto + later)
            if j == nb - 1:
                e = jnp.where(strict, e, 0.0)
            blocks[j] = e.astype(BF16)
            later = later + upto[:, 0:1]
        a = blocks[0] if nb == 1 else jnp.concatenate(blocks, axis=1)
        o_ref[qi * tq:(qi + 1) * tq, :] = _dot(a, v_ref[0:nb * tq, :]).astype(o_ref.dtype)


def _split_dot_rhs(x, tri, pieces):
    acc = None
    r = x
    for p in range(pieces):
        hi = r.astype(BF16)
        part = _dot(hi, tri)
        acc = part if acc is None else acc + part
        if p + 1 < pieces:
            r = r - hi.astype(F32)
    return acc


def _sb_attention(u, col_base, batch, seq, *, tq=256):
    t = u.shape[0]
    tq = min(tq, seq)
    body = functools.partial(_sb_body, tq=tq, seq=seq)
    spec = lambda off: pl.BlockSpec((seq, SB_HEAD_DIM), lambda b, h: (b, col_base + off + h))
    return pl.pallas_call(
        body,
        grid=(batch, SB_HEADS),
        in_specs=[spec(0), spec(SB_HEADS), spec(2 * SB_HEADS)],
        out_specs=pl.BlockSpec((seq, SB_HEAD_DIM), lambda b, h: (b, h)),
        out_shape=jax.ShapeDtypeStruct((t, SB_HEADS * SB_HEAD_DIM), BF16),
        compiler_params=_cp(("parallel", "parallel"), 48),
        name="sb_attention",
    )(u, u, u)


def _route_body(x_ref, g_ref, wr_ref, br_ref, h_ref, info_ref, cnt_ref, carry):
    i = pl.program_id(0)

    @pl.when(i == 0)
    def _():
        carry[...] = jnp.zeros(carry.shape, F32)

    x = x_ref[...]
    h = x * lax.rsqrt(jnp.mean(x * x, axis=-1, keepdims=True) + NORM_EPS) * g_ref[...]
    h_ref[...] = h
    wr = wr_ref[...]
    h_hi = h.astype(BF16)
    h_lo = (h - h_hi.astype(F32)).astype(BF16)
    w_hi = wr.astype(BF16)
    w_lo = (wr - w_hi.astype(F32)).astype(BF16)
    logits = (_dot(h_hi, w_hi) + _dot(h_hi, w_lo)) + (_dot(h_lo, w_hi) + _dot(h_lo, w_lo)) + br_ref[...]
    tm = logits.shape[0]
    lane = lax.broadcasted_iota(jnp.int32, logits.shape, 1).astype(F32)
    lg = jnp.where(lane < N_EXPERTS, logits, -jnp.inf)
    m1 = jnp.max(lg, axis=1, keepdims=True)
    i1 = jnp.min(jnp.where(lg == m1, lane, float(LANES)), axis=1, keepdims=True)
    lg2 = jnp.where(lane == i1, -jnp.inf, lg)
    m2 = jnp.max(lg2, axis=1, keepdims=True)
    i2 = jnp.min(jnp.where(lg2 == m2, lane, float(LANES)), axis=1, keepdims=True)
    e = jnp.exp(m2 - m1)
    w1 = 1.0 / (1.0 + e)
    w2 = e / (1.0 + e)
    oh1 = lane == i1
    oh2 = lane == i2
    mask = jnp.where(oh1 | oh2, 1.0, 0.0)
    row = lax.broadcasted_iota(jnp.int32, (tm, tm), 0)
    col = lax.broadcasted_iota(jnp.int32, (tm, tm), 1)
    before = jnp.where(col < row, 1.0, 0.0).astype(BF16)
    rank_mat = _dot(before, mask.astype(BF16)) + carry[...]
    r1 = jnp.sum(jnp.where(oh1, rank_mat, 0.0), axis=1, keepdims=True)
    r2 = jnp.sum(jnp.where(oh2, rank_mat, 0.0), axis=1, keepdims=True)
    carry[...] += jnp.sum(mask, axis=0, keepdims=True)
    info = jnp.where(lane == 0, i1,
           jnp.where(lane == 1, i2,
           jnp.where(lane == 2, r1,
           jnp.where(lane == 3, r2,
           jnp.where(lane == 4, w1,
           jnp.where(lane == 5, w2, 0.0))))))
    info_ref[...] = info
    cnt_ref[...] = jnp.broadcast_to(carry[...], cnt_ref.shape)


def _route(x, g, w_router, b_router, *, tm=256):
    t, d = x.shape
    wr = jnp.concatenate([w_router.astype(F32), jnp.zeros((d, LANES - N_EXPERTS), F32)], axis=1)
    br = jnp.concatenate([b_router.astype(F32), jnp.zeros((LANES - N_EXPERTS,), F32)]).reshape(1, LANES)
    return pl.pallas_call(
        _route_body,
        grid=(t // tm,),
        in_specs=[pl.BlockSpec((tm, d), lambda i: (i, 0)),
                  pl.BlockSpec((1, d), lambda i: (0, 0)),
                  pl.BlockSpec((d, LANES), lambda i: (0, 0)),
                  pl.BlockSpec((1, LANES), lambda i: (0, 0))],
        out_specs=[pl.BlockSpec((tm, d), lambda i: (i, 0)),
                   pl.BlockSpec((tm, LANES), lambda i: (i, 0)),
                   pl.BlockSpec((8, LANES), lambda i: (0, 0))],
        out_shape=[jax.ShapeDtypeStruct((t, d), F32),
                   jax.ShapeDtypeStruct((t, LANES), F32),
                   jax.ShapeDtypeStruct((8, LANES), F32)],
        scratch_shapes=[pltpu.VMEM((1, LANES), F32)],
        compiler_params=_cp(("arbitrary",), 40),
        name="moe_route",
    )(x, g.reshape(1, d).astype(F32), wr, br)


def _invperm_body(pos_ref, inv_ref, *, n_assign, n_slots):
    def clear(p, c):
        inv_ref[p] = 0
        return c

    lax.fori_loop(0, n_slots, clear, 0, unroll=SCALAR_UNROLL)

    def put(a, c):
        inv_ref[pos_ref[a]] = lax.shift_right_logical(a, TOP_K.bit_length() - 1)
        return c

    lax.fori_loop(0, n_assign, put, 0, unroll=SCALAR_UNROLL)


def _invperm(pos_flat, n_slots):
    n_assign = pos_flat.shape[0]
    return pl.pallas_call(
        functools.partial(_invperm_body, n_assign=n_assign, n_slots=n_slots),
        in_specs=[pl.BlockSpec(memory_space=pltpu.SMEM)],
        out_specs=pl.BlockSpec(memory_space=pltpu.SMEM),
        out_shape=jax.ShapeDtypeStruct((n_slots,), jnp.int32),
        name="moe_invperm",
    )(pos_flat)


def _row_copy(src_hbm, row, dst_vmem, slot, sem):
    return pltpu.make_async_copy(src_hbm.at[pl.ds(row, 1), :], dst_vmem.at[pl.ds(slot, 1), :], sem)


def _wait_rows(src_hbm, dst_vmem, sem, n):
    def wait(r, c):
        _row_copy(src_hbm, 0, dst_vmem, r, sem).wait()
        return c

    lax.fori_loop(0, n, wait, 0, unroll=SCALAR_UNROLL)


def _dispatch_body(inv_ref, nact_ref, h_hbm, o_ref, buf, sem, *, tm):
    i = pl.program_id(0)
    nact = nact_ref[0]

    def gather(tile, slot):
        def issue(rr, c):
            for u in range(DMA_QUEUES):
                r = rr * DMA_QUEUES + u
                _row_copy(h_hbm, inv_ref[tile * tm + r], buf.at[slot], r, sem.at[slot]).start(priority=u)
            return c

        lax.fori_loop(0, tm // DMA_QUEUES, issue, 0, unroll=SCALAR_UNROLL // DMA_QUEUES)

    ahead = GATHER_SLOTS - 1
    for first in range(ahead):
        @pl.when((i == 0) & (first < nact))
        def _():
            gather(first, first)

    for slot in range(GATHER_SLOTS):
        @pl.when((i + ahead < nact) & ((i + ahead) % GATHER_SLOTS == slot))
        def _():
            gather(i + ahead, slot)

    for slot in range(GATHER_SLOTS):
        @pl.when((i < nact) & (i % GATHER_SLOTS == slot))
        def _():
            _wait_rows(h_hbm, buf.at[slot], sem.at[slot], tm)
            o_ref[...] = buf[slot].astype(o_ref.dtype)

    @pl.when(i >= nact)
    def _():
        o_ref[...] = jnp.zeros(o_ref.shape, o_ref.dtype)


def _dispatch(h, inv, n_active_tiles, n_tiles, *, tm):
    d = h.shape[1]
    grid_spec = pltpu.PrefetchScalarGridSpec(
        num_scalar_prefetch=2,
        grid=(n_tiles,),
        in_specs=[pl.BlockSpec(memory_space=pl.ANY)],
        out_specs=pl.BlockSpec((tm, d), lambda i, inv, na: (i, 0)),
        scratch_shapes=[pltpu.VMEM((GATHER_SLOTS, tm, d), F32), pltpu.SemaphoreType.DMA((GATHER_SLOTS,))],
    )
    return pl.pallas_call(
        functools.partial(_dispatch_body, tm=tm),
        grid_spec=grid_spec,
        out_shape=jax.ShapeDtypeStruct((n_tiles * tm, d), BF16),
        compiler_params=_cp(("arbitrary",), 40),
        name="moe_dispatch",
    )(inv, n_active_tiles, h)


def _gmm_body(rt_ref, e_ref, jo_ref, jw_ref, first_ref, act_ref, slot_ref, ne_ref, nj_ref, hn_ref, nv_ref, x_ref,
              *refs, n_acc, tn, epilogue):
    w_hbm = refs[:n_acc]
    out = refs[n_acc]
    scratch = refs[n_acc + 1:]
    stage, sem, wbf = scratch[:n_acc], scratch[n_acc], scratch[n_acc + 1:]
    w = pl.program_id(0)

    def fetch(a, expert, col_tile, s):
        col = pl.multiple_of(col_tile * tn, tn)
        return pltpu.make_async_copy(w_hbm[a].at[expert, :, pl.ds(col, tn)], stage[a].at[s], sem.at[a, s])

    @pl.when(first_ref[w] == 1)
    def _():
        s = slot_ref[w]

        @pl.when(w == 0)
        def _():
            for a in range(n_acc):
                fetch(a, e_ref[0], jw_ref[0], 0).start()

        @pl.when(hn_ref[w] == 1)
        def _():
            for a in range(n_acc):
                fetch(a, ne_ref[w], nj_ref[w], 1 - s).start()

        for a in range(n_acc):
            fetch(a, e_ref[w], jw_ref[w], s).wait()
            wbf[a][...] = stage[a][s].astype(BF16)

    tm = x_ref.shape[0]
    for nsub in range(1, tm // MOE_SUB + 1):
        @pl.when((act_ref[w] == 1) & (nv_ref[w] == nsub))
        def _():
            rows = nsub * MOE_SUB
            x = x_ref[0:rows, :].astype(BF16)
            out[0:rows, :] = epilogue([_dot(x, wbf[a][...]) for a in range(n_acc)], ()).astype(out.dtype)
            if rows < tm:
                out[rows:, :] = jnp.zeros((tm - rows, out.shape[1]), out.dtype)

    @pl.when(act_ref[w] == 0)
    def _():
        out[...] = jnp.zeros(out.shape, out.dtype)


def _gmm(xs, weights, tables, *, tm, tn, out_dtype, epilogue, vmem_mb, name):
    p, kdim = xs.shape
    n_out = weights[0].shape[2]
    n_items = tables[0].shape[0]
    n_acc = len(weights)
    grid_spec = pltpu.PrefetchScalarGridSpec(
        num_scalar_prefetch=len(tables),
        grid=(n_items,),
        in_specs=[pl.BlockSpec((tm, kdim), lambda w, rt, *_: (rt[w], 0))]
        + [pl.BlockSpec(memory_space=pl.ANY) for _ in weights],
        out_specs=pl.BlockSpec((tm, tn), lambda w, rt, e, jo, *_: (rt[w], jo[w])),
        scratch_shapes=[pltpu.VMEM((2, kdim, tn), weights[0].dtype) for _ in weights]
        + [pltpu.SemaphoreType.DMA((n_acc, 2))]
        + [pltpu.VMEM((kdim, tn), BF16) for _ in weights],
    )
    return pl.pallas_call(
        functools.partial(_gmm_body, n_acc=n_acc, tn=tn, epilogue=epilogue),
        grid_spec=grid_spec,
        out_shape=jax.ShapeDtypeStruct((p, n_out), out_dtype),
        compiler_params=_cp(("arbitrary",), vmem_mb),
        name=name,
    )(*tables, xs, *weights)


def _work_tables(counts, n_col_tiles, tm, n_tiles):
    tiles_e = (counts + tm - 1) // tm
    tile_end = jnp.cumsum(tiles_e)
    tile_start = tile_end - tiles_e
    total_tiles = tile_end[-1]
    n_items = n_col_tiles * n_tiles
    item_end = n_col_tiles * tile_end
    w = jnp.arange(n_items, dtype=jnp.int32)
    n_active = n_col_tiles * total_tiles
    active = w < n_active
    wc = jnp.minimum(w, n_active - 1)
    e = jnp.sum((wc[:, None] >= item_end[None, :]).astype(jnp.int32), axis=1)
    e = jnp.minimum(e, N_EXPERTS - 1)
    te = jnp.maximum(tiles_e[e], 1)
    local = wc - n_col_tiles * tile_start[e]
    j = local // te
    q = local % te
    rt = tile_start[e] + q
    first = (q == 0) & active
    slot = (jnp.cumsum(first.astype(jnp.int32)) - 1) % 2
    nxt = lax.cummin(jnp.where(first, w, n_items), axis=0, reverse=True)
    nxt = jnp.concatenate([nxt[1:], jnp.full((1,), n_items, nxt.dtype)])
    has_next = nxt < n_items
    nxt = jnp.minimum(nxt, n_items - 1)
    spare = jnp.maximum(w - n_active, 0)
    rt = jnp.where(active, rt, total_tiles + spare // n_col_tiles)
    j_out = jnp.where(active, j, spare % n_col_tiles)
    rows_valid = jnp.clip(counts[e] - q * tm, 1, tm)
    n_sub = (rows_valid + MOE_SUB - 1) // MOE_SUB
    i32 = lambda a: a.astype(jnp.int32)
    return (i32(rt), i32(e), i32(j_out), i32(j), i32(first), i32(active), i32(slot), i32(e[nxt]), i32(j[nxt]),
            i32(has_next), i32(n_sub))


def _combine_body(pos_ref, x_ref, info_ref, g_ref, y_hbm, o_ref, buf, sem, *, tm):
    i = pl.program_id(0)
    n = pl.num_programs(0)

    def gather(tile, slot):
        def issue(r, c):
            for k in range(TOP_K):
                _row_copy(y_hbm, pos_ref[(tile * tm + r) * TOP_K + k], buf.at[slot, k], r,
                          sem.at[slot]).start(priority=k % DMA_QUEUES)
            return c

        lax.fori_loop(0, tm, issue, 0, unroll=SCALAR_UNROLL)

    @pl.when(i == 0)
    def _():
        gather(0, 0)

    for slot in range(2):
        @pl.when((i + 1 < n) & ((i + 1) % 2 == slot))
        def _():
            gather(i + 1, slot)

    for slot in range(2):
        @pl.when(i % 2 == slot)
        def _():
            for k in range(TOP_K):
                _wait_rows(y_hbm, buf.at[slot, k], sem.at[slot], tm)
            info = info_ref[...]
            acc = x_ref[...] + info[:, 4:5] * buf[slot, 0] + info[:, 5:6] * buf[slot, 1]
            y = acc * lax.rsqrt(jnp.mean(acc * acc, axis=-1, keepdims=True) + NORM_EPS)
            o_ref[...] = y * g_ref[...]


def _combine(x, info, pos_flat, y_sorted, final_norm, *, tm=128):
    t, d = x.shape
    grid_spec = pltpu.PrefetchScalarGridSpec(
        num_scalar_prefetch=1,
        grid=(t // tm,),
        in_specs=[pl.BlockSpec((tm, d), lambda i, pos: (i, 0)),
                  pl.BlockSpec((tm, LANES), lambda i, pos: (i, 0)),
                  pl.BlockSpec((1, d), lambda i, pos: (0, 0)),
                  pl.BlockSpec(memory_space=pl.ANY)],
        out_specs=pl.BlockSpec((tm, d), lambda i, pos: (i, 0)),
        scratch_shapes=[pltpu.VMEM((2, TOP_K, tm, d), F32), pltpu.SemaphoreType.DMA((2,))],
    )
    return pl.pallas_call(
        functools.partial(_combine_body, tm=tm),
        grid_spec=grid_spec,
        out_shape=jax.ShapeDtypeStruct((t, d), F32),
        compiler_params=_cp(("arbitrary",), 40),
        name="moe_combine",
    )(pos_flat, x, info, final_norm.reshape(1, d).astype(F32), y_sorted)


def _even_mixers(x, tables, w_in, attn_norm, q_norm, w_uq, kv_norm, w_ukv, gate_bias, head_norm, batch, seq):
    t = x.shape[0]
    c_tab, s1_tab, s2_tab = tables
    n_ml_main = 2 * ML_HEADS * ML_QK + 2 * ML_HEADS * ML_V
    gate_lo = MLA_IN + n_ml_main
    w_t = w_in.T
    w_a_t = jnp.concatenate(
        [w_t[:MLA_IN], jnp.zeros((LANES - MLA_ROPE, D_MODEL), w_t.dtype),
         w_t[gate_lo:], jnp.zeros((LANES - 2 * ML_HEADS, D_MODEL), w_t.dtype)], axis=0)
    n_a = w_a_t.shape[0]
    w_q = jnp.pad(w_uq.reshape(MLA_Q_LORA, MLA_HEADS, MLA_NOPE + MLA_ROPE),
                  ((0, 0), (0, 0), (0, MLA_QK_PAD - MLA_NOPE - MLA_ROPE))).reshape(MLA_Q_LORA, -1).astype(BF16)
    w_kv3 = w_ukv.reshape(MLA_KV_LORA, MLA_HEADS, MLA_NOPE + MLA_V)
    w_k = jnp.pad(w_kv3[:, :, :MLA_NOPE], ((0, 0), (0, 0), (0, MLA_QK_PAD - MLA_NOPE))).reshape(MLA_KV_LORA, -1).astype(BF16)
    w_v = w_kv3[:, :, MLA_NOPE:].reshape(MLA_KV_LORA, -1).astype(BF16)

    xn = _rmsnorm(x, attn_norm, width=D_MODEL)
    u_a = _matmul_ws([xn], [[(w_a_t, 0)]], n_out=n_a, tm=1024, tn=256, out_dtype=F32, epilogue=_epi_plain,
                     transposed=True, name="even_in_a")
    u_b = _matmul_ws([xn], [[(w_t, 0)]], n_out=n_ml_main, tm=1024, tn=512, out_dtype=BF16, epilogue=_epi_plain,
                     col_offset=MLA_IN, transposed=True, name="even_in_b")

    cqn = _rmsnorm(u_a, q_norm, width=MLA_Q_LORA, col_block=0)
    ckvn = _rmsnorm(u_a, kv_norm, width=MLA_KV_LORA, col_block=MLA_Q_LORA // MLA_KV_LORA)
    tm_p = min(1024, t)
    tab_specs = [(tab, (tm_p, LANES), lambda i, j: (i, 0)) for tab in (c_tab, s1_tab, s2_tab)]
    n_qk = MLA_HEADS * MLA_QK_PAD
    q_full = _matmul_ws([cqn], [[(w_q, 0)]], n_out=n_qk, tm=tm_p, tn=1024, out_dtype=BF16, epilogue=_epi_mla_q,
                        extras=tab_specs, name="mla_q")
    kr_spec = (u_a, (tm_p, LANES), lambda i, j: (i, (MLA_Q_LORA + MLA_KV_LORA) // LANES))
    k_full = _matmul_ws([ckvn], [[(w_k, 0)]], n_out=n_qk, tm=tm_p, tn=1024, out_dtype=BF16, epilogue=_epi_mla_k,
                        extras=[kr_spec] + tab_specs, name="mla_k")
    v = _matmul_ws([ckvn], [[(w_v, 0)]], n_out=MLA_HEADS * MLA_V, tm=tm_p, tn=1024, out_dtype=BF16,
                   epilogue=_epi_plain, name="mla_v")
    y_a = _mla_attention(q_full, k_full, v, batch, seq)

    bias_pad = jnp.concatenate([gate_bias.astype(F32), jnp.zeros((LANES - 2 * ML_HEADS,), F32)]).reshape(1, LANES)
    y_b = _mlstm(u_b, u_a, (MLA_IN + LANES - MLA_ROPE) // LANES, bias_pad, head_norm, batch, seq)
    return y_a, y_b


def _even_layer(x, tables, w_in, attn_norm, q_norm, w_uq, kv_norm, w_ukv, gate_bias, head_norm, w_out,
                ffn_norm, w_gate, w_up, w_down, batch, seq):
    y_a, y_b = _even_mixers(x, tables, w_in, attn_norm, q_norm, w_uq, kv_norm, w_ukv, gate_bias, head_norm,
                            batch, seq)
    tm = min(1024, x.shape[0])
    res_spec = lambda arr, tn: [(arr, (tm, tn), lambda i, j: (i, j))]
    x1 = _matmul_ws([y_a, y_b], [[(w_out, 0), (w_out, 1)]], n_out=D_MODEL, tm=tm, tn=512, out_dtype=F32,
                    epilogue=_epi_residual, extras=res_spec(x, 512), name="even_out")
    hn = _rmsnorm(x1, ffn_norm, width=D_MODEL)
    a = _matmul_ws([hn], [[(w_gate, 0)], [(w_up, 0)]], n_out=D_FF, tm=min(2048, x.shape[0]), tn=256, out_dtype=BF16,
                   epilogue=_epi_swiglu, vmem_mb=58, name="ffn_gate_up")
    tm_d = min(2048, x.shape[0])
    x2 = _matmul([a], [[(w_down, 0)]], n_out=D_MODEL, tm=tm_d, tn=1024, tk=512, out_dtype=F32, epilogue=_epi_residual,
                 extras=[(x1, (tm_d, 1024), lambda i, j: (i, j))], vmem_mb=58, name="ffn_down")
    return x2


def _odd_layer(x, w_in, attn_norm, conv_w, w_out, ffn_norm, w_router, b_router, w_gate_e, w_up_e, w_down_e,
               final_norm, batch, seq):
    t = x.shape[0]
    xn = _rmsnorm(x, attn_norm, width=D_MODEL)
    n_in = w_in.shape[1]
    tm = min(1024, t)
    u = _matmul_ws([xn], [[(w_in, 0)]], n_out=n_in, tm=tm, tn=512, out_dtype=BF16, epilogue=_epi_plain,
                   name="odd_in")
    y_c = _short_conv(u, conv_w, batch, seq)
    y_d = _sb_attention(u, 3 * SC_WIDTH // LANES, batch, seq)
    x1 = _matmul_ws([y_c, y_d], [[(w_out, 0), (w_out, 1)]], n_out=D_MODEL, tm=tm, tn=512, out_dtype=F32,
                    epilogue=_epi_residual, extras=[(x, (tm, 512), lambda i, j: (i, j))], name="odd_out")
    return _moe(x1, ffn_norm, w_router, b_router, w_gate_e, w_up_e, w_down_e, final_norm)


def _moe(x1, ffn_norm, w_router, b_router, w_gate_e, w_up_e, w_down_e, final_norm):
    t = x1.shape[0]
    h, info, cnt = _route(x1, ffn_norm, w_router, b_router)
    tm = MOE_TM
    n_tiles = (t * TOP_K) // tm + N_EXPERTS
    counts = cnt[0, :N_EXPERTS].astype(jnp.int32)
    tiles_e = (counts + tm - 1) // tm
    offs = (jnp.cumsum(tiles_e) - tiles_e) * tm
    idx = info[:, 0:TOP_K].astype(jnp.int32)
    rank = info[:, TOP_K:2 * TOP_K].astype(jnp.int32)
    pos_flat = (offs[idx] + rank).reshape(-1)
    n_active = jnp.sum(tiles_e).astype(jnp.int32).reshape(1)

    inv = _invperm(pos_flat, n_tiles * tm)
    xs = _dispatch(h, inv, n_active, n_tiles, tm=tm)
    tn_gu, tn_d = 512, 1024
    tab_gu = _work_tables(counts, D_FF_EXPERT // tn_gu, tm, n_tiles)
    a_s = _gmm(xs, [w_gate_e, w_up_e], tab_gu, tm=tm, tn=tn_gu, out_dtype=BF16, epilogue=_epi_swiglu,
               vmem_mb=58, name="moe_gate_up")
    tab_d = _work_tables(counts, D_MODEL // tn_d, tm, n_tiles)
    y_s = _gmm(a_s, [w_down_e], tab_d, tm=tm, tn=tn_d, out_dtype=F32, epilogue=_epi_plain,
               vmem_mb=58, name="moe_down")
    return _combine(x1, info, pos_flat, y_s, final_norm)


def kernel(x, positions, even_attn_norm, even_w_in, even_q_norm, even_w_uq, even_kv_norm, even_w_ukv, even_ml_gate_bias, even_ml_head_norm, even_w_out, even_ffn_norm, even_w_gate, even_w_up, even_w_down, odd_attn_norm, odd_w_in, odd_conv_w, odd_w_out, odd_ffn_norm, odd_w_router, odd_b_router, odd_w_gate_e, odd_w_up_e, odd_w_down_e, final_norm):
    batch, seq, d = x.shape
    assert even_w_in.shape[0] == 1 and odd_w_in.shape[0] == 1, "kernel is written for one even and one odd layer"
    xf = x.reshape(batch * seq, d)
    tables = _rope_tables(positions)
    xf = _even_layer(xf, tables, even_w_in[0], even_attn_norm[0], even_q_norm[0], even_w_uq[0], even_kv_norm[0],
                     even_w_ukv[0], even_ml_gate_bias[0], even_ml_head_norm[0], even_w_out[0], even_ffn_norm[0],
                     even_w_gate[0], even_w_up[0], even_w_down[0], batch, seq)
    out = _odd_layer(xf, odd_w_in[0], odd_attn_norm[0], odd_conv_w[0], odd_w_out[0], odd_ffn_norm[0],
                     odd_w_router[0], odd_b_router[0], odd_w_gate_e[0], odd_w_up_e[0], odd_w_down_e[0],
                     final_norm, batch, seq)
    return out.reshape(batch, seq, d)
```

```python
import functools

import jax
import jax.numpy as jnp
from jax import lax
from jax.experimental import pallas as pl
from jax.experimental.pallas import tpu as pltpu

F32 = jnp.float32
BF16 = jnp.bfloat16

D_MODEL = 4096
NORM_EPS = 1e-6

MLA_HEADS = 16
MLA_Q_LORA = 1024
MLA_KV_LORA = 512
MLA_NOPE = 128
MLA_ROPE = 64
MLA_V = 128
ROPE_THETA = 10000.0
MLA_IN = MLA_Q_LORA + MLA_KV_LORA + MLA_ROPE
MLA_QK_PAD = 256

ML_HEADS = 4
ML_QK = 256
ML_V = 512
ML_CHUNK = 128
ML_VA = ML_V + 128

SC_WIDTH = 2048
SC_KERNEL = 3

SB_HEADS = 16
SB_HEAD_DIM = 128

D_FF = 11008
N_EXPERTS = 8
TOP_K = 2
D_FF_EXPERT = 4096

LANES = 128
SUBLANES_BF16 = 16
MOE_TM = 512
MOE_SUB = 256
GATHER_SLOTS = 3
DMA_QUEUES = 2
LOG2_E = 1.4426950408889634
SCALAR_UNROLL = 8


def _cp(sem, vmem_mb):
    return pltpu.CompilerParams(dimension_semantics=sem, vmem_limit_bytes=vmem_mb * 1024 * 1024)


def _dot(a, b):
    return jnp.dot(a, b, preferred_element_type=F32)


def _dot_nt(a, b):
    return lax.dot_general(a, b, (((1,), (1,)), ((), ())), preferred_element_type=F32)


def _log_sigmoid(x):
    return jnp.minimum(x, 0.0) - jnp.log1p(jnp.exp(-jnp.abs(x)))


def _split_dot(tri, x, pieces):
    acc = None
    r = x
    for p in range(pieces):
        hi = r.astype(BF16)
        part = _dot(tri, hi)
        acc = part if acc is None else acc + part
        if p + 1 < pieces:
            r = r - hi.astype(F32)
    return acc


def _rmsnorm_body(x_ref, g_ref, o_ref):
    x = x_ref[...].astype(F32)
    y = x * lax.rsqrt(jnp.mean(x * x, axis=-1, keepdims=True) + NORM_EPS)
    o_ref[...] = (y * g_ref[...]).astype(o_ref.dtype)


def _rmsnorm(x, g, *, width, col_block=0, out_dtype=BF16, tm=256):
    m = x.shape[0]
    return pl.pallas_call(
        _rmsnorm_body,
        grid=(m // tm,),
        in_specs=[pl.BlockSpec((tm, width), lambda i: (i, col_block)),
                  pl.BlockSpec((1, width), lambda i: (0, 0))],
        out_specs=pl.BlockSpec((tm, width), lambda i: (i, 0)),
        out_shape=jax.ShapeDtypeStruct((m, width), out_dtype),
        compiler_params=_cp(("parallel",), 40),
        name="rmsnorm",
    )(x, g.reshape(1, width).astype(F32))


def _mm_body(*refs, n_lhs, n_acc, n_extra, nk, k_valid_last, epilogue):
    lhs = refs[:n_lhs]
    ws = refs[n_lhs:n_lhs + n_acc * n_lhs]
    extras = refs[n_lhs + n_acc * n_lhs:n_lhs + n_acc * n_lhs + n_extra]
    out = refs[n_lhs + n_acc * n_lhs + n_extra]
    accs = refs[n_lhs + n_acc * n_lhs + n_extra + 1:]
    k = pl.program_id(2)

    def partials(mask_tail):
        parts = []
        for a in range(n_acc):
            s = None
            for l in range(n_lhs):
                x = lhs[l][...]
                w = ws[a * n_lhs + l][...]
                if mask_tail:
                    xc = lax.broadcasted_iota(jnp.int32, x.shape, 1)
                    x = jnp.where(xc < k_valid_last, x, jnp.zeros_like(x))
                    wr = lax.broadcasted_iota(jnp.int32, w.shape, 0)
                    w = jnp.where(wr < k_valid_last, w, jnp.zeros_like(w))
                p = _dot(x.astype(BF16), w.astype(BF16))
                s = p if s is None else s + p
            parts.append(s)
        return parts

    if nk == 1:
        out[...] = epilogue(partials(False), extras).astype(out.dtype)
        return

    ragged = k_valid_last is not None

    @pl.when(k == 0)
    def _():
        for a, p in enumerate(partials(False)):
            accs[a][...] = p

    @pl.when((k > 0) & (k < nk - 1) if ragged else (k > 0))
    def _():
        for a, p in enumerate(partials(False)):
            accs[a][...] += p

    if ragged:
        @pl.when(k == nk - 1)
        def _():
            for a, p in enumerate(partials(True)):
                accs[a][...] += p

    @pl.when(k == nk - 1)
    def _():
        out[...] = epilogue([acc[...] for acc in accs], extras).astype(out.dtype)


def _matmul(lhs, weights, *, n_out, tm, tn, tk, out_dtype, epilogue, extras=(), vmem_mb=48, name="matmul"):
    m = lhs[0].shape[0]
    kdim = lhs[0].shape[1]
    tm = min(tm, m)
    nk = pl.cdiv(kdim, tk)
    k_valid_last = None if kdim % tk == 0 else kdim - (nk - 1) * tk
    n_lhs, n_acc = len(lhs), len(weights)
    grid = (m // tm, n_out // tn, nk)
    ij = lambda g0, g1: (g0, g1)

    in_specs, args = [], []
    for x in lhs:
        in_specs.append(pl.BlockSpec((tm, tk), lambda g0, g1, k: (ij(g0, g1)[0], k)))
        args.append(x)
    for wl in weights:
        for (w, rb) in wl:
            in_specs.append(pl.BlockSpec((tk, tn), lambda g0, g1, k, rb=rb: (rb * nk + k, ij(g0, g1)[1])))
            args.append(w)
    for (arr, bshape, imap) in extras:
        in_specs.append(pl.BlockSpec(bshape, lambda g0, g1, k, imap=imap: imap(*ij(g0, g1))))
        args.append(arr)
    scratch = [pltpu.VMEM((tm, tn), F32) for _ in range(n_acc)] if nk > 1 else []
    body = functools.partial(_mm_body, n_lhs=n_lhs, n_acc=n_acc, n_extra=len(extras), nk=nk,
                             k_valid_last=k_valid_last, epilogue=epilogue)
    return pl.pallas_call(
        body,
        grid=grid,
        in_specs=in_specs,
        out_specs=pl.BlockSpec((tm, tn), lambda g0, g1, k: ij(g0, g1)),
        out_shape=jax.ShapeDtypeStruct((m, n_out), out_dtype),
        scratch_shapes=scratch,
        compiler_params=_cp(("parallel", "parallel", "arbitrary"), vmem_mb),
        name=name,
    )(*args)


def _mm_ws_body(*refs, n_lhs, n_acc, n_extra, tn, kdims, row_blocks, col_offset, transposed, cast, epilogue):
    n_w = n_acc * n_lhs
    lhs = refs[:n_lhs]
    w_hbm = refs[n_lhs:n_lhs + n_w]
    extras = refs[n_lhs + n_w:n_lhs + n_w + n_extra]
    out = refs[n_lhs + n_w + n_extra]
    scratch = refs[n_lhs + n_w + n_extra + 1:]
    stage, sem, wbf = scratch[:n_w], scratch[n_w], scratch[n_w + 1:]
    j = pl.program_id(0)
    i = pl.program_id(1)
    nj = pl.num_programs(0)
    slot = j % 2

    def fetch(widx, jj, s):
        kd = kdims[widx % n_lhs]
        if transposed:
            col = pl.multiple_of(col_offset + jj * tn, SUBLANES_BF16)
            src = w_hbm[widx].at[pl.ds(col, tn), pl.ds(row_blocks[widx] * kd, kd)]
        else:
            col = pl.multiple_of(col_offset + jj * tn, LANES)
            src = w_hbm[widx].at[pl.ds(row_blocks[widx] * kd, kd), pl.ds(col, tn)]
        return pltpu.make_async_copy(src, stage[widx].at[s], sem.at[widx, s])

    @pl.when(i == 0)
    def _():
        @pl.when(j == 0)
        def _():
            for widx in range(n_w):
                fetch(widx, 0, 0).start()

        @pl.when(j + 1 < nj)
        def _():
            for widx in range(n_w):
                fetch(widx, j + 1, 1 - slot).start()

        for widx in range(n_w):
            fetch(widx, j, slot).wait()
            if cast:
                wbf[widx][...] = stage[widx][slot].astype(BF16)

    parts = []
    for a in range(n_acc):
        s = None
        for l in range(n_lhs):
            widx = a * n_lhs + l
            w = wbf[widx][...] if cast else stage[widx][slot]
            x = lhs[l][...].astype(BF16)
            p = _dot_nt(x, w) if transposed else _dot(x, w)
            s = p if s is None else s + p
        parts.append(s)
    out[...] = epilogue(parts, extras).astype(out.dtype)


def _matmul_ws(lhs, weights, *, n_out, tm, tn, out_dtype, epilogue, extras=(), col_offset=0, transposed=False,
               vmem_mb=48, name="matmul_ws"):
    assert tn % LANES == 0 and col_offset % (SUBLANES_BF16 if transposed else LANES) == 0
    m = lhs[0].shape[0]
    tm = min(tm, m)
    n_lhs, n_acc = len(lhs), len(weights)
    kdims = tuple(x.shape[1] for x in lhs)
    flat_w = [w for wl in weights for (w, _) in wl]
    row_blocks = tuple(rb for wl in weights for (_, rb) in wl)
    wdtype = flat_w[0].dtype
    cast = wdtype != BF16
    in_specs = [pl.BlockSpec((tm, kd), lambda j, i: (i, 0)) for kd in kdims]
    in_specs += [pl.BlockSpec(memory_space=pl.ANY) for _ in flat_w]
    in_specs += [pl.BlockSpec(bshape, lambda j, i, imap=imap: imap(i, j)) for (_, bshape, imap) in extras]
    w_kdims = [kdims[widx % n_lhs] for widx in range(len(flat_w))]
    tile = (lambda kd: (tn, kd)) if transposed else (lambda kd: (kd, tn))
    scratch = [pltpu.VMEM((2,) + tile(kd), wdtype) for kd in w_kdims]
    scratch.append(pltpu.SemaphoreType.DMA((len(flat_w), 2)))
    if cast:
        scratch += [pltpu.VMEM(tile(kd), BF16) for kd in w_kdims]
    body = functools.partial(_mm_ws_body, n_lhs=n_lhs, n_acc=n_acc, n_extra=len(extras), tn=tn, kdims=kdims,
                             row_blocks=row_blocks, col_offset=col_offset, transposed=transposed, cast=cast,
                             epilogue=epilogue)
    return pl.pallas_call(
        body,
        grid=(n_out // tn, m // tm),
        in_specs=in_specs,
        out_specs=pl.BlockSpec((tm, tn), lambda j, i: (i, j)),
        out_shape=jax.ShapeDtypeStruct((m, n_out), out_dtype),
        scratch_shapes=scratch,
        compiler_params=_cp(("arbitrary", "arbitrary"), vmem_mb),
        name=name,
    )(*lhs, *flat_w, *[arr for (arr, _, _) in extras])


def _epi_plain(parts, extras):
    return parts[0]


def _epi_residual(parts, extras):
    return parts[0] + extras[0][...]


def _epi_swiglu(parts, extras):
    g, u = parts
    return g * jax.nn.sigmoid(g) * u


def _rope_tables_body(pos_ref, invf_ref, c_ref, s1_ref, s2_ref):
    ang = pos_ref[...] * invf_ref[...]
    lane = lax.broadcasted_iota(jnp.int32, ang.shape, 1)
    cos = jnp.cos(ang)
    sin = jnp.sin(ang)
    half = MLA_ROPE // 2
    c_ref[...] = jnp.where(lane < MLA_ROPE, cos, 0.0)
    s1_ref[...] = jnp.where(lane < half, -sin, 0.0)
    s2_ref[...] = jnp.where((lane >= half) & (lane < MLA_ROPE), sin, 0.0)


def _rope_tables(positions):
    t = positions.size
    half = MLA_ROPE // 2
    inv_freq = 1.0 / (ROPE_THETA ** (jnp.arange(0, MLA_ROPE, 2, dtype=F32) / MLA_ROPE))
    invf = jnp.concatenate([inv_freq, inv_freq, jnp.zeros((LANES - 2 * half,), F32)]).reshape(1, LANES)
    pos = jnp.broadcast_to(positions.reshape(t, 1).astype(F32), (t, LANES))
    tm = min(512, t)
    spec = pl.BlockSpec((tm, LANES), lambda i: (i, 0))
    sds = jax.ShapeDtypeStruct((t, LANES), F32)
    return pl.pallas_call(
        _rope_tables_body,
        grid=(t // tm,),
        in_specs=[spec, pl.BlockSpec((1, LANES), lambda i: (0, 0))],
        out_specs=[spec, spec, spec],
        out_shape=[sds, sds, sds],
        compiler_params=_cp(("parallel",), 32),
        name="rope_tables",
    )(pos, invf)


def _rope_lanes(t, c, s1, s2):
    return t * c + pltpu.roll(t, LANES - MLA_ROPE // 2, axis=1) * s1 + pltpu.roll(t, MLA_ROPE // 2, axis=1) * s2


def _epi_mla_q(parts, extras):
    acc = parts[0]
    c, s1, s2 = extras[0][...], extras[1][...], extras[2][...]
    scale = (MLA_NOPE + MLA_ROPE) ** -0.5
    pieces = []
    for h in range(acc.shape[1] // MLA_QK_PAD):
        base = h * MLA_QK_PAD
        pieces.append(acc[:, base:base + MLA_NOPE] * scale)
        pieces.append(_rope_lanes(acc[:, base + MLA_NOPE:base + MLA_QK_PAD], c, s1, s2) * scale)
    return jnp.concatenate(pieces, axis=1)


def _epi_mla_k(parts, extras):
    acc = parts[0]
    kr = _rope_lanes(extras[0][...], extras[1][...], extras[2][...], extras[3][...])
    pieces = []
    for h in range(acc.shape[1] // MLA_QK_PAD):
        base = h * MLA_QK_PAD
        pieces.append(acc[:, base:base + MLA_NOPE])
        pieces.append(acc[:, base + MLA_NOPE:base + MLA_QK_PAD] + kr)
    return jnp.concatenate(pieces, axis=1)


def _mla_attn_body(q_ref, k_ref, v_ref, o_ref, *, tq, seq):
    row = lax.broadcasted_iota(jnp.int32, (tq, tq), 0)
    col = lax.broadcasted_iota(jnp.int32, (tq, tq), 1)
    causal = col <= row
    for qi in range(seq // tq):
        n = (qi + 1) * tq
        q = q_ref[qi * tq:(qi + 1) * tq, :]
        s = _dot_nt(q, k_ref[0:n, :])
        diag = jnp.where(causal, s[:, n - tq:], -jnp.inf)
        s = diag if qi == 0 else jnp.concatenate([s[:, :n - tq], diag], axis=1)
        m = jnp.max(s, axis=1, keepdims=True)
        p = jnp.exp(s - m)
        l = jnp.sum(p, axis=1, keepdims=True)
        o = _dot(p.astype(BF16), v_ref[0:n, :]) / l
        o_ref[qi * tq:(qi + 1) * tq, :] = o.astype(o_ref.dtype)


def _mla_attention(q, k, v, batch, seq, *, tq=256):
    t = q.shape[0]
    tq = min(tq, seq)
    body = functools.partial(_mla_attn_body, tq=tq, seq=seq)
    return pl.pallas_call(
        body,
        grid=(batch, MLA_HEADS),
        in_specs=[pl.BlockSpec((seq, MLA_QK_PAD), lambda b, h: (b, h)),
                  pl.BlockSpec((seq, MLA_QK_PAD), lambda b, h: (b, h)),
                  pl.BlockSpec((seq, MLA_V), lambda b, h: (b, h))],
        out_specs=pl.BlockSpec((seq, MLA_V), lambda b, h: (b, h)),
        out_shape=jax.ShapeDtypeStruct((t, MLA_HEADS * MLA_V), BF16),
        compiler_params=_cp(("parallel", "parallel"), 48),
        name="mla_attention",
    )(q, k, v)


def _mlstm_body(q_ref, k_ref, v_ref, o_ref, g_ref, bias_ref, hn_ref, y_ref, ct_ref, m_ref):
    L = ML_CHUNK
    c = pl.program_id(1)

    @pl.when(c == 0)
    def _():
        ct_ref[...] = jnp.zeros(ct_ref.shape, F32)
        m_ref[...] = jnp.zeros(m_ref.shape, F32)

    g = g_ref[...] + bias_ref[...]
    logf = _log_sigmoid(g)
    row = lax.broadcasted_iota(jnp.int32, (L, L), 0)
    col = lax.broadcasted_iota(jnp.int32, (L, L), 1)
    causal = col <= row
    tri = jnp.where(causal, 1.0, 0.0).astype(BF16)
    cum = _split_dot(tri, logf, 3)
    lane = lax.broadcasted_iota(jnp.int32, (L, LANES), 1)
    colq = jnp.where(lane < ML_HEADS, g, cum)
    rowq = colq.T
    ones_col = jnp.where(lane == 0, 1.0, 0.0).astype(BF16)
    scale = ML_QK ** -0.5

    for h in range(ML_HEADS):
        i_c = colq[:, h:h + 1]
        cf_c = colq[:, ML_HEADS + h:ML_HEADS + h + 1]
        i_r = rowq[h:h + 1, :]
        cf_r = rowq[ML_HEADS + h:ML_HEADS + h + 1, :]
        m_prev = m_ref[h][:, 0:1]
        d = jnp.where(causal, cf_c - cf_r + i_r, -jnp.inf)
        m_inter = cf_c + m_prev
        m_t = jnp.maximum(m_inter, jnp.max(d, axis=1, keepdims=True))
        inter = jnp.exp(m_inter - m_t)
        p = jnp.exp(d - m_t)
        qh = q_ref[:, h * ML_QK:(h + 1) * ML_QK]
        kf = k_ref[:, h * ML_QK:(h + 1) * ML_QK].astype(F32) * scale
        kh = kf.astype(BF16)
        s = _dot_nt(qh, kh) * p
        va = jnp.concatenate([v_ref[:, h * ML_V:(h + 1) * ML_V], ones_col], axis=1)
        ct = ct_ref[h]
        na = _dot(s.astype(BF16), va) + inter * _dot(qh, ct.astype(BF16))
        num = na[:, :ML_V]
        den = na[:, ML_V:ML_V + 1]
        hh = num / jnp.maximum(jnp.abs(den), jnp.exp(-m_t))

        f_tot = cf_c[L - 1:L, :]
        w_log = f_tot - cf_c + i_c
        m_new = jnp.maximum(f_tot + m_prev, jnp.max(w_log, axis=0, keepdims=True))
        decay = jnp.exp(f_tot + m_prev - m_new)
        w = jnp.exp(w_log - m_new)
        wv = (va.astype(F32) * w).astype(BF16)
        ct_ref[h] = decay * ct + _dot(kf.T.astype(BF16), wv)
        m_ref[h] = jnp.broadcast_to(m_new, (1, LANES))

        hn = hh * lax.rsqrt(jnp.mean(hh * hh, axis=1, keepdims=True) + NORM_EPS) * hn_ref[:, h * ML_V:(h + 1) * ML_V]
        og = o_ref[:, h * ML_V:(h + 1) * ML_V].astype(F32)
        y_ref[:, h * ML_V:(h + 1) * ML_V] = (jax.nn.sigmoid(og) * hn).astype(y_ref.dtype)


def _mlstm(u_b, gates_src, gate_col_block, bias_pad, head_norm, batch, seq):
    t = u_b.shape[0]
    nc = seq // ML_CHUNK
    nqk = ML_HEADS * ML_QK
    nv = ML_HEADS * ML_V
    rows = lambda b, c: b * nc + c
    return pl.pallas_call(
        _mlstm_body,
        grid=(batch, nc),
        in_specs=[pl.BlockSpec((ML_CHUNK, nqk), lambda b, c: (rows(b, c), 0)),
                  pl.BlockSpec((ML_CHUNK, nqk), lambda b, c: (rows(b, c), 1)),
                  pl.BlockSpec((ML_CHUNK, nv), lambda b, c: (rows(b, c), 1)),
                  pl.BlockSpec((ML_CHUNK, nv), lambda b, c: (rows(b, c), 2)),
                  pl.BlockSpec((ML_CHUNK, LANES), lambda b, c: (rows(b, c), gate_col_block)),
                  pl.BlockSpec((1, LANES), lambda b, c: (0, 0)),
                  pl.BlockSpec((1, nv), lambda b, c: (0, 0))],
        out_specs=pl.BlockSpec((ML_CHUNK, nv), lambda b, c: (rows(b, c), 0)),
        out_shape=jax.ShapeDtypeStruct((t, nv), BF16),
        scratch_shapes=[pltpu.VMEM((ML_HEADS, ML_QK, ML_VA), F32), pltpu.VMEM((ML_HEADS, 1, LANES), F32)],
        compiler_params=_cp(("parallel", "arbitrary"), 32),
        name="mlstm",
    )(u_b, u_b, u_b, u_b, gates_src, bias_pad, head_norm.reshape(1, nv).astype(F32))


def _conv_body(b_ref, c_ref, h_ref, cp_ref, hp_ref, w_ref, y_ref, z_sc, *, ts, halo):
    i = pl.program_id(1)
    z = c_ref[...].astype(F32) * h_ref[...].astype(F32)
    zp = cp_ref[...].astype(F32) * hp_ref[...].astype(F32)
    z_sc[0:halo, :] = jnp.where(i > 0, zp, 0.0)
    z_sc[halo:halo + ts, :] = z
    w = w_ref[...]
    y = w[2:3, :] * z + w[1:2, :] * z_sc[halo - 1:halo - 1 + ts, :] + w[0:1, :] * z_sc[halo - 2:halo - 2 + ts, :]
    y_ref[...] = (b_ref[...].astype(F32) * y).astype(y_ref.dtype)


def _short_conv(u, conv_w, batch, seq, *, ts=256, tw=1024):
    t = u.shape[0]
    halo = 16
    ns = seq // ts
    nw = SC_WIDTH // tw
    wpad = jnp.concatenate([conv_w.astype(F32), jnp.zeros((8 - SC_KERNEL, SC_WIDTH), F32)], axis=0)
    rows = lambda b, i: b * ns + i
    prev = lambda b, i: jnp.maximum((b * seq + i * ts) // halo - 1, 0)
    body = functools.partial(_conv_body, ts=ts, halo=halo)
    return pl.pallas_call(
        body,
        grid=(batch, ns, nw),
        in_specs=[pl.BlockSpec((ts, tw), lambda b, i, j: (rows(b, i), j)),
                  pl.BlockSpec((ts, tw), lambda b, i, j: (rows(b, i), nw + j)),
                  pl.BlockSpec((ts, tw), lambda b, i, j: (rows(b, i), 2 * nw + j)),
                  pl.BlockSpec((halo, tw), lambda b, i, j: (prev(b, i), nw + j)),
                  pl.BlockSpec((halo, tw), lambda b, i, j: (prev(b, i), 2 * nw + j)),
                  pl.BlockSpec((8, tw), lambda b, i, j: (0, j))],
        out_specs=pl.BlockSpec((ts, tw), lambda b, i, j: (rows(b, i), j)),
        out_shape=jax.ShapeDtypeStruct((t, SC_WIDTH), BF16),
        scratch_shapes=[pltpu.VMEM((halo + ts, tw), F32)],
        compiler_params=_cp(("parallel", "parallel", "parallel"), 32),
        name="short_conv",
    )(u, u, u, u, u, wpad)


def _sb_body(q_ref, k_ref, v_ref, o_ref, *, tq, seq):
    scale = SB_HEAD_DIM ** -0.5 * LOG2_E
    row = lax.broadcasted_iota(jnp.int32, (tq, tq), 0)
    col = lax.broadcasted_iota(jnp.int32, (tq, tq), 1)
    strict = col < row
    neg_upper_incl = jnp.where(row >= col, -1.0, 0.0).astype(BF16)
    for qi in range(seq // tq):
        nb = qi + 1
        q = q_ref[qi * tq:(qi + 1) * tq, :]
        z = _dot_nt(q, k_ref[0:nb * tq, :]) * scale
        sp = jnp.maximum(z, 0.0) + jnp.log2(1.0 + jnp.exp2(-jnp.abs(z)))
        later = jnp.zeros((tq, 1), F32)
        blocks = [None] * nb
        for j in range(nb - 1, -1, -1):
            zj = z[:, j * tq:(j + 1) * tq]
            spj = sp[:, j * tq:(j + 1) * tq]
            spm = jnp.where(strict, spj, 0.0) if j == nb - 1 else spj
            upto = _split_dot_rhs(spm, neg_upper_incl, 2)
            e = jnp.exp2(zj + upto + later)
            if j == nb - 1:
                e = jnp.where(strict, e, 0.0)
            blocks[j] = e.astype(BF16)
            later = later + upto[:, 0:1]
        a = blocks[0] if nb == 1 else jnp.concatenate(blocks, axis=1)
        o_ref[qi * tq:(qi + 1) * tq, :] = _dot(a, v_ref[0:nb * tq, :]).astype(o_ref.dtype)


def _split_dot_rhs(x, tri, pieces):
    acc = None
    r = x
    for p in range(pieces):
        hi = r.astype(BF16)
        part = _dot(hi, tri)
        acc = part if acc is None else acc + part
        if p + 1 < pieces:
            r = r - hi.astype(F32)
    return acc


def _sb_attention(u, col_base, batch, seq, *, tq=256):
    t = u.shape[0]
    tq = min(tq, seq)
    body = functools.partial(_sb_body, tq=tq, seq=seq)
    spec = lambda off: pl.BlockSpec((seq, SB_HEAD_DIM), lambda b, h: (b, col_base + off + h))
    return pl.pallas_call(
        body,
        grid=(batch, SB_HEADS),
        in_specs=[spec(0), spec(SB_HEADS), spec(2 * SB_HEADS)],
        out_specs=pl.BlockSpec((seq, SB_HEAD_DIM), lambda b, h: (b, h)),
        out_shape=jax.ShapeDtypeStruct((t, SB_HEADS * SB_HEAD_DIM), BF16),
        compiler_params=_cp(("parallel", "parallel"), 48),
        name="sb_attention",
    )(u, u, u)


def _route_body(x_ref, g_ref, wr_ref, br_ref, h_ref, info_ref, cnt_ref, carry):
    i = pl.program_id(0)

    @pl.when(i == 0)
    def _():
        carry[...] = jnp.zeros(carry.shape, F32)

    x = x_ref[...]
    h = x * lax.rsqrt(jnp.mean(x * x, axis=-1, keepdims=True) + NORM_EPS) * g_ref[...]
    h_ref[...] = h
    wr = wr_ref[...]
    h_hi = h.astype(BF16)
    h_lo = (h - h_hi.astype(F32)).astype(BF16)
    w_hi = wr.astype(BF16)
    w_lo = (wr - w_hi.astype(F32)).astype(BF16)
    logits = (_dot(h_hi, w_hi) + _dot(h_hi, w_lo)) + (_dot(h_lo, w_hi) + _dot(h_lo, w_lo)) + br_ref[...]
    tm = logits.shape[0]
    lane = lax.broadcasted_iota(jnp.int32, logits.shape, 1).astype(F32)
    lg = jnp.where(lane < N_EXPERTS, logits, -jnp.inf)
    m1 = jnp.max(lg, axis=1, keepdims=True)
    i1 = jnp.min(jnp.where(lg == m1, lane, float(LANES)), axis=1, keepdims=True)
    lg2 = jnp.where(lane == i1, -jnp.inf, lg)
    m2 = jnp.max(lg2, axis=1, keepdims=True)
    i2 = jnp.min(jnp.where(lg2 == m2, lane, float(LANES)), axis=1, keepdims=True)
    e = jnp.exp(m2 - m1)
    w1 = 1.0 / (1.0 + e)
    w2 = e / (1.0 + e)
    oh1 = lane == i1
    oh2 = lane == i2
    mask = jnp.where(oh1 | oh2, 1.0, 0.0)
    row = lax.broadcasted_iota(jnp.int32, (tm, tm), 0)
    col = lax.broadcasted_iota(jnp.int32, (tm, tm), 1)
    before = jnp.where(col < row, 1.0, 0.0).astype(BF16)
    rank_mat = _dot(before, mask.astype(BF16)) + carry[...]
    r1 = jnp.sum(jnp.where(oh1, rank_mat, 0.0), axis=1, keepdims=True)
    r2 = jnp.sum(jnp.where(oh2, rank_mat, 0.0), axis=1, keepdims=True)
    carry[...] += jnp.sum(mask, axis=0, keepdims=True)
    info = jnp.where(lane == 0, i1,
           jnp.where(lane == 1, i2,
           jnp.where(lane == 2, r1,
           jnp.where(lane == 3, r2,
           jnp.where(lane == 4, w1,
           jnp.where(lane == 5, w2, 0.0))))))
    info_ref[...] = info
    cnt_ref[...] = jnp.broadcast_to(carry[...], cnt_ref.shape)


def _route(x, g, w_router, b_router, *, tm=256):
    t, d = x.shape
    wr = jnp.concatenate([w_router.astype(F32), jnp.zeros((d, LANES - N_EXPERTS), F32)], axis=1)
    br = jnp.concatenate([b_router.astype(F32), jnp.zeros((LANES - N_EXPERTS,), F32)]).reshape(1, LANES)
    return pl.pallas_call(
        _route_body,
        grid=(t // tm,),
        in_specs=[pl.BlockSpec((tm, d), lambda i: (i, 0)),
                  pl.BlockSpec((1, d), lambda i: (0, 0)),
                  pl.BlockSpec((d, LANES), lambda i: (0, 0)),
                  pl.BlockSpec((1, LANES), lambda i: (0, 0))],
        out_specs=[pl.BlockSpec((tm, d), lambda i: (i, 0)),
                   pl.BlockSpec((tm, LANES), lambda i: (i, 0)),
                   pl.BlockSpec((8, LANES), lambda i: (0, 0))],
        out_shape=[jax.ShapeDtypeStruct((t, d), F32),
                   jax.ShapeDtypeStruct((t, LANES), F32),
                   jax.ShapeDtypeStruct((8, LANES), F32)],
        scratch_shapes=[pltpu.VMEM((1, LANES), F32)],
        compiler_params=_cp(("arbitrary",), 40),
        name="moe_route",
    )(x, g.reshape(1, d).astype(F32), wr, br)


def _invperm_body(pos_ref, inv_ref, *, n_assign, n_slots):
    def clear(p, c):
        inv_ref[p] = 0
        return c

    lax.fori_loop(0, n_slots, clear, 0, unroll=SCALAR_UNROLL)

    def put(a, c):
        inv_ref[pos_ref[a]] = lax.shift_right_logical(a, TOP_K.bit_length() - 1)
        return c

    lax.fori_loop(0, n_assign, put, 0, unroll=SCALAR_UNROLL)


def _invperm(pos_flat, n_slots):
    n_assign = pos_flat.shape[0]
    return pl.pallas_call(
        functools.partial(_invperm_body, n_assign=n_assign, n_slots=n_slots),
        in_specs=[pl.BlockSpec(memory_space=pltpu.SMEM)],
        out_specs=pl.BlockSpec(memory_space=pltpu.SMEM),
        out_shape=jax.ShapeDtypeStruct((n_slots,), jnp.int32),
        name="moe_invperm",
    )(pos_flat)


def _row_copy(src_hbm, row, dst_vmem, slot, sem):
    return pltpu.make_async_copy(src_hbm.at[pl.ds(row, 1), :], dst_vmem.at[pl.ds(slot, 1), :], sem)


def _wait_rows(src_hbm, dst_vmem, sem, n):
    def wait(r, c):
        _row_copy(src_hbm, 0, dst_vmem, r, sem).wait()
        return c

    lax.fori_loop(0, n, wait, 0, unroll=SCALAR_UNROLL)


def _dispatch_body(inv_ref, nact_ref, h_hbm, o_ref, buf, sem, *, tm):
    i = pl.program_id(0)
    nact = nact_ref[0]

    def gather(tile, slot):
        def issue(rr, c):
            for u in range(DMA_QUEUES):
                r = rr * DMA_QUEUES + u
                _row_copy(h_hbm, inv_ref[tile * tm + r], buf.at[slot], r, sem.at[slot]).start(priority=u)
            return c

        lax.fori_loop(0, tm // DMA_QUEUES, issue, 0, unroll=SCALAR_UNROLL // DMA_QUEUES)

    ahead = GATHER_SLOTS - 1
    for first in range(ahead):
        @pl.when((i == 0) & (first < nact))
        def _():
            gather(first, first)

    for slot in range(GATHER_SLOTS):
        @pl.when((i + ahead < nact) & ((i + ahead) % GATHER_SLOTS == slot))
        def _():
            gather(i + ahead, slot)

    for slot in range(GATHER_SLOTS):
        @pl.when((i < nact) & (i % GATHER_SLOTS == slot))
        def _():
            _wait_rows(h_hbm, buf.at[slot], sem.at[slot], tm)
            o_ref[...] = buf[slot].astype(o_ref.dtype)

    @pl.when(i >= nact)
    def _():
        o_ref[...] = jnp.zeros(o_ref.shape, o_ref.dtype)


def _dispatch(h, inv, n_active_tiles, n_tiles, *, tm):
    d = h.shape[1]
    grid_spec = pltpu.PrefetchScalarGridSpec(
        num_scalar_prefetch=2,
        grid=(n_tiles,),
        in_specs=[pl.BlockSpec(memory_space=pl.ANY)],
        out_specs=pl.BlockSpec((tm, d), lambda i, inv, na: (i, 0)),
        scratch_shapes=[pltpu.VMEM((GATHER_SLOTS, tm, d), F32), pltpu.SemaphoreType.DMA((GATHER_SLOTS,))],
    )
    return pl.pallas_call(
        functools.partial(_dispatch_body, tm=tm),
        grid_spec=grid_spec,
        out_shape=jax.ShapeDtypeStruct((n_tiles * tm, d), BF16),
        compiler_params=_cp(("arbitrary",), 40),
        name="moe_dispatch",
    )(inv, n_active_tiles, h)


def _gmm_body(rt_ref, e_ref, jo_ref, jw_ref, first_ref, act_ref, slot_ref, ne_ref, nj_ref, hn_ref, nv_ref, x_ref,
              *refs, n_acc, tn, epilogue):
    w_hbm = refs[:n_acc]
    out = refs[n_acc]
    scratch = refs[n_acc + 1:]
    stage, sem, wbf = scratch[:n_acc], scratch[n_acc], scratch[n_acc + 1:]
    w = pl.program_id(0)

    def fetch(a, expert, col_tile, s):
        col = pl.multiple_of(col_tile * tn, tn)
        return pltpu.make_async_copy(w_hbm[a].at[expert, :, pl.ds(col, tn)], stage[a].at[s], sem.at[a, s])

    @pl.when(first_ref[w] == 1)
    def _():
        s = slot_ref[w]

        @pl.when(w == 0)
        def _():
            for a in range(n_acc):
                fetch(a, e_ref[0], jw_ref[0], 0).start()

        @pl.when(hn_ref[w] == 1)
        def _():
            for a in range(n_acc):
                fetch(a, ne_ref[w], nj_ref[w], 1 - s).start()

        for a in range(n_acc):
            fetch(a, e_ref[w], jw_ref[w], s).wait()
            wbf[a][...] = stage[a][s].astype(BF16)

    tm = x_ref.shape[0]
    for nsub in range(1, tm // MOE_SUB + 1):
        @pl.when((act_ref[w] == 1) & (nv_ref[w] == nsub))
        def _():
            rows = nsub * MOE_SUB
            x = x_ref[0:rows, :].astype(BF16)
            out[0:rows, :] = epilogue([_dot(x, wbf[a][...]) for a in range(n_acc)], ()).astype(out.dtype)
            if rows < tm:
                out[rows:, :] = jnp.zeros((tm - rows, out.shape[1]), out.dtype)

    @pl.when(act_ref[w] == 0)
    def _():
        out[...] = jnp.zeros(out.shape, out.dtype)


def _gmm(xs, weights, tables, *, tm, tn, out_dtype, epilogue, vmem_mb, name):
    p, kdim = xs.shape
    n_out = weights[0].shape[2]
    n_items = tables[0].shape[0]
    n_acc = len(weights)
    grid_spec = pltpu.PrefetchScalarGridSpec(
        num_scalar_prefetch=len(tables),
        grid=(n_items,),
        in_specs=[pl.BlockSpec((tm, kdim), lambda w, rt, *_: (rt[w], 0))]
        + [pl.BlockSpec(memory_space=pl.ANY) for _ in weights],
        out_specs=pl.BlockSpec((tm, tn), lambda w, rt, e, jo, *_: (rt[w], jo[w])),
        scratch_shapes=[pltpu.VMEM((2, kdim, tn), weights[0].dtype) for _ in weights]
        + [pltpu.SemaphoreType.DMA((n_acc, 2))]
        + [pltpu.VMEM((kdim, tn), BF16) for _ in weights],
    )
    return pl.pallas_call(
        functools.partial(_gmm_body, n_acc=n_acc, tn=tn, epilogue=epilogue),
        grid_spec=grid_spec,
        out_shape=jax.ShapeDtypeStruct((p, n_out), out_dtype),
        compiler_params=_cp(("arbitrary",), vmem_mb),
        name=name,
    )(*tables, xs, *weights)


def _work_tables(counts, n_col_tiles, tm, n_tiles):
    tiles_e = (counts + tm - 1) // tm
    tile_end = jnp.cumsum(tiles_e)
    tile_start = tile_end - tiles_e
    total_tiles = tile_end[-1]
    n_items = n_col_tiles * n_tiles
    item_end = n_col_tiles * tile_end
    w = jnp.arange(n_items, dtype=jnp.int32)
    n_active = n_col_tiles * total_tiles
    active = w < n_active
    wc = jnp.minimum(w, n_active - 1)
    e = jnp.sum((wc[:, None] >= item_end[None, :]).astype(jnp.int32), axis=1)
    e = jnp.minimum(e, N_EXPERTS - 1)
    te = jnp.maximum(tiles_e[e], 1)
    local = wc - n_col_tiles * tile_start[e]
    j = local // te
    q = local % te
    rt = tile_start[e] + q
    first = (q == 0) & active
    slot = (jnp.cumsum(first.astype(jnp.int32)) - 1) % 2
    nxt = lax.cummin(jnp.where(first, w, n_items), axis=0, reverse=True)
    nxt = jnp.concatenate([nxt[1:], jnp.full((1,), n_items, nxt.dtype)])
    has_next = nxt < n_items
    nxt = jnp.minimum(nxt, n_items - 1)
    spare = jnp.maximum(w - n_active, 0)
    rt = jnp.where(active, rt, total_tiles + spare // n_col_tiles)
    j_out = jnp.where(active, j, spare % n_col_tiles)
    rows_valid = jnp.clip(counts[e] - q * tm, 1, tm)
    n_sub = (rows_valid + MOE_SUB - 1) // MOE_SUB
    i32 = lambda a: a.astype(jnp.int32)
    return (i32(rt), i32(e), i32(j_out), i32(j), i32(first), i32(active), i32(slot), i32(e[nxt]), i32(j[nxt]),
            i32(has_next), i32(n_sub))


def _combine_body(pos_ref, x_ref, info_ref, g_ref, y_hbm, o_ref, buf, sem, *, tm):
    i = pl.program_id(0)
    n = pl.num_programs(0)

    def gather(tile, slot):
        def issue(r, c):
            for k in range(TOP_K):
                _row_copy(y_hbm, pos_ref[(tile * tm + r) * TOP_K + k], buf.at[slot, k], r,
                          sem.at[slot]).start(priority=k % DMA_QUEUES)
            return c

        lax.fori_loop(0, tm, issue, 0, unroll=SCALAR_UNROLL)

    ahead = GATHER_SLOTS - 1
    for first in range(ahead):
        @pl.when((i == 0) & (first < n))
        def _():
            gather(first, first)

    for slot in range(GATHER_SLOTS):
        @pl.when((i + ahead < n) & ((i + ahead) % GATHER_SLOTS == slot))
        def _():
            gather(i + ahead, slot)

    for slot in range(GATHER_SLOTS):
        @pl.when(i % GATHER_SLOTS == slot)
        def _():
            for k in range(TOP_K):
                _wait_rows(y_hbm, buf.at[slot, k], sem.at[slot], tm)
            info = info_ref[...]
            acc = x_ref[...] + info[:, 4:5] * buf[slot, 0] + info[:, 5:6] * buf[slot, 1]
            y = acc * lax.rsqrt(jnp.mean(acc * acc, axis=-1, keepdims=True) + NORM_EPS)
            o_ref[...] = y * g_ref[...]


def _combine(x, info, pos_flat, y_sorted, final_norm, *, tm=128):
    t, d = x.shape
    grid_spec = pltpu.PrefetchScalarGridSpec(
        num_scalar_prefetch=1,
        grid=(t // tm,),
        in_specs=[pl.BlockSpec((tm, d), lambda i, pos: (i, 0)),
                  pl.BlockSpec((tm, LANES), lambda i, pos: (i, 0)),
                  pl.BlockSpec((1, d), lambda i, pos: (0, 0)),
                  pl.BlockSpec(memory_space=pl.ANY)],
        out_specs=pl.BlockSpec((tm, d), lambda i, pos: (i, 0)),
        scratch_shapes=[pltpu.VMEM((GATHER_SLOTS, TOP_K, tm, d), F32), pltpu.SemaphoreType.DMA((GATHER_SLOTS,))],
    )
    return pl.pallas_call(
        functools.partial(_combine_body, tm=tm),
        grid_spec=grid_spec,
        out_shape=jax.ShapeDtypeStruct((t, d), F32),
        compiler_params=_cp(("arbitrary",), 40),
        name="moe_combine",
    )(pos_flat, x, info, final_norm.reshape(1, d).astype(F32), y_sorted)


def _even_mixers(x, tables, w_in, attn_norm, q_norm, w_uq, kv_norm, w_ukv, gate_bias, head_norm, batch, seq):
    t = x.shape[0]
    c_tab, s1_tab, s2_tab = tables
    n_ml_main = 2 * ML_HEADS * ML_QK + 2 * ML_HEADS * ML_V
    gate_lo = MLA_IN + n_ml_main
    w_t = w_in.T
    w_a_t = jnp.concatenate(
        [w_t[:MLA_IN], jnp.zeros((LANES - MLA_ROPE, D_MODEL), w_t.dtype),
         w_t[gate_lo:], jnp.zeros((LANES - 2 * ML_HEADS, D_MODEL), w_t.dtype)], axis=0)
    n_a = w_a_t.shape[0]
    w_q = jnp.pad(w_uq.reshape(MLA_Q_LORA, MLA_HEADS, MLA_NOPE + MLA_ROPE),
                  ((0, 0), (0, 0), (0, MLA_QK_PAD - MLA_NOPE - MLA_ROPE))).reshape(MLA_Q_LORA, -1).astype(BF16)
    w_kv3 = w_ukv.reshape(MLA_KV_LORA, MLA_HEADS, MLA_NOPE + MLA_V)
    w_k = jnp.pad(w_kv3[:, :, :MLA_NOPE], ((0, 0), (0, 0), (0, MLA_QK_PAD - MLA_NOPE))).reshape(MLA_KV_LORA, -1).astype(BF16)
    w_v = w_kv3[:, :, MLA_NOPE:].reshape(MLA_KV_LORA, -1).astype(BF16)

    xn = _rmsnorm(x, attn_norm, width=D_MODEL)
    u_a = _matmul_ws([xn], [[(w_a_t, 0)]], n_out=n_a, tm=1024, tn=256, out_dtype=F32, epilogue=_epi_plain,
                     transposed=True, name="even_in_a")
    u_b = _matmul_ws([xn], [[(w_t, 0)]], n_out=n_ml_main, tm=1024, tn=512, out_dtype=BF16, epilogue=_epi_plain,
                     col_offset=MLA_IN, transposed=True, name="even_in_b")

    cqn = _rmsnorm(u_a, q_norm, width=MLA_Q_LORA, col_block=0)
    ckvn = _rmsnorm(u_a, kv_norm, width=MLA_KV_LORA, col_block=MLA_Q_LORA // MLA_KV_LORA)
    tm_p = min(1024, t)
    tab_specs = [(tab, (tm_p, LANES), lambda i, j: (i, 0)) for tab in (c_tab, s1_tab, s2_tab)]
    n_qk = MLA_HEADS * MLA_QK_PAD
    q_full = _matmul_ws([cqn], [[(w_q, 0)]], n_out=n_qk, tm=tm_p, tn=1024, out_dtype=BF16, epilogue=_epi_mla_q,
                        extras=tab_specs, name="mla_q")
    kr_spec = (u_a, (tm_p, LANES), lambda i, j: (i, (MLA_Q_LORA + MLA_KV_LORA) // LANES))
    k_full = _matmul_ws([ckvn], [[(w_k, 0)]], n_out=n_qk, tm=tm_p, tn=1024, out_dtype=BF16, epilogue=_epi_mla_k,
                        extras=[kr_spec] + tab_specs, name="mla_k")
    v = _matmul_ws([ckvn], [[(w_v, 0)]], n_out=MLA_HEADS * MLA_V, tm=tm_p, tn=1024, out_dtype=BF16,
                   epilogue=_epi_plain, name="mla_v")
    y_a = _mla_attention(q_full, k_full, v, batch, seq)

    bias_pad = jnp.concatenate([gate_bias.astype(F32), jnp.zeros((LANES - 2 * ML_HEADS,), F32)]).reshape(1, LANES)
    y_b = _mlstm(u_b, u_a, (MLA_IN + LANES - MLA_ROPE) // LANES, bias_pad, head_norm, batch, seq)
    return y_a, y_b


def _even_layer(x, tables, w_in, attn_norm, q_norm, w_uq, kv_norm, w_ukv, gate_bias, head_norm, w_out,
                ffn_norm, w_gate, w_up, w_down, batch, seq):
    y_a, y_b = _even_mixers(x, tables, w_in, attn_norm, q_norm, w_uq, kv_norm, w_ukv, gate_bias, head_norm,
                            batch, seq)
    tm = min(1024, x.shape[0])
    res_spec = lambda arr, tn: [(arr, (tm, tn), lambda i, j: (i, j))]
    x1 = _matmul_ws([y_a, y_b], [[(w_out, 0), (w_out, 1)]], n_out=D_MODEL, tm=tm, tn=512, out_dtype=F32,
                    epilogue=_epi_residual, extras=res_spec(x, 512), name="even_out")
    hn = _rmsnorm(x1, ffn_norm, width=D_MODEL)
    a = _matmul_ws([hn], [[(w_gate, 0)], [(w_up, 0)]], n_out=D_FF, tm=min(2048, x.shape[0]), tn=256, out_dtype=BF16,
                   epilogue=_epi_swiglu, vmem_mb=58, name="ffn_gate_up")
    tm_d = min(2048, x.shape[0])
    x2 = _matmul([a], [[(w_down, 0)]], n_out=D_MODEL, tm=tm_d, tn=1024, tk=512, out_dtype=F32, epilogue=_epi_residual,
                 extras=[(x1, (tm_d, 1024), lambda i, j: (i, j))], vmem_mb=58, name="ffn_down")
    return x2


def _odd_layer(x, w_in, attn_norm, conv_w, w_out, ffn_norm, w_router, b_router, w_gate_e, w_up_e, w_down_e,
               final_norm, batch, seq):
    t = x.shape[0]
    xn = _rmsnorm(x, attn_norm, width=D_MODEL)
    n_in = w_in.shape[1]
    tm = min(1024, t)
    u = _matmul_ws([xn], [[(w_in, 0)]], n_out=n_in, tm=tm, tn=512, out_dtype=BF16, epilogue=_epi_plain,
                   name="odd_in")
    y_c = _short_conv(u, conv_w, batch, seq)
    y_d = _sb_attention(u, 3 * SC_WIDTH // LANES, batch, seq)
    x1 = _matmul_ws([y_c, y_d], [[(w_out, 0), (w_out, 1)]], n_out=D_MODEL, tm=tm, tn=512, out_dtype=F32,
                    epilogue=_epi_residual, extras=[(x, (tm, 512), lambda i, j: (i, j))], name="odd_out")
    return _moe(x1, ffn_norm, w_router, b_router, w_gate_e, w_up_e, w_down_e, final_norm)


def _moe(x1, ffn_norm, w_router, b_router, w_gate_e, w_up_e, w_down_e, final_norm):
    t = x1.shape[0]
    h, info, cnt = _route(x1, ffn_norm, w_router, b_router)
    tm = MOE_TM
    n_tiles = (t * TOP_K) // tm + N_EXPERTS
    counts = cnt[0, :N_EXPERTS].astype(jnp.int32)
    tiles_e = (counts + tm - 1) // tm
    offs = (jnp.cumsum(tiles_e) - tiles_e) * tm
    idx = info[:, 0:TOP_K].astype(jnp.int32)
    rank = info[:, TOP_K:2 * TOP_K].astype(jnp.int32)
    pos_flat = (offs[idx] + rank).reshape(-1)
    n_active = jnp.sum(tiles_e).astype(jnp.int32).reshape(1)

    inv = _invperm(pos_flat, n_tiles * tm)
    xs = _dispatch(h, inv, n_active, n_tiles, tm=tm)
    tn_gu, tn_d = 512, 1024
    tab_gu = _work_tables(counts, D_FF_EXPERT // tn_gu, tm, n_tiles)
    a_s = _gmm(xs, [w_gate_e, w_up_e], tab_gu, tm=tm, tn=tn_gu, out_dtype=BF16, epilogue=_epi_swiglu,
               vmem_mb=58, name="moe_gate_up")
    tab_d = _work_tables(counts, D_MODEL // tn_d, tm, n_tiles)
    y_s = _gmm(a_s, [w_down_e], tab_d, tm=tm, tn=tn_d, out_dtype=F32, epilogue=_epi_plain,
               vmem_mb=58, name="moe_down")
    return _combine(x1, info, pos_flat, y_s, final_norm)


def kernel(x, positions, even_attn_norm, even_w_in, even_q_norm, even_w_uq, even_kv_norm, even_w_ukv, even_ml_gate_bias, even_ml_head_norm, even_w_out, even_ffn_norm, even_w_gate, even_w_up, even_w_down, odd_attn_norm, odd_w_in, odd_conv_w, odd_w_out, odd_ffn_norm, odd_w_router, odd_b_router, odd_w_gate_e, odd_w_up_e, odd_w_down_e, final_norm):
    batch, seq, d = x.shape
    assert even_w_in.shape[0] == 1 and odd_w_in.shape[0] == 1, "kernel is written for one even and one odd layer"
    xf = x.reshape(batch * seq, d)
    tables = _rope_tables(positions)
    xf = _even_layer(xf, tables, even_w_in[0], even_attn_norm[0], even_q_norm[0], even_w_uq[0], even_kv_norm[0],
                     even_w_ukv[0], even_ml_gate_bias[0], even_ml_head_norm[0], even_w_out[0], even_ffn_norm[0],
                     even_w_gate[0], even_w_up[0], even_w_down[0], batch, seq)
    out = _odd_layer(xf, odd_w_in[0], odd_attn_norm[0], odd_conv_w[0], odd_w_out[0], odd_ffn_norm[0],
                     odd_w_router[0], odd_b_router[0], odd_w_gate_e[0], odd_w_up_e[0], odd_w_down_e[0],
                     final_norm, batch, seq)
    return out.reshape(batch, seq, d)
```
